```python
import jax, jax.numpy as jnp
from jax import lax
import numpy as np

D_MODEL = 1024
BATCH = 8
SEQ = 2048
DEPTH = 1
DEC_BATCH = 128
DEC_SEQ = 4
PAST_LEN = 16384
PAGE_SIZE = 128

D_MIX = D_MODEL
W_A = D_MIX // 2
W_B = D_MIX - W_A
N_HEADS_A = 8
HEAD_A = W_A // N_HEADS_A
CONV_W = 4
LRU_C = 8.0
POOL_WINDOWS = (2, 4, 8, 16)
N_GROUPS_B = len(POOL_WINDOWS)
GROUP_B = W_B // N_GROUPS_B
POOL_BUF = max(POOL_WINDOWS) - 1
N_EXPERTS = 32
TOP_K = 4
D_FF = D_MODEL
SWIGLU_LIMIT = 7.0
SWIGLU_ALPHA = 1.702
PLE_DIM = 256
EPS = 1e-6

kernel_name = "hybrid_rglru_pool_moe_decode_step"


def rms_norm(x, g):
    xf = x.astype(jnp.float32)
    y = xf * lax.rsqrt(jnp.mean(xf * xf, axis=-1, keepdims=True) + EPS)
    return (y * g.astype(jnp.float32)).astype(x.dtype)


def causal_depthwise_conv(x, buf, w, b):
    S = x.shape[1]
    ext = jnp.concatenate([buf.astype(x.dtype), x], axis=1)
    y = b
    for k in range(CONV_W):
        y = y + ext[:, k:k + S] * w[k]
    return y, ext[:, ext.shape[1] - (CONV_W - 1):]


def _lru_combine(left, right):
    a1, b1 = left
    a2, b2 = right
    return a1 * a2, a2 * b1 + b2


def rg_lru(xc, h0, w_rgate, b_rgate, w_igate, b_igate, lru_lambda):
    B, S, _ = xc.shape
    f32 = jnp.float32
    xf = xc.astype(f32)
    xh = xf.reshape(B, S, N_HEADS_A, HEAD_A)
    r = jax.nn.sigmoid(jnp.einsum('bshi,hij->bshj', xh, w_rgate.astype(f32)).reshape(B, S, W_A) + b_rgate.astype(f32))
    i = jax.nn.sigmoid(jnp.einsum('bshi,hij->bshj', xh, w_igate.astype(f32)).reshape(B, S, W_A) + b_igate.astype(f32))
    log_a = -LRU_C * r * jax.nn.softplus(-lru_lambda.astype(f32))
    a = jnp.exp(log_a)
    bt = jnp.sqrt(-jnp.expm1(2.0 * log_a)) * (i * xf)
    a_cum, b_cum = lax.associative_scan(_lru_combine, (a, bt), axis=1)
    h = a_cum * h0.astype(f32)[:, None, :] + b_cum
    return h, h[:, -1]


def multiscale_pool(xb, buf, start, pool_w, pool_b, pool_scale):
    B, S, _ = xb.shape
    f32 = jnp.float32
    ext = jnp.concatenate([buf.astype(f32), xb.astype(f32)], axis=1)
    cs = jnp.concatenate([jnp.zeros((B, 1, W_B), f32), jnp.cumsum(ext, axis=1)], axis=1)
    pos = start + jnp.arange(S)
    outs = []
    for g, w in enumerate(POOL_WINDOWS):
        sl = slice(g * GROUP_B, (g + 1) * GROUP_B)
        win = cs[:, POOL_BUF + 1:, sl] - cs[:, POOL_BUF + 1 - w:POOL_BUF + 1 - w + S, sl]
        cnt = jnp.minimum(w, pos + 1).astype(f32)[None, :, None]
        outs.append(win / cnt - ext[:, POOL_BUF:, sl])
    d = jnp.stack(outs, axis=2)
    y = jnp.einsum('bsgi,gij->bsgj', d, pool_w.astype(f32)).reshape(B, S, W_B) + pool_b.astype(f32)
    y = y * pool_scale.astype(f32)
    return y.astype(xb.dtype), ext[:, ext.shape[1] - POOL_BUF:]


def moe(x, w_router, b_router, w_gate, b_gate, w_up, b_up, w_down, b_down):
    logits = (x @ w_router + b_router).astype(jnp.float32)
    vals, idx = lax.top_k(logits, TOP_K)
    probs = jax.nn.softmax(vals, axis=-1)
    combine = jnp.sum(jax.nn.one_hot(idx, N_EXPERTS, dtype=jnp.float32) * probs[..., None], axis=1)
    out = jnp.zeros(x.shape, jnp.float32)
    for e in range(N_EXPERTS):
        g = jnp.minimum(x @ w_gate[e] + b_gate[e], SWIGLU_LIMIT)
        u = jnp.clip(x @ w_up[e] + b_up[e], -SWIGLU_LIMIT, SWIGLU_LIMIT)
        hid = (u + 1.0) * (g * jax.nn.sigmoid(SWIGLU_ALPHA * g))
        y = hid @ w_down[e] + b_down[e]
        out = out + combine[:, e:e + 1] * y.astype(jnp.float32)
    return out.astype(x.dtype)


def layer(h, p, lru_h0, conv_buf0, pool_buf0, start,
          norm_mix_pre, w_in, conv_w, conv_b, w_rgate, b_rgate, w_igate, b_igate, lru_lambda,
          pool_w, pool_b, pool_scale, norm_group_a, norm_group_b, w_out, norm_mix_post,
          norm_ffn_pre, w_router, b_router, w_gate, b_gate, w_up, b_up, w_down, b_down, norm_ffn_post,
          w_ple, w_ple_gate, b_ple_gate, norm_ple):
    B, S, D = h.shape
    u = rms_norm(h, norm_mix_pre)
    z = u @ w_in
    xa, ga, xb = z[..., :W_A], z[..., W_A:2 * W_A], z[..., 2 * W_A:]
    xc, conv_buf = causal_depthwise_conv(xa, conv_buf0, conv_w, conv_b)
    hs, lru_h = rg_lru(xc, lru_h0, w_rgate, b_rgate, w_igate, b_igate, lru_lambda)
    ya = hs.astype(h.dtype) * jax.nn.gelu(ga)
    yb, pool_buf = multiscale_pool(xb, pool_buf0, start, pool_w, pool_b, pool_scale)
    m = jnp.concatenate([rms_norm(ya, norm_group_a), rms_norm(yb, norm_group_b)], axis=-1) @ w_out
    h = h + rms_norm(m, norm_mix_post)
    v = rms_norm(h, norm_ffn_pre).reshape(B * S, D)
    f = moe(v, w_router, b_router, w_gate, b_gate, w_up, b_up, w_down, b_down).reshape(B, S, D)
    h = h + rms_norm(f, norm_ffn_post)
    pe = (p.astype(h.dtype) @ w_ple) * jax.nn.sigmoid(h @ w_ple_gate + b_ple_gate)
    h = h + rms_norm(pe, norm_ple)
    return h, lru_h, conv_buf, pool_buf


def setup_inputs(seed: int = 0) -> dict:
    key = jax.random.key(seed)
    ks = iter(jax.random.split(key, 48))
    f32 = jnp.float32
    L = DEPTH

    def nrm(shape, scale):
        return jax.random.normal(next(ks), shape, f32) * scale

    def gain(shape):
        return 1.0 + 0.05 * jax.random.normal(next(ks), shape, f32)

    x_prompt = nrm((BATCH, SEQ, D_MODEL), 1.0)
    x_sample = nrm((DEC_BATCH, DEC_SEQ, D_MODEL), 1.0)
    state_rglru_h = nrm((L, DEC_BATCH, W_A), 0.5)
    state_rglru_conv = nrm((L, DEC_BATCH, CONV_W - 1, W_A), 1.0)
    state_pool = nrm((L, DEC_BATCH, POOL_BUF, W_B), 1.0)
    p_prompt = nrm((L, BATCH, SEQ, PLE_DIM), 1.0)
    p_sample = nrm((L, DEC_BATCH, DEC_SEQ, PLE_DIM), 1.0)
    norm_mix_pre = gain((L, D_MODEL))
    w_in = nrm((L, D_MODEL, 2 * W_A + W_B), D_MODEL ** -0.5)
    conv_w = nrm((L, CONV_W, W_A), CONV_W ** -0.5)
    conv_b = nrm((L, W_A), 0.02)
    w_rgate = nrm((L, N_HEADS_A, HEAD_A, HEAD_A), HEAD_A ** -0.5)
    b_rgate = nrm((L, W_A), 0.02)
    w_igate = nrm((L, N_HEADS_A, HEAD_A, HEAD_A), HEAD_A ** -0.5)
    b_igate = nrm((L, W_A), 0.02)
    a_c = jax.random.uniform(next(ks), (L, W_A), f32, minval=0.9, maxval=0.999)
    a0 = a_c ** (1.0 / LRU_C)
    lru_lambda = jnp.log(a0) - jnp.log1p(-a0)
    pool_w = nrm((L, N_GROUPS_B, GROUP_B, GROUP_B), GROUP_B ** -0.5)
    pool_b = nrm((L, W_B), 0.02)
    pool_scale = 1.0 + 0.1 * jax.random.normal(next(ks), (L, W_B), f32)
    norm_group_a = gain((L, W_A))
    norm_group_b = gain((L, W_B))
    w_out = nrm((L, D_MIX, D_MODEL), D_MIX ** -0.5)
    norm_mix_post = gain((L, D_MODEL))
    norm_ffn_pre = gain((L, D_MODEL))
    w_router = nrm((L, D_MODEL, N_EXPERTS), D_MODEL ** -0.5)
    b_router = nrm((L, N_EXPERTS), 0.01)
    w_gate = nrm((L, N_EXPERTS, D_MODEL, D_FF), D_MODEL ** -0.5)
    b_gate = nrm((L, N_EXPERTS, D_FF), 0.02)
    w_up = nrm((L, N_EXPERTS, D_MODEL, D_FF), D_MODEL ** -0.5)
    b_up = nrm((L, N_EXPERTS, D_FF), 0.02)
    w_down = nrm((L, N_EXPERTS, D_FF, D_MODEL), D_FF ** -0.5)
    b_down = nrm((L, N_EXPERTS, D_MODEL), 0.02)
    norm_ffn_post = gain((L, D_MODEL))
    w_ple = nrm((L, PLE_DIM, D_MODEL), PLE_DIM ** -0.5)
    w_ple_gate = nrm((L, D_MODEL, D_MODEL), D_MODEL ** -0.5)
    b_ple_gate = nrm((L, D_MODEL), 0.02)
    norm_ple = gain((L, D_MODEL))
    return {
        "x_prompt": x_prompt, "x_sample": x_sample,
        "state_rglru_h": state_rglru_h, "state_rglru_conv": state_rglru_conv, "state_pool": state_pool,
        "p_prompt": p_prompt, "p_sample": p_sample,
        "norm_mix_pre": norm_mix_pre, "w_in": w_in, "conv_w": conv_w, "conv_b": conv_b,
        "w_rgate": w_rgate, "b_rgate": b_rgate, "w_igate": w_igate, "b_igate": b_igate,
        "lru_lambda": lru_lambda, "pool_w": pool_w, "pool_b": pool_b, "pool_scale": pool_scale,
        "norm_group_a": norm_group_a, "norm_group_b": norm_group_b, "w_out": w_out,
        "norm_mix_post": norm_mix_post, "norm_ffn_pre": norm_ffn_pre,
        "w_router": w_router, "b_router": b_router, "w_gate": w_gate, "b_gate": b_gate,
        "w_up": w_up, "b_up": b_up, "w_down": w_down, "b_down": b_down,
        "norm_ffn_post": norm_ffn_post, "w_ple": w_ple, "w_ple_gate": w_ple_gate,
        "b_ple_gate": b_ple_gate, "norm_ple": norm_ple,
    }


def reference(x_prompt, x_sample, state_rglru_h, state_rglru_conv, state_pool, p_prompt, p_sample,
              norm_mix_pre, w_in, conv_w, conv_b, w_rgate, b_rgate, w_igate, b_igate, lru_lambda,
              pool_w, pool_b, pool_scale, norm_group_a, norm_group_b, w_out, norm_mix_post,
              norm_ffn_pre, w_router, b_router, w_gate, b_gate, w_up, b_up, w_down, b_down,
              norm_ffn_post, w_ple, w_ple_gate, b_ple_gate, norm_ple):
    dt = x_prompt.dtype
    hp, hs = x_prompt, x_sample
    hp_l, cp_l, pp_l, hs_l, cs_l, ps_l = [], [], [], [], [], []
    for i in range(DEPTH):
        lw = (norm_mix_pre[i], w_in[i], conv_w[i], conv_b[i], w_rgate[i], b_rgate[i], w_igate[i], b_igate[i],
              lru_lambda[i], pool_w[i], pool_b[i], pool_scale[i], norm_group_a[i], norm_group_b[i], w_out[i],
              norm_mix_post[i], norm_ffn_pre[i], w_router[i], b_router[i], w_gate[i], b_gate[i], w_up[i],
              b_up[i], w_down[i], b_down[i], norm_ffn_post[i], w_ple[i], w_ple_gate[i], b_ple_gate[i],
              norm_ple[i])
        hp, lru_p, conv_p, pool_p = layer(
            hp, p_prompt[i],
            jnp.zeros((hp.shape[0], W_A), dt),
            jnp.zeros((hp.shape[0], CONV_W - 1, W_A), dt),
            jnp.zeros((hp.shape[0], POOL_BUF, W_B), dt),
            0, *lw)
        hs, lru_s, conv_s, pool_s = layer(
            hs, p_sample[i], state_rglru_h[i], state_rglru_conv[i], state_pool[i], PAST_LEN, *lw)
        hp_l.append(lru_p); cp_l.append(conv_p); pp_l.append(pool_p)
        hs_l.append(lru_s); cs_l.append(conv_s); ps_l.append(pool_s)
    new_h_prompt = jnp.stack(hp_l)
    new_conv_prompt = jnp.stack(cp_l)
    new_pool_prompt = jnp.stack(pp_l)
    new_h_sample = jnp.stack(hs_l)
    new_conv_sample = jnp.stack(cs_l)
    new_pool_sample = jnp.stack(ps_l)
    return (hp, hs, new_h_prompt, new_conv_prompt, new_pool_prompt, new_h_sample, new_conv_sample, new_pool_sample)
```

```python
import functools

import jax
import jax.numpy as jnp
from jax import lax
from jax.experimental import pallas as pl
from jax.experimental.pallas import tpu as pltpu

D_MODEL = 1024
W_A = 512
W_B = 512
N_HEADS_A = 8
HEAD_A = W_A // N_HEADS_A
CONV_W = 4
LRU_C = 8.0
POOL_WINDOWS = (2, 4, 8, 16)
GROUP_B = W_B // len(POOL_WINDOWS)
POOL_BUF = max(POOL_WINDOWS) - 1
N_EXPERTS = 32
TOP_K = 4
D_FF = 1024
SWIGLU_LIMIT = 7.0
SWIGLU_ALPHA = 1.702
PLE_DIM = 256
EPS = 1e-6
PAST_LEN = 16384

LANES = 128
MXU_DIM = 256
HALF = D_MODEL // 2

MIX_ROWS = 512
ROUTE_ROWS = 512
DISP_ROWS = 512
FFN_ROWS = 512
FIN_ROWS = 256
VMEM_LIMIT = 56 * 1024 * 1024

f32 = jnp.float32
bf16 = jnp.bfloat16
u32 = jnp.uint32
i32 = jnp.int32


def _rms(x, g):
    return x * lax.rsqrt(jnp.mean(x * x, axis=-1, keepdims=True) + EPS) * g


def _split(x):
    hi = x.astype(bf16)
    lo = (x - hi.astype(f32)).astype(bf16)
    return hi, lo


def _dot(a, b):
    return jnp.dot(a, b, preferred_element_type=f32)


def _dot3(x, w_hi, w_lo):
    x_hi, x_lo = _split(x)
    return _dot(x_hi, w_hi) + (_dot(x_hi, w_lo) + _dot(x_lo, w_hi))


def _pack_rows(x):
    bits = lax.bitcast_convert_type(x.astype(bf16).astype(f32), u32)
    return (bits[:, HALF:] & jnp.uint32(0xFFFF0000)) | (bits[:, :HALF] >> 16)


def _unpack_rows(p):
    lo = lax.bitcast_convert_type(p << 16, f32)
    hi = lax.bitcast_convert_type(p & jnp.uint32(0xFFFF0000), f32)
    return jnp.concatenate([lo, hi], axis=-1)


def _mixer_kernel(x_ref, conv0_ref, pool0_ref, h0_ref, g_pre_ref, w_in_ref, conv_w_ref, conv_b_ref,
                  wr_hi_ref, wr_lo_ref, br_ref, wi_hi_ref, wi_lo_ref, bi_ref, lam_ref,
                  pool_w_ref, pool_b_ref, pool_scale_ref, ng_a_ref, ng_b_ref, w_out_ref, g_post_ref,
                  h1_ref, hfin_ref, convfin_ref, poolfin_ref,
                  h_s, conv_s, pool_s, a_s, b_s, hs_s, *, bt, tt, start):
    step = pl.program_id(0)
    rows = bt * tt

    @pl.when(step == 0)
    def _():
        h_s[...] = h0_ref[...]
        conv_s[...] = conv0_ref[...]
        pool_s[...] = pool0_ref[...]

    x = x_ref[...]
    u = _rms(x, g_pre_ref[...]).astype(bf16)
    z = _dot(u, w_in_ref[...])
    xa, ga, xb = z[:, :W_A], z[:, W_A:2 * W_A], z[:, 2 * W_A:]

    ext_a = jnp.concatenate([conv_s[...], xa], axis=0)
    xc = conv_b_ref[...]
    for k in range(CONV_W):
        xc = xc + ext_a[k * bt:k * bt + rows] * conv_w_ref[k:k + 1, :]
    conv_s[...] = ext_a[rows:]

    def gate(w_hi_ref, w_lo_ref, b_ref):
        parts = [_dot3(xc[:, j * MXU_DIM:(j + 1) * MXU_DIM], w_hi_ref[j], w_lo_ref[j])
                 for j in range(W_A // MXU_DIM)]
        return jax.nn.sigmoid(jnp.concatenate(parts, axis=-1) + b_ref[...])

    r = gate(wr_hi_ref, wr_lo_ref, br_ref)
    ig = gate(wi_hi_ref, wi_lo_ref, bi_ref)
    lam = lam_ref[...]
    softplus_neg = jnp.maximum(-lam, 0.0) + jnp.log1p(jnp.exp(-jnp.abs(lam)))
    log_a = (-LRU_C) * r * softplus_neg
    a_s[...] = jnp.exp(log_a)
    th = jnp.tanh(log_a)
    b_s[...] = jnp.sqrt(-2.0 * th / (1.0 - th)) * (ig * xc)

    def scan_step(t, h):
        sl = pl.ds(pl.multiple_of(t * bt, bt), bt)
        h = a_s[sl, :] * h + b_s[sl, :]
        hs_s[sl, :] = h
        return h

    h_last = lax.fori_loop(0, tt, scan_step, h_s[...], unroll=True)
    h_s[...] = h_last
    ya = hs_s[...] * jax.nn.gelu(ga)

    ext_b = jnp.concatenate([pool_s[...], xb], axis=0)
    pool_s[...] = ext_b[rows:]
    s2 = ext_b[bt:, :] + ext_b[:-bt, :]
    s4 = s2[2 * bt:, GROUP_B:] + s2[:-2 * bt, GROUP_B:]
    s8 = s4[4 * bt:, GROUP_B:] + s4[:-4 * bt, GROUP_B:]
    s16 = s8[8 * bt:, GROUP_B:] + s8[:-8 * bt, GROUP_B:]
    wins = (s2[14 * bt:, :GROUP_B], s4[12 * bt:, :GROUP_B], s8[8 * bt:, :GROUP_B], s16)
    t_idx = lax.broadcasted_iota(i32, (rows, GROUP_B), 0) // bt
    pos1 = (t_idx + (step * tt + start + 1)).astype(f32)
    yb_parts = []
    for g, w in enumerate(POOL_WINDOWS):
        cnt = jnp.minimum(jnp.float32(w), pos1)
        d = wins[g] / cnt - xb[:, g * GROUP_B:(g + 1) * GROUP_B]
        yb_parts.append(_dot(d.astype(bf16), pool_w_ref[g]))
    yb = (jnp.concatenate(yb_parts, axis=-1) + pool_b_ref[...]) * pool_scale_ref[...]

    na = _rms(ya, ng_a_ref[...])
    nb = _rms(yb, ng_b_ref[...])
    m = _dot(jnp.concatenate([na, nb], axis=-1).astype(bf16), w_out_ref[...])
    h1_ref[...] = x + _rms(m, g_post_ref[...])

    hfin_ref[...] = h_last
    convfin_ref[...] = conv_s[...]
    poolfin_ref[...] = pool_s[...]


def _full(shape):
    return pl.BlockSpec(shape, lambda i: (0,) * len(shape))


def _mixer(x_tm, conv0, pool0, h0, wts, *, bt, start):
    n = x_tm.shape[0]
    tt = MIX_ROWS // bt
    rows = bt * tt
    assert n % rows == 0
    kern = functools.partial(_mixer_kernel, bt=bt, tt=tt, start=start)
    small = [conv0, pool0, h0] + list(wts)
    return pl.pallas_call(
        kern,
        grid=(n // rows,),
        in_specs=[pl.BlockSpec((rows, D_MODEL), lambda i: (i, 0))] + [_full(a.shape) for a in small],
        out_specs=[pl.BlockSpec((rows, D_MODEL), lambda i: (i, 0)),
                   _full((bt, W_A)), _full(((CONV_W - 1) * bt, W_A)), _full((POOL_BUF * bt, W_B))],
        out_shape=[jax.ShapeDtypeStruct((n, D_MODEL), f32),
                   jax.ShapeDtypeStruct((bt, W_A), f32),
                   jax.ShapeDtypeStruct(((CONV_W - 1) * bt, W_A), f32),
                   jax.ShapeDtypeStruct((POOL_BUF * bt, W_B), f32)],
        scratch_shapes=[pltpu.VMEM((bt, W_A), f32),
                        pltpu.VMEM(((CONV_W - 1) * bt, W_A), f32),
                        pltpu.VMEM((POOL_BUF * bt, W_B), f32),
                        pltpu.VMEM((rows, W_A), f32),
                        pltpu.VMEM((rows, W_A), f32),
                        pltpu.VMEM((rows, W_A), f32)],
        compiler_params=pltpu.CompilerParams(dimension_semantics=("arbitrary",),
                                             vmem_limit_bytes=VMEM_LIMIT),
        name="mixer",
    )(x_tm, *small)


def _router_kernel(h1_ref, g_ref, wr_hi_ref, wr_lo_ref, br_ref, tri_ref,
                   vpk_ref, idx_ref, rank_ref, pw_ref, cnt_ref, carry_s):
    step = pl.program_id(0)
    rows = h1_ref.shape[0]

    @pl.when(step == 0)
    def _():
        carry_s[...] = jnp.zeros_like(carry_s)

    v = _rms(h1_ref[...], g_ref[...])
    vpk_ref[...] = _pack_rows(v)
    logits = _dot3(v, wr_hi_ref[...], wr_lo_ref[...]) + br_ref[...]
    lt = jnp.transpose(logits)[:N_EXPERTS, :]

    eio = lax.broadcasted_iota(i32, (N_EXPERTS, rows), 0).astype(f32)
    work = lt
    vals, idxs, sels = [], [], []
    for _ in range(TOP_K):
        m = jnp.max(work, axis=0, keepdims=True)
        ik = jnp.min(jnp.where(work == m, eio, float(N_EXPERTS)), axis=0, keepdims=True)
        sel = eio == ik
        vals.append(m)
        idxs.append(ik)
        sels.append(sel)
        work = jnp.where(sel, -jnp.inf, work)
    exps = [jnp.exp(val - vals[0]) for val in vals]
    denom = exps[0] + exps[1] + exps[2] + exps[3]
    probs = [e / denom for e in exps]

    chosen = sels[0] | sels[1] | sels[2] | sels[3]
    onehot = jnp.where(chosen, 1.0, 0.0)
    before = _dot(onehot.astype(bf16), tri_ref[...])
    base = carry_s[...] + before
    ranks = [jnp.sum(jnp.where(sel, base, 0.0), axis=0, keepdims=True) for sel in sels]
    carry_s[...] = carry_s[...] + jnp.sum(onehot, axis=1, keepdims=True)

    idx_ref[...] = jnp.concatenate(idxs, axis=0).astype(i32)
    rank_ref[...] = jnp.concatenate(ranks, axis=0).astype(i32)
    pad = jnp.zeros((LANES - TOP_K, rows), f32)
    pw_ref[...] = jnp.transpose(jnp.concatenate(probs + [pad], axis=0))
    cnt_ref[...] = carry_s[:, :LANES].astype(i32)


def _router(h1, g, wr_hi, wr_lo, br, tri):
    t = h1.shape[0]
    rows = ROUTE_ROWS
    assert t % rows == 0
    return pl.pallas_call(
        _router_kernel,
        grid=(t // rows,),
        in_specs=[pl.BlockSpec((rows, D_MODEL), lambda i: (i, 0)),
                  _full(g.shape), _full(wr_hi.shape), _full(wr_lo.shape), _full(br.shape), _full(tri.shape)],
        out_specs=[pl.BlockSpec((rows, HALF), lambda i: (i, 0)),
                   pl.BlockSpec((TOP_K, rows), lambda i: (0, i)),
                   pl.BlockSpec((TOP_K, rows), lambda i: (0, i)),
                   pl.BlockSpec((rows, LANES), lambda i: (i, 0)),
                   _full((N_EXPERTS, LANES))],
        out_shape=[jax.ShapeDtypeStruct((t, HALF), u32),
                   jax.ShapeDtypeStruct((TOP_K, t), i32),
                   jax.ShapeDtypeStruct((TOP_K, t), i32),
                   jax.ShapeDtypeStruct((t, LANES), f32),
                   jax.ShapeDtypeStruct((N_EXPERTS, LANES), i32)],
        scratch_shapes=[pltpu.VMEM((N_EXPERTS, rows), f32)],
        compiler_params=pltpu.CompilerParams(dimension_semantics=("arbitrary",),
                                             vmem_limit_bytes=VMEM_LIMIT),
        name="router",
    )(h1, g, wr_hi, wr_lo, br, tri)


def _row_copy(src, src_row, dst, dst_row, sem):
    return pltpu.make_async_copy(src.at[pl.ds(src_row, 1)], dst.at[pl.ds(dst_row, 1)], sem)


def _dispatch_kernel(off_ref, fill_ref, idx_ref, rank_ref, vpk_ref, zeros_ref, xs_ref, sem, fill_sem):
    step = pl.program_id(0)
    rows = idx_ref.shape[1]

    @pl.when(step == 0)
    def _():
        n_tiles = xs_ref.shape[0] // FFN_ROWS

        def fill_copy(i):
            start = pl.multiple_of(i * FFN_ROWS, FFN_ROWS)
            return pltpu.make_async_copy(zeros_ref, xs_ref.at[pl.ds(start, FFN_ROWS)], fill_sem)

        def fill(i, c):
            @pl.when(fill_ref[i] == 1)
            def _():
                fill_copy(i).start()
            return c

        def fill_wait(i, c):
            @pl.when(fill_ref[i] == 1)
            def _():
                fill_copy(i).wait()
            return c

        lax.fori_loop(0, n_tiles, fill, 0)
        lax.fori_loop(0, n_tiles, fill_wait, 0)

    def copies(t):
        return [_row_copy(vpk_ref, step * rows + t, xs_ref, off_ref[idx_ref[k, t]] + rank_ref[k, t], sem)
                for k in range(TOP_K)]

    def issue(t, c):
        for cp in copies(t):
            cp.start()
        return c

    def drain(t, c):
        for cp in copies(t):
            cp.wait()
        return c

    lax.fori_loop(0, rows, issue, 0)
    lax.fori_loop(0, rows, drain, 0)


def _dispatch(off, fill, idx_t, rank_t, vpk, zeros, n_rows):
    t = vpk.shape[0]
    rows = DISP_ROWS
    assert t % rows == 0
    smem_blk = pl.BlockSpec((TOP_K, rows), lambda i, *_: (0, i), memory_space=pltpu.SMEM)
    return pl.pallas_call(
        _dispatch_kernel,
        grid_spec=pltpu.PrefetchScalarGridSpec(
            num_scalar_prefetch=2,
            grid=(t // rows,),
            in_specs=[smem_blk, smem_blk,
                      pl.BlockSpec(memory_space=pl.ANY), pl.BlockSpec(memory_space=pl.ANY)],
            out_specs=pl.BlockSpec(memory_space=pl.ANY),
            scratch_shapes=[pltpu.SemaphoreType.DMA, pltpu.SemaphoreType.DMA]),
        out_shape=jax.ShapeDtypeStruct((n_rows, HALF), u32),
        compiler_params=pltpu.CompilerParams(dimension_semantics=("arbitrary",)),
        name="dispatch",
    )(off, fill, idx_t, rank_t, vpk, zeros)


def _ffn_kernel(te_ref, tf_ref, nv_ref, xs_ref, wg_ref, bg_ref, wu_ref, bu_ref, wd_ref, bd_ref,
                ys_ref, wg_s, wu_s, wd_s):
    step = pl.program_id(0)

    @pl.when(step < nv_ref[0])
    def _():
        @pl.when(tf_ref[step] == 1)
        def _():
            wg_s[...] = wg_ref[0].astype(bf16)
            wu_s[...] = wu_ref[0].astype(bf16)
            wd_s[...] = wd_ref[0].astype(bf16)

        x = _unpack_rows(xs_ref[...]).astype(bf16)
        g = jnp.minimum(_dot(x, wg_s[...]) + bg_ref[0], SWIGLU_LIMIT)
        u = jnp.clip(_dot(x, wu_s[...]) + bu_ref[0], -SWIGLU_LIMIT, SWIGLU_LIMIT)
        hid = (u + 1.0) * (g * jax.nn.sigmoid(SWIGLU_ALPHA * g))
        y = _dot(hid.astype(bf16), wd_s[...]) + bd_ref[0]
        ys_ref[...] = _pack_rows(y)

    @pl.when(step >= nv_ref[0])
    def _():
        ys_ref[...] = jnp.zeros_like(ys_ref)


def _ffn(tile_e, tile_first, n_valid, xs, w_gate, b_gate, w_up, b_up, w_down, b_down):
    n_rows = xs.shape[0]
    n_tiles = n_rows // FFN_ROWS

    def row_map(i, te, tf, nv):
        return (jnp.minimum(i, nv[0] - 1), 0)

    def w_map(i, te, tf, nv):
        return (te[i], 0, 0)

    w_spec = pl.BlockSpec((1, D_MODEL, D_FF), w_map)
    b_spec = pl.BlockSpec((1, 1, D_FF), w_map)
    return pl.pallas_call(
        _ffn_kernel,
        grid_spec=pltpu.PrefetchScalarGridSpec(
            num_scalar_prefetch=3,
            grid=(n_tiles,),
            in_specs=[pl.BlockSpec((FFN_ROWS, HALF), row_map),
                      w_spec, b_spec, w_spec, b_spec, w_spec, b_spec],
            out_specs=pl.BlockSpec((FFN_ROWS, HALF), lambda i, *_: (i, 0)),
            scratch_shapes=[pltpu.VMEM((D_MODEL, D_FF), bf16),
                            pltpu.VMEM((D_MODEL, D_FF), bf16),
                            pltpu.VMEM((D_FF, D_MODEL), bf16)]),
        out_shape=jax.ShapeDtypeStruct((n_rows, HALF), u32),
        compiler_params=pltpu.CompilerParams(dimension_semantics=("arbitrary",),
                                             vmem_limit_bytes=VMEM_LIMIT),
        name="experts",
    )(tile_e, tile_first, n_valid, xs, w_gate, b_gate.reshape(N_EXPERTS, 1, D_FF),
      w_up, b_up.reshape(N_EXPERTS, 1, D_FF), w_down, b_down.reshape(N_EXPERTS, 1, D_MODEL))


def _finish_kernel(off_ref, idx_ref, rank_ref, ys_ref, h1_ref, pw_ref, pin_ref, g_post_ref,
                   w_ple_ref, w_pg_ref, b_pg_ref, g_ple_ref, out_ref, gbuf, sem):
    rows = h1_ref.shape[0]

    def copies(t):
        return [_row_copy(ys_ref, off_ref[idx_ref[k, t]] + rank_ref[k, t], gbuf.at[k], t, sem)
                for k in range(TOP_K)]

    def issue(t, c):
        for cp in copies(t):
            cp.start()
        return c

    def drain(t, c):
        for cp in copies(t):
            cp.wait()
        return c

    lax.fori_loop(0, rows, issue, 0)
    lax.fori_loop(0, rows, drain, 0)

    pw = pw_ref[...]
    f = jnp.zeros((rows, D_MODEL), f32)
    for k in range(TOP_K):
        f = f + pw[:, k:k + 1] * _unpack_rows(gbuf[k])
    h2 = h1_ref[...] + _rms(f, g_post_ref[...])
    gate = jax.nn.sigmoid(_dot(h2.astype(bf16), w_pg_ref[...]) + b_pg_ref[...])
    pe = _dot(pin_ref[...], w_ple_ref[...]) * gate
    out_ref[...] = h2 + _rms(pe, g_ple_ref[...])


def _finish(off, idx_t, rank_t, ys, h1, pw, pin, g_post, w_ple, w_pg, b_pg, g_ple):
    t = h1.shape[0]
    rows = FIN_ROWS
    assert t % rows == 0
    smem_blk = pl.BlockSpec((TOP_K, rows), lambda i, *_: (0, i), memory_space=pltpu.SMEM)

    def full(shape):
        return pl.BlockSpec(shape, lambda i, *_: (0,) * len(shape))

    return pl.pallas_call(
        _finish_kernel,
        grid_spec=pltpu.PrefetchScalarGridSpec(
            num_scalar_prefetch=1,
            grid=(t // rows,),
            in_specs=[smem_blk, smem_blk, pl.BlockSpec(memory_space=pl.ANY),
                      pl.BlockSpec((rows, D_MODEL), lambda i, *_: (i, 0)),
                      pl.BlockSpec((rows, LANES), lambda i, *_: (i, 0)),
                      pl.BlockSpec((rows, PLE_DIM), lambda i, *_: (i, 0)),
                      full(g_post.shape), full(w_ple.shape), full(w_pg.shape),
                      full(b_pg.shape), full(g_ple.shape)],
            out_specs=pl.BlockSpec((rows, D_MODEL), lambda i, *_: (i, 0)),
            scratch_shapes=[pltpu.VMEM((TOP_K, rows, HALF), u32), pltpu.SemaphoreType.DMA]),
        out_shape=jax.ShapeDtypeStruct((t, D_MODEL), f32),
        compiler_params=pltpu.CompilerParams(dimension_semantics=("arbitrary",),
                                             vmem_limit_bytes=VMEM_LIMIT),
        name="finish",
    )(off, idx_t, rank_t, ys, h1, pw, pin, g_post, w_ple, w_pg, b_pg, g_ple)


def _block_diag_pairs(w):
    per = MXU_DIM // HEAD_A
    out = jnp.zeros((N_HEADS_A // per, MXU_DIM, MXU_DIM), w.dtype)
    for h in range(N_HEADS_A):
        j, q = divmod(h, per)
        out = out.at[j, q * HEAD_A:(q + 1) * HEAD_A, q * HEAD_A:(q + 1) * HEAD_A].set(w[h])
    return out


def _time_major(x):
    x = jnp.swapaxes(x, 0, 1)
    return x.reshape((x.shape[0] * x.shape[1],) + x.shape[2:])


def _batch_major(x, b):
    return jnp.swapaxes(x.reshape(x.shape[0] // b, b, x.shape[1]), 0, 1)


def _routing_tables(counts):
    padded = ((counts + FFN_ROWS - 1) // FFN_ROWS) * FFN_ROWS
    ends = jnp.cumsum(padded)
    off = ends - padded
    return off.astype(i32), (ends // FFN_ROWS).astype(i32)


def _layer(xp_tm, xs_tm, pin, state_h, state_conv, state_pool, lw, b_p):
    (norm_mix_pre, w_in, conv_w, conv_b, w_rgate, b_rgate, w_igate, b_igate, lru_lambda,
     pool_w, pool_b, pool_scale, norm_group_a, norm_group_b, w_out, norm_mix_post,
     norm_ffn_pre, w_router, b_router, w_gate, b_gate, w_up, b_up, w_down, b_down, norm_ffn_post,
     w_ple, w_ple_gate, b_ple_gate, norm_ple) = lw
    n_p, n_s = xp_tm.shape[0], xs_tm.shape[0]
    b_s = state_h.shape[0]
    row = lambda a: a.reshape(1, -1)

    wr_hi, wr_lo = _split(_block_diag_pairs(w_rgate))
    wi_hi, wi_lo = _split(_block_diag_pairs(w_igate))
    mix_w = (row(norm_mix_pre), w_in.astype(bf16), conv_w, row(conv_b),
             wr_hi, wr_lo, row(b_rgate), wi_hi, wi_lo, row(b_igate), row(lru_lambda),
             pool_w.astype(bf16), row(pool_b), row(pool_scale), row(norm_group_a), row(norm_group_b),
             w_out.astype(bf16), row(norm_mix_post))

    zeros = lambda *s: jnp.zeros(s, f32)
    h1_p, hfin_p, cfin_p, pfin_p = _mixer(
        xp_tm, zeros((CONV_W - 1) * b_p, W_A), zeros(POOL_BUF * b_p, W_B), zeros(b_p, W_A),
        mix_w, bt=b_p, start=0)
    h1_s, hfin_s, cfin_s, pfin_s = _mixer(
        xs_tm, _time_major(state_conv), _time_major(state_pool), state_h,
        mix_w, bt=b_s, start=PAST_LEN)
    h1 = jnp.concatenate([h1_p, h1_s], axis=0)
    t = n_p + n_s

    w_router_pad = jnp.zeros((D_MODEL, LANES), f32).at[:, :N_EXPERTS].set(w_router)
    b_router_pad = jnp.zeros((1, LANES), f32).at[0, :N_EXPERTS].set(b_router)
    rt_hi, rt_lo = _split(w_router_pad)
    tri = jnp.triu(jnp.ones((ROUTE_ROWS, ROUTE_ROWS), bf16), k=1)
    vpk, idx_t, rank_t, pw, cnt = _router(h1, row(norm_ffn_pre), rt_hi, rt_lo, b_router_pad, tri)

    counts = cnt[:, 0]
    off, tile_ends = _routing_tables(counts)
    n_tiles = (t * TOP_K) // FFN_ROWS + N_EXPERTS
    tile_ids = jnp.arange(n_tiles, dtype=i32)
    n_valid = tile_ends[-1:]
    tile_e = jnp.searchsorted(tile_ends, jnp.minimum(tile_ids, n_valid - 1), side="right").astype(i32)
    tile_first = jnp.concatenate([jnp.ones((1,), i32), (tile_e[1:] != tile_e[:-1]).astype(i32)])
    partial_last = (tile_ids == tile_ends[tile_e] - 1) & (counts[tile_e] % FFN_ROWS != 0)
    fill = (partial_last | (tile_ids >= n_valid)).astype(i32)

    xs = _dispatch(off, fill, idx_t, rank_t, vpk, jnp.zeros((FFN_ROWS, HALF), u32), n_tiles * FFN_ROWS)
    ys = _ffn(tile_e, tile_first, n_valid, xs, w_gate, b_gate, w_up, b_up, w_down, b_down)
    out = _finish(off, idx_t, rank_t, ys, h1, pw, pin.astype(bf16), row(norm_ffn_post),
                  w_ple.astype(bf16), w_ple_gate.astype(bf16), row(b_ple_gate), row(norm_ple))

    states = (hfin_p, _batch_major(cfin_p, b_p), _batch_major(pfin_p, b_p),
              hfin_s, _batch_major(cfin_s, b_s), _batch_major(pfin_s, b_s))
    return out[:n_p], out[n_p:], states


def kernel(x_prompt, x_sample, state_rglru_h, state_rglru_conv, state_pool, p_prompt, p_sample, norm_mix_pre, w_in, conv_w, conv_b, w_rgate, b_rgate, w_igate, b_igate, lru_lambda, pool_w, pool_b, pool_scale, norm_group_a, norm_group_b, w_out, norm_mix_post, norm_ffn_pre, w_router, b_router, w_gate, b_gate, w_up, b_up, w_down, b_down, norm_ffn_post, w_ple, w_ple_gate, b_ple_gate, norm_ple):
    depth = w_in.shape[0]
    b_p, b_s = x_prompt.shape[0], x_sample.shape[0]
    per_layer = (norm_mix_pre, w_in, conv_w, conv_b, w_rgate, b_rgate, w_igate, b_igate, lru_lambda,
                 pool_w, pool_b, pool_scale, norm_group_a, norm_group_b, w_out, norm_mix_post,
                 norm_ffn_pre, w_router, b_router, w_gate, b_gate, w_up, b_up, w_down, b_down,
                 norm_ffn_post, w_ple, w_ple_gate, b_ple_gate, norm_ple)
    hp, hs = _time_major(x_prompt), _time_major(x_sample)
    collected = []
    for i in range(depth):
        pin = jnp.concatenate([_time_major(p_prompt[i]), _time_major(p_sample[i])], axis=0)
        hp, hs, states = _layer(hp, hs, pin, state_rglru_h[i], state_rglru_conv[i], state_pool[i],
                                tuple(w[i] for w in per_layer), b_p)
        collected.append(states)
    stacked = tuple(jnp.stack([c[j] for c in collected]) for j in range(6))
    return (_batch_major(hp, b_p), _batch_major(hs, b_s)) + stacked
```

```python
import dataclasses
import functools

import jax
import jax.numpy as jnp
from jax import lax
from jax.experimental import pallas as pl
from jax.experimental.pallas import tpu as pltpu
from jax.experimental.pallas import tpu_sc as plsc

D_MODEL = 1024
W_A = 512
W_B = 512
N_HEADS_A = 8
HEAD_A = W_A // N_HEADS_A
CONV_W = 4
LRU_C = 8.0
POOL_WINDOWS = (2, 4, 8, 16)
GROUP_B = W_B // len(POOL_WINDOWS)
POOL_BUF = max(POOL_WINDOWS) - 1
N_EXPERTS = 32
TOP_K = 4
D_FF = 1024
SWIGLU_LIMIT = 7.0
SWIGLU_ALPHA = 1.702
PLE_DIM = 256
EPS = 1e-6
PAST_LEN = 16384

LANES = 128
MXU_DIM = 256
HALF = D_MODEL // 2
SC_CORES = 2
SC_SUBCORES = 16
SC_WORKERS = SC_CORES * SC_SUBCORES
SC_LANES = 16
SC_CHUNK = 64

MIX_ROWS = 512
ROUTE_ROWS = 512
FFN_ROWS = 512
FIN_ROWS = 256
VMEM_LIMIT = 56 * 1024 * 1024

f32 = jnp.float32
bf16 = jnp.bfloat16
u32 = jnp.uint32
i32 = jnp.int32


def _rms(x, g):
    return x * lax.rsqrt(jnp.mean(x * x, axis=-1, keepdims=True) + EPS) * g


def _split(x):
    hi = x.astype(bf16)
    lo = (x - hi.astype(f32)).astype(bf16)
    return hi, lo


def _dot(a, b):
    return jnp.dot(a, b, preferred_element_type=f32)


def _dot3(x, w_hi, w_lo):
    x_hi, x_lo = _split(x)
    return _dot(x_hi, w_hi) + (_dot(x_hi, w_lo) + _dot(x_lo, w_hi))


def _pack_rows(x):
    bits = lax.bitcast_convert_type(x.astype(bf16).astype(f32), u32)
    return (bits[:, HALF:] & jnp.uint32(0xFFFF0000)) | (bits[:, :HALF] >> 16)


def _unpack_rows(p):
    lo = lax.bitcast_convert_type(p << 16, f32)
    hi = lax.bitcast_convert_type(p & jnp.uint32(0xFFFF0000), f32)
    return jnp.concatenate([lo, hi], axis=-1)


def _mixer_kernel(x_ref, conv0_ref, pool0_ref, h0_ref, g_pre_ref, w_in_ref, conv_w_ref, conv_b_ref,
                  wr_hi_ref, wr_lo_ref, br_ref, wi_hi_ref, wi_lo_ref, bi_ref, lam_ref,
                  pool_w_ref, pool_b_ref, pool_scale_ref, ng_a_ref, ng_b_ref, w_out_ref, g_post_ref,
                  h1_ref, hfin_ref, convfin_ref, poolfin_ref,
                  h_s, conv_s, pool_s, a_s, b_s, hs_s, *, bt, tt, start):
    step = pl.program_id(0)
    rows = bt * tt

    @pl.when(step == 0)
    def _():
        h_s[...] = h0_ref[...]
        conv_s[...] = conv0_ref[...]
        pool_s[...] = pool0_ref[...]

    x = x_ref[...]
    u = _rms(x, g_pre_ref[...]).astype(bf16)
    z = _dot(u, w_in_ref[...])
    xa, ga, xb = z[:, :W_A], z[:, W_A:2 * W_A], z[:, 2 * W_A:]

    ext_a = jnp.concatenate([conv_s[...], xa], axis=0)
    xc = conv_b_ref[...]
    for k in range(CONV_W):
        xc = xc + ext_a[k * bt:k * bt + rows] * conv_w_ref[k:k + 1, :]
    conv_s[...] = ext_a[rows:]

    def gate(w_hi_ref, w_lo_ref, b_ref):
        parts = [_dot3(xc[:, j * MXU_DIM:(j + 1) * MXU_DIM], w_hi_ref[j], w_lo_ref[j])
                 for j in range(W_A // MXU_DIM)]
        return jax.nn.sigmoid(jnp.concatenate(parts, axis=-1) + b_ref[...])

    r = gate(wr_hi_ref, wr_lo_ref, br_ref)
    ig = gate(wi_hi_ref, wi_lo_ref, bi_ref)
    lam = lam_ref[...]
    softplus_neg = jnp.maximum(-lam, 0.0) + jnp.log1p(jnp.exp(-jnp.abs(lam)))
    log_a = (-LRU_C) * r * softplus_neg
    a_s[...] = jnp.exp(log_a)
    th = jnp.tanh(log_a)
    b_s[...] = jnp.sqrt(-2.0 * th / (1.0 - th)) * (ig * xc)

    def scan_step(t, h):
        sl = pl.ds(pl.multiple_of(t * bt, bt), bt)
        h = a_s[sl, :] * h + b_s[sl, :]
        hs_s[sl, :] = h
        return h

    h_last = lax.fori_loop(0, tt, scan_step, h_s[...], unroll=True)
    h_s[...] = h_last
    ya = hs_s[...] * jax.nn.gelu(ga)

    ext_b = jnp.concatenate([pool_s[...], xb], axis=0)
    pool_s[...] = ext_b[rows:]
    s2 = ext_b[bt:, :] + ext_b[:-bt, :]
    s4 = s2[2 * bt:, GROUP_B:] + s2[:-2 * bt, GROUP_B:]
    s8 = s4[4 * bt:, GROUP_B:] + s4[:-4 * bt, GROUP_B:]
    s16 = s8[8 * bt:, GROUP_B:] + s8[:-8 * bt, GROUP_B:]
    wins = (s2[14 * bt:, :GROUP_B], s4[12 * bt:, :GROUP_B], s8[8 * bt:, :GROUP_B], s16)
    t_idx = lax.broadcasted_iota(i32, (rows, GROUP_B), 0) // bt
    pos1 = (t_idx + (step * tt + start + 1)).astype(f32)
    yb_parts = []
    for g, w in enumerate(POOL_WINDOWS):
        cnt = jnp.minimum(jnp.float32(w), pos1)
        d = wins[g] / cnt - xb[:, g * GROUP_B:(g + 1) * GROUP_B]
        yb_parts.append(_dot(d.astype(bf16), pool_w_ref[g]))
    yb = (jnp.concatenate(yb_parts, axis=-1) + pool_b_ref[...]) * pool_scale_ref[...]

    na = _rms(ya, ng_a_ref[...])
    nb = _rms(yb, ng_b_ref[...])
    m = _dot(jnp.concatenate([na, nb], axis=-1).astype(bf16), w_out_ref[...])
    h1_ref[...] = x + _rms(m, g_post_ref[...])

    hfin_ref[...] = h_last
    convfin_ref[...] = conv_s[...]
    poolfin_ref[...] = pool_s[...]


def _full(shape):
    return pl.BlockSpec(shape, lambda i, *_: (0,) * len(shape))


def _mixer(x_tm, conv0, pool0, h0, wts, *, bt, start):
    n = x_tm.shape[0]
    tt = MIX_ROWS // bt
    rows = bt * tt
    assert n % rows == 0
    kern = functools.partial(_mixer_kernel, bt=bt, tt=tt, start=start)
    small = [conv0, pool0, h0] + list(wts)
    return pl.pallas_call(
        kern,
        grid=(n // rows,),
        in_specs=[pl.BlockSpec((rows, D_MODEL), lambda i: (i, 0))] + [_full(a.shape) for a in small],
        out_specs=[pl.BlockSpec((rows, D_MODEL), lambda i: (i, 0)),
                   _full((bt, W_A)), _full(((CONV_W - 1) * bt, W_A)), _full((POOL_BUF * bt, W_B))],
        out_shape=[jax.ShapeDtypeStruct((n, D_MODEL), f32),
                   jax.ShapeDtypeStruct((bt, W_A), f32),
                   jax.ShapeDtypeStruct(((CONV_W - 1) * bt, W_A), f32),
                   jax.ShapeDtypeStruct((POOL_BUF * bt, W_B), f32)],
        scratch_shapes=[pltpu.VMEM((bt, W_A), f32),
                        pltpu.VMEM(((CONV_W - 1) * bt, W_A), f32),
                        pltpu.VMEM((POOL_BUF * bt, W_B), f32),
                        pltpu.VMEM((rows, W_A), f32),
                        pltpu.VMEM((rows, W_A), f32),
                        pltpu.VMEM((rows, W_A), f32)],
        compiler_params=pltpu.CompilerParams(dimension_semantics=("arbitrary",),
                                             vmem_limit_bytes=VMEM_LIMIT),
        name="mixer",
    )(x_tm, *small)


def _router_kernel(h1_ref, g_ref, wr_hi_ref, wr_lo_ref, br_ref, tri_ref,
                   vpk_ref, idx_ref, rank_ref, pw_ref, cnt_ref, carry_s):
    step = pl.program_id(0)
    rows = h1_ref.shape[0]

    @pl.when(step == 0)
    def _():
        carry_s[...] = jnp.zeros_like(carry_s)

    v = _rms(h1_ref[...], g_ref[...])
    vpk_ref[...] = _pack_rows(v)
    logits = _dot3(v, wr_hi_ref[...], wr_lo_ref[...]) + br_ref[...]
    lt = jnp.transpose(logits)[:N_EXPERTS, :]

    eio = lax.broadcasted_iota(i32, (N_EXPERTS, rows), 0).astype(f32)
    work = lt
    vals, idxs, sels = [], [], []
    for _ in range(TOP_K):
        m = jnp.max(work, axis=0, keepdims=True)
        ik = jnp.min(jnp.where(work == m, eio, float(N_EXPERTS)), axis=0, keepdims=True)
        sel = eio == ik
        vals.append(m)
        idxs.append(ik)
        sels.append(sel)
        work = jnp.where(sel, -jnp.inf, work)
    exps = [jnp.exp(val - vals[0]) for val in vals]
    denom = exps[0] + exps[1] + exps[2] + exps[3]
    probs = [e / denom for e in exps]

    chosen = sels[0] | sels[1] | sels[2] | sels[3]
    onehot = jnp.where(chosen, 1.0, 0.0)
    before = _dot(onehot.astype(bf16), tri_ref[...])
    base = carry_s[...] + before
    ranks = [jnp.sum(jnp.where(sel, base, 0.0), axis=0, keepdims=True) for sel in sels]
    carry_s[...] = carry_s[...] + jnp.sum(onehot, axis=1, keepdims=True)

    idx_ref[...] = jnp.concatenate(idxs, axis=0).astype(i32)
    rank_ref[...] = jnp.concatenate(ranks, axis=0).astype(i32)
    pad = jnp.zeros((LANES - TOP_K, rows), f32)
    pw_ref[...] = jnp.transpose(jnp.concatenate(probs + [pad], axis=0))
    cnt_ref[...] = carry_s[:, :LANES].astype(i32)


def _router(h1, g, wr_hi, wr_lo, br, tri):
    t = h1.shape[0]
    rows = ROUTE_ROWS
    assert t % rows == 0
    return pl.pallas_call(
        _router_kernel,
        grid=(t // rows,),
        in_specs=[pl.BlockSpec((rows, D_MODEL), lambda i: (i, 0)),
                  _full(g.shape), _full(wr_hi.shape), _full(wr_lo.shape), _full(br.shape), _full(tri.shape)],
        out_specs=[pl.BlockSpec((rows, HALF), lambda i: (i, 0)),
                   pl.BlockSpec((TOP_K, rows), lambda i: (0, i)),
                   pl.BlockSpec((TOP_K, rows), lambda i: (0, i)),
                   pl.BlockSpec((rows, LANES), lambda i: (i, 0)),
                   _full((N_EXPERTS, LANES))],
        out_shape=[jax.ShapeDtypeStruct((t, HALF), u32),
                   jax.ShapeDtypeStruct((TOP_K, t), i32),
                   jax.ShapeDtypeStruct((TOP_K, t), i32),
                   jax.ShapeDtypeStruct((t, LANES), f32),
                   jax.ShapeDtypeStruct((N_EXPERTS, LANES), i32)],
        scratch_shapes=[pltpu.VMEM((N_EXPERTS, rows), f32)],
        compiler_params=pltpu.CompilerParams(dimension_semantics=("arbitrary",),
                                             vmem_limit_bytes=VMEM_LIMIT),
        name="router",
    )(h1, g, wr_hi, wr_lo, br, tri)


def _sc_mesh():
    return plsc.VectorSubcoreMesh(core_axis_name="core", subcore_axis_name="subcore")


def _sc_params():
    return dataclasses.replace(pltpu.CompilerParams(), needs_layout_passes=False)


def _sc_worker():
    return lax.axis_index("subcore") * SC_CORES + lax.axis_index("core")


def _sc_positions(idx_v, rank_v, off_v, n):
    @pl.loop(0, n // SC_LANES)
    def _(i):
        sl = pl.ds(pl.multiple_of(i * SC_LANES, SC_LANES), SC_LANES)
        idx_v[sl] = plsc.load_gather(off_v, [idx_v[sl]]) + rank_v[sl]


def _sc_copy_rows(src_hbm, idx_v, dst_hbm, dst_row0, n, rows_v, sem):
    @pl.loop(0, n // SC_CHUNK)
    def _(c):
        o = pl.multiple_of(c * SC_CHUNK, SC_CHUNK)
        pltpu.async_copy(src_hbm.at[idx_v.at[pl.ds(o, SC_CHUNK)]], rows_v, sem).wait()
        pltpu.sync_copy(rows_v, dst_hbm.at[pl.ds(dst_row0 + o, SC_CHUNK)])


def _sc_dispatch(vpk, idx_flat, rank_flat, off, n_out):
    n_tok = vpk.shape[0]
    per_w = n_out // SC_WORKERS
    piece = n_tok // (SC_WORKERS // TOP_K)
    assert n_out % (SC_WORKERS * SC_CHUNK) == 0 and n_tok % piece == 0 and piece % SC_LANES == 0

    @pl.kernel(out_type=jax.ShapeDtypeStruct((n_out, HALF), u32), mesh=_sc_mesh(),
               compiler_params=_sc_params(), name="dispatch",
               scratch_types=[pltpu.VMEM((per_w,), i32), pltpu.VMEM((piece,), i32), pltpu.VMEM((piece,), i32),
                              pltpu.VMEM((N_EXPERTS,), i32), pltpu.VMEM((SC_CHUNK, HALF), u32),
                              pltpu.SemaphoreType.DMA])
    def k(v_hbm, i_hbm, r_hbm, off_hbm, o_hbm, src_v, idx_v, rank_v, off_v, rows_v, sem):
        lo = _sc_worker() * per_w
        pltpu.sync_copy(off_hbm, off_v)

        @pl.loop(0, per_w // SC_LANES)
        def _(i):
            src_v[pl.ds(pl.multiple_of(i * SC_LANES, SC_LANES), SC_LANES)] = jnp.zeros((SC_LANES,), i32)

        @pl.loop(0, TOP_K * n_tok // piece)
        def _(pc):
            pltpu.sync_copy(i_hbm.at[pl.ds(pc * piece, piece)], idx_v)
            pltpu.sync_copy(r_hbm.at[pl.ds(pc * piece, piece)], rank_v)
            _sc_positions(idx_v, rank_v, off_v, piece)
            tok0 = lax.rem(pc * piece, n_tok)

            @pl.loop(0, piece // SC_LANES)
            def _(i):
                o = pl.multiple_of(i * SC_LANES, SC_LANES)
                p = idx_v[pl.ds(o, SC_LANES)] - lo
                mine = (p >= 0) & (p < per_w)
                tok = lax.iota(i32, SC_LANES) + (tok0 + o)
                plsc.store_scatter(src_v, [jnp.where(mine, p, 0)], tok, mask=mine)

        _sc_copy_rows(v_hbm, src_v, o_hbm, lo, per_w, rows_v, sem)

    return k(vpk, idx_flat, rank_flat, off)


def _sc_collect(ys, idx_flat, rank_flat, off):
    n = idx_flat.shape[0]
    per_w = n // SC_WORKERS
    assert n % (SC_WORKERS * SC_CHUNK) == 0

    @pl.kernel(out_type=jax.ShapeDtypeStruct((n, HALF), u32), mesh=_sc_mesh(),
               compiler_params=_sc_params(), name="collect",
               scratch_types=[pltpu.VMEM((per_w,), i32), pltpu.VMEM((per_w,), i32),
                              pltpu.VMEM((N_EXPERTS,), i32), pltpu.VMEM((SC_CHUNK, HALF), u32),
                              pltpu.SemaphoreType.DMA])
    def k(y_hbm, i_hbm, r_hbm, off_hbm, o_hbm, idx_v, rank_v, off_v, rows_v, sem):
        lo = _sc_worker() * per_w
        pltpu.sync_copy(off_hbm, off_v)
        pltpu.sync_copy(i_hbm.at[pl.ds(lo, per_w)], idx_v)
        pltpu.sync_copy(r_hbm.at[pl.ds(lo, per_w)], rank_v)
        _sc_positions(idx_v, rank_v, off_v, per_w)
        _sc_copy_rows(y_hbm, idx_v, o_hbm, lo, per_w, rows_v, sem)

    return k(ys, idx_flat, rank_flat, off)


def _ffn_kernel(te_ref, tf_ref, nv_ref, xs_ref, wg_ref, bg_ref, wu_ref, bu_ref, wd_ref, bd_ref,
                ys_ref, wg_s, wu_s, wd_s):
    step = pl.program_id(0)

    @pl.when(step < nv_ref[0])
    def _():
        @pl.when(tf_ref[step] == 1)
        def _():
            wg_s[...] = wg_ref[0].astype(bf16)
            wu_s[...] = wu_ref[0].astype(bf16)
            wd_s[...] = wd_ref[0].astype(bf16)

        x = _unpack_rows(xs_ref[...]).astype(bf16)
        g = jnp.minimum(_dot(x, wg_s[...]) + bg_ref[0], SWIGLU_LIMIT)
        u = jnp.clip(_dot(x, wu_s[...]) + bu_ref[0], -SWIGLU_LIMIT, SWIGLU_LIMIT)
        hid = (u + 1.0) * (g * jax.nn.sigmoid(SWIGLU_ALPHA * g))
        y = _dot(hid.astype(bf16), wd_s[...]) + bd_ref[0]
        ys_ref[...] = _pack_rows(y)

    @pl.when(step >= nv_ref[0])
    def _():
        ys_ref[...] = jnp.zeros_like(ys_ref)


def _ffn(tile_e, tile_first, n_valid, xs, w_gate, b_gate, w_up, b_up, w_down, b_down):
    n_rows = xs.shape[0]
    n_tiles = n_rows // FFN_ROWS

    def row_map(i, te, tf, nv):
        return (jnp.minimum(i, nv[0] - 1), 0)

    def w_map(i, te, tf, nv):
        return (te[i], 0, 0)

    w_spec = pl.BlockSpec((1, D_MODEL, D_FF), w_map)
    b_spec = pl.BlockSpec((1, 1, D_FF), w_map)
    return pl.pallas_call(
        _ffn_kernel,
        grid_spec=pltpu.PrefetchScalarGridSpec(
            num_scalar_prefetch=3,
            grid=(n_tiles,),
            in_specs=[pl.BlockSpec((FFN_ROWS, HALF), row_map),
                      w_spec, b_spec, w_spec, b_spec, w_spec, b_spec],
            out_specs=pl.BlockSpec((FFN_ROWS, HALF), lambda i, *_: (i, 0)),
            scratch_shapes=[pltpu.VMEM((D_MODEL, D_FF), bf16),
                            pltpu.VMEM((D_MODEL, D_FF), bf16),
                            pltpu.VMEM((D_FF, D_MODEL), bf16)]),
        out_shape=jax.ShapeDtypeStruct((n_rows, HALF), u32),
        compiler_params=pltpu.CompilerParams(dimension_semantics=("arbitrary",),
                                             vmem_limit_bytes=VMEM_LIMIT),
        name="experts",
    )(tile_e, tile_first, n_valid, xs, w_gate, b_gate.reshape(N_EXPERTS, 1, D_FF),
      w_up, b_up.reshape(N_EXPERTS, 1, D_FF), w_down, b_down.reshape(N_EXPERTS, 1, D_MODEL))


def _finish_kernel(g_ref, h1_ref, pw_ref, pin_ref, g_post_ref, w_ple_ref, w_pg_ref, b_pg_ref, g_ple_ref,
                   out_ref):
    rows = h1_ref.shape[0]
    pw = pw_ref[...]
    f = jnp.zeros((rows, D_MODEL), f32)
    for k in range(TOP_K):
        f = f + pw[:, k:k + 1] * _unpack_rows(g_ref[k])
    h2 = h1_ref[...] + _rms(f, g_post_ref[...])
    gate = jax.nn.sigmoid(_dot(h2.astype(bf16), w_pg_ref[...]) + b_pg_ref[...])
    pe = _dot(pin_ref[...], w_ple_ref[...]) * gate
    out_ref[...] = h2 + _rms(pe, g_ple_ref[...])


def _finish(g, h1, pw, pin, g_post, w_ple, w_pg, b_pg, g_ple):
    t = h1.shape[0]
    rows = FIN_ROWS
    assert t % rows == 0
    return pl.pallas_call(
        _finish_kernel,
        grid=(t // rows,),
        in_specs=[pl.BlockSpec((TOP_K, rows, HALF), lambda i: (0, i, 0)),
                  pl.BlockSpec((rows, D_MODEL), lambda i: (i, 0)),
                  pl.BlockSpec((rows, LANES), lambda i: (i, 0)),
                  pl.BlockSpec((rows, PLE_DIM), lambda i: (i, 0)),
                  _full(g_post.shape), _full(w_ple.shape), _full(w_pg.shape),
                  _full(b_pg.shape), _full(g_ple.shape)],
        out_specs=pl.BlockSpec((rows, D_MODEL), lambda i: (i, 0)),
        out_shape=jax.ShapeDtypeStruct((t, D_MODEL), f32),
        compiler_params=pltpu.CompilerParams(dimension_semantics=("arbitrary",),
                                             vmem_limit_bytes=VMEM_LIMIT),
        name="finish",
    )(g, h1, pw, pin, g_post, w_ple, w_pg, b_pg, g_ple)


def _block_diag_pairs(w):
    per = MXU_DIM // HEAD_A
    out = jnp.zeros((N_HEADS_A // per, MXU_DIM, MXU_DIM), w.dtype)
    for h in range(N_HEADS_A):
        j, q = divmod(h, per)
        out = out.at[j, q * HEAD_A:(q + 1) * HEAD_A, q * HEAD_A:(q + 1) * HEAD_A].set(w[h])
    return out


def _time_major(x):
    x = jnp.swapaxes(x, 0, 1)
    return x.reshape((x.shape[0] * x.shape[1],) + x.shape[2:])


def _batch_major(x, b):
    return jnp.swapaxes(x.reshape(x.shape[0] // b, b, x.shape[1]), 0, 1)


def _routing_tables(counts, n_tiles):
    tiles = (counts + FFN_ROWS - 1) // FFN_ROWS
    tile_ends = jnp.cumsum(tiles)
    off = (tile_ends - tiles) * FFN_ROWS
    n_valid = tile_ends[-1:]
    tile_ids = jnp.minimum(jnp.arange(n_tiles, dtype=i32), n_valid - 1)
    tile_e = jnp.sum((tile_ids[:, None] >= tile_ends[None, :]).astype(i32), axis=1)
    tile_first = jnp.concatenate([jnp.ones((1,), i32), (tile_e[1:] != tile_e[:-1]).astype(i32)])
    return off.astype(i32), tile_e.astype(i32), tile_first, n_valid.astype(i32)


def _layer(xp_tm, xs_tm, pin, state_h, state_conv, state_pool, lw, b_p):
    (norm_mix_pre, w_in, conv_w, conv_b, w_rgate, b_rgate, w_igate, b_igate, lru_lambda,
     pool_w, pool_b, pool_scale, norm_group_a, norm_group_b, w_out, norm_mix_post,
     norm_ffn_pre, w_router, b_router, w_gate, b_gate, w_up, b_up, w_down, b_down, norm_ffn_post,
     w_ple, w_ple_gate, b_ple_gate, norm_ple) = lw
    n_p, n_s = xp_tm.shape[0], xs_tm.shape[0]
    b_s = state_h.shape[0]
    row = lambda a: a.reshape(1, -1)

    wr_hi, wr_lo = _split(_block_diag_pairs(w_rgate))
    wi_hi, wi_lo = _split(_block_diag_pairs(w_igate))
    mix_w = (row(norm_mix_pre), w_in.astype(bf16), conv_w, row(conv_b),
             wr_hi, wr_lo, row(b_rgate), wi_hi, wi_lo, row(b_igate), row(lru_lambda),
             pool_w.astype(bf16), row(pool_b), row(pool_scale), row(norm_group_a), row(norm_group_b),
             w_out.astype(bf16), row(norm_mix_post))

    zeros = lambda *s: jnp.zeros(s, f32)
    h1_p, hfin_p, cfin_p, pfin_p = _mixer(
        xp_tm, zeros((CONV_W - 1) * b_p, W_A), zeros(POOL_BUF * b_p, W_B), zeros(b_p, W_A),
        mix_w, bt=b_p, start=0)
    h1_s, hfin_s, cfin_s, pfin_s = _mixer(
        xs_tm, _time_major(state_conv), _time_major(state_pool), state_h,
        mix_w, bt=b_s, start=PAST_LEN)
    h1 = jnp.concatenate([h1_p, h1_s], axis=0)
    t = n_p + n_s

    w_router_pad = jnp.zeros((D_MODEL, LANES), f32).at[:, :N_EXPERTS].set(w_router)
    b_router_pad = jnp.zeros((1, LANES), f32).at[0, :N_EXPERTS].set(b_router)
    rt_hi, rt_lo = _split(w_router_pad)
    tri = jnp.triu(jnp.ones((ROUTE_ROWS, ROUTE_ROWS), bf16), k=1)
    vpk, idx_t, rank_t, pw, cnt = _router(h1, row(norm_ffn_pre), rt_hi, rt_lo, b_router_pad, tri)

    n_tiles = (t * TOP_K) // FFN_ROWS + N_EXPERTS
    off, tile_e, tile_first, n_valid = _routing_tables(cnt[:, 0], n_tiles)
    idx_flat, rank_flat = idx_t.reshape(-1), rank_t.reshape(-1)

    xs = _sc_dispatch(vpk, idx_flat, rank_flat, off, n_tiles * FFN_ROWS)
    ys = _ffn(tile_e, tile_first, n_valid, xs, w_gate, b_gate, w_up, b_up, w_down, b_down)
    g = _sc_collect(ys, idx_flat, rank_flat, off).reshape(TOP_K, t, HALF)
    out = _finish(g, h1, pw, pin.astype(bf16), row(norm_ffn_post),
                  w_ple.astype(bf16), w_ple_gate.astype(bf16), row(b_ple_gate), row(norm_ple))

    states = (hfin_p, _batch_major(cfin_p, b_p), _batch_major(pfin_p, b_p),
              hfin_s, _batch_major(cfin_s, b_s), _batch_major(pfin_s, b_s))
    return out[:n_p], out[n_p:], states


def kernel(x_prompt, x_sample, state_rglru_h, state_rglru_conv, state_pool, p_prompt, p_sample, norm_mix_pre, w_in, conv_w, conv_b, w_rgate, b_rgate, w_igate, b_igate, lru_lambda, pool_w, pool_b, pool_scale, norm_group_a, norm_group_b, w_out, norm_mix_post, norm_ffn_pre, w_router, b_router, w_gate, b_gate, w_up, b_up, w_down, b_down, norm_ffn_post, w_ple, w_ple_gate, b_ple_gate, norm_ple):
    depth = w_in.shape[0]
    b_p, b_s = x_prompt.shape[0], x_sample.shape[0]
    per_layer = (norm_mix_pre, w_in, conv_w, conv_b, w_rgate, b_rgate, w_igate, b_igate, lru_lambda,
                 pool_w, pool_b, pool_scale, norm_group_a, norm_group_b, w_out, norm_mix_post,
                 norm_ffn_pre, w_router, b_router, w_gate, b_gate, w_up, b_up, w_down, b_down,
                 norm_ffn_post, w_ple, w_ple_gate, b_ple_gate, norm_ple)
    hp, hs = _time_major(x_prompt), _time_major(x_sample)
    collected = []
    for i in range(depth):
        pin = jnp.concatenate([_time_major(p_prompt[i]), _time_major(p_sample[i])], axis=0)
        hp, hs, states = _layer(hp, hs, pin, state_rglru_h[i], state_rglru_conv[i], state_pool[i],
                                tuple(w[i] for w in per_layer), b_p)
        collected.append(states)
    stacked = tuple(jnp.stack([c[j] for c in collected]) for j in range(6))
    return (_batch_major(hp, b_p), _batch_major(hs, b_s)) + stacked
```

```python
import dataclasses
import functools

import jax
import jax.numpy as jnp
from jax import lax
from jax.experimental import pallas as pl
from jax.experimental.pallas import tpu as pltpu
from jax.experimental.pallas import tpu_sc as plsc

D_MODEL = 1024
W_A = 512
W_B = 512
N_HEADS_A = 8
HEAD_A = W_A // N_HEADS_A
CONV_W = 4
LRU_C = 8.0
POOL_WINDOWS = (2, 4, 8, 16)
GROUP_B = W_B // len(POOL_WINDOWS)
POOL_BUF = max(POOL_WINDOWS) - 1
N_EXPERTS = 32
TOP_K = 4
D_FF = 1024
SWIGLU_LIMIT = 7.0
SWIGLU_ALPHA = 1.702
PLE_DIM = 256
EPS = 1e-6
PAST_LEN = 16384

LANES = 128
MXU_DIM = 256
HALF = D_MODEL // 2
SC_CORES = 2
SC_SUBCORES = 16
SC_WORKERS = SC_CORES * SC_SUBCORES
SC_LANES = 16
SC_CHUNK = 64

MIX_ROWS = 512
ROUTE_ROWS = 512
FFN_ROWS = 512
FIN_ROWS = 256
VMEM_LIMIT = 56 * 1024 * 1024

f32 = jnp.float32
bf16 = jnp.bfloat16
u32 = jnp.uint32
i32 = jnp.int32


def _rms(x, g):
    return x * lax.rsqrt(jnp.mean(x * x, axis=-1, keepdims=True) + EPS) * g


def _split(x):
    hi = x.astype(bf16)
    lo = (x - hi.astype(f32)).astype(bf16)
    return hi, lo


def _dot(a, b):
    return jnp.dot(a, b, preferred_element_type=f32)


def _dot3(x, w_hi, w_lo):
    x_hi, x_lo = _split(x)
    return _dot(x_hi, w_hi) + (_dot(x_hi, w_lo) + _dot(x_lo, w_hi))


def _pack_rows(x):
    bits = lax.bitcast_convert_type(x.astype(bf16).astype(f32), u32)
    return (bits[:, HALF:] & jnp.uint32(0xFFFF0000)) | (bits[:, :HALF] >> 16)


def _unpack_rows(p):
    lo = lax.bitcast_convert_type(p << 16, f32)
    hi = lax.bitcast_convert_type(p & jnp.uint32(0xFFFF0000), f32)
    return jnp.concatenate([lo, hi], axis=-1)


def _mixer_kernel(x_ref, conv0_ref, pool0_ref, h0_ref, g_pre_ref, w_in_ref, conv_w_ref, conv_b_ref,
                  wr_hi_ref, wr_lo_ref, br_ref, wi_hi_ref, wi_lo_ref, bi_ref, lam_ref,
                  pool_w_ref, pool_b_ref, pool_scale_ref, ng_a_ref, ng_b_ref, w_out_ref, g_post_ref,
                  h1_ref, hfin_ref, convfin_ref, poolfin_ref,
                  h_s, conv_s, pool_s, a_s, b_s, hs_s, *, bt, tt, start):
    step = pl.program_id(0)
    rows = bt * tt

    @pl.when(step == 0)
    def _():
        h_s[...] = h0_ref[...]
        conv_s[...] = conv0_ref[...]
        pool_s[...] = pool0_ref[...]

    if x_ref.ndim == 3:
        x = pltpu.einshape("btd->tbd", x_ref[...]).reshape(rows, D_MODEL)
    else:
        x = x_ref[...]
    u = _rms(x, g_pre_ref[...]).astype(bf16)
    z = _dot(u, w_in_ref[...])
    xa, ga, xb = z[:, :W_A], z[:, W_A:2 * W_A], z[:, 2 * W_A:]

    ext_a = jnp.concatenate([conv_s[...], xa], axis=0)
    xc = conv_b_ref[...]
    for k in range(CONV_W):
        xc = xc + ext_a[k * bt:k * bt + rows] * conv_w_ref[k:k + 1, :]
    conv_s[...] = ext_a[rows:]

    def gate(w_hi_ref, w_lo_ref, b_ref):
        parts = [_dot3(xc[:, j * MXU_DIM:(j + 1) * MXU_DIM], w_hi_ref[j], w_lo_ref[j])
                 for j in range(W_A // MXU_DIM)]
        return jax.nn.sigmoid(jnp.concatenate(parts, axis=-1) + b_ref[...])

    r = gate(wr_hi_ref, wr_lo_ref, br_ref)
    ig = gate(wi_hi_ref, wi_lo_ref, bi_ref)
    lam = lam_ref[...]
    softplus_neg = jnp.maximum(-lam, 0.0) + jnp.log1p(jnp.exp(-jnp.abs(lam)))
    log_a = (-LRU_C) * r * softplus_neg
    a_s[...] = jnp.exp(log_a)
    th = jnp.tanh(log_a)
    b_s[...] = jnp.sqrt(-2.0 * th / (1.0 - th)) * (ig * xc)

    def scan_step(t, h):
        sl = pl.ds(pl.multiple_of(t * bt, bt), bt)
        h = a_s[sl, :] * h + b_s[sl, :]
        hs_s[sl, :] = h
        return h

    h_last = lax.fori_loop(0, tt, scan_step, h_s[...], unroll=True)
    h_s[...] = h_last
    ya = hs_s[...] * jax.nn.gelu(ga)

    ext_b = jnp.concatenate([pool_s[...], xb], axis=0)
    pool_s[...] = ext_b[rows:]
    s2 = ext_b[bt:, :] + ext_b[:-bt, :]
    s4 = s2[2 * bt:, GROUP_B:] + s2[:-2 * bt, GROUP_B:]
    s8 = s4[4 * bt:, GROUP_B:] + s4[:-4 * bt, GROUP_B:]
    s16 = s8[8 * bt:, GROUP_B:] + s8[:-8 * bt, GROUP_B:]
    wins = (s2[14 * bt:, :GROUP_B], s4[12 * bt:, :GROUP_B], s8[8 * bt:, :GROUP_B], s16)
    t_idx = lax.broadcasted_iota(i32, (rows, GROUP_B), 0) // bt
    pos1 = (t_idx + (step * tt + start + 1)).astype(f32)
    yb_parts = []
    for g, w in enumerate(POOL_WINDOWS):
        cnt = jnp.minimum(jnp.float32(w), pos1)
        d = wins[g] / cnt - xb[:, g * GROUP_B:(g + 1) * GROUP_B]
        yb_parts.append(_dot(d.astype(bf16), pool_w_ref[g]))
    yb = (jnp.concatenate(yb_parts, axis=-1) + pool_b_ref[...]) * pool_scale_ref[...]

    na = _rms(ya, ng_a_ref[...])
    nb = _rms(yb, ng_b_ref[...])
    m = _dot(jnp.concatenate([na, nb], axis=-1).astype(bf16), w_out_ref[...])
    h1 = x + _rms(m, g_post_ref[...])
    if h1_ref.ndim == 3:
        h1_ref[...] = pltpu.einshape("tbd->btd", h1.reshape(tt, bt, D_MODEL))
    else:
        h1_ref[...] = h1

    hfin_ref[...] = h_last
    convfin_ref[...] = conv_s[...]
    poolfin_ref[...] = pool_s[...]


def _full(shape):
    return pl.BlockSpec(shape, lambda i, *_: (0,) * len(shape))


def _mixer(x, conv0, pool0, h0, wts, *, bt, start):
    tt = MIX_ROWS // bt
    rows = bt * tt
    if x.ndim == 3:
        assert x.shape[0] == bt and x.shape[1] % tt == 0
        n_steps = x.shape[1] // tt
        x_spec = pl.BlockSpec((bt, tt, D_MODEL), lambda i: (0, i, 0))
    else:
        assert x.shape[0] % rows == 0
        n_steps = x.shape[0] // rows
        x_spec = pl.BlockSpec((rows, D_MODEL), lambda i: (i, 0))
    kern = functools.partial(_mixer_kernel, bt=bt, tt=tt, start=start)
    small = [conv0, pool0, h0] + list(wts)
    return pl.pallas_call(
        kern,
        grid=(n_steps,),
        in_specs=[x_spec] + [_full(a.shape) for a in small],
        out_specs=[x_spec,
                   _full((bt, W_A)), _full(((CONV_W - 1) * bt, W_A)), _full((POOL_BUF * bt, W_B))],
        out_shape=[jax.ShapeDtypeStruct(x.shape, f32),
                   jax.ShapeDtypeStruct((bt, W_A), f32),
                   jax.ShapeDtypeStruct(((CONV_W - 1) * bt, W_A), f32),
                   jax.ShapeDtypeStruct((POOL_BUF * bt, W_B), f32)],
        scratch_shapes=[pltpu.VMEM((bt, W_A), f32),
                        pltpu.VMEM(((CONV_W - 1) * bt, W_A), f32),
                        pltpu.VMEM((POOL_BUF * bt, W_B), f32),
                        pltpu.VMEM((rows, W_A), f32),
                        pltpu.VMEM((rows, W_A), f32),
                        pltpu.VMEM((rows, W_A), f32)],
        compiler_params=pltpu.CompilerParams(dimension_semantics=("arbitrary",),
                                             vmem_limit_bytes=VMEM_LIMIT),
        name="mixer",
    )(x, *small)


def _two_groups(step, first_steps, a_ref, b_ref):
    return jnp.where(step < first_steps, a_ref[...], b_ref[...])


def _router_kernel(h1a_ref, h1b_ref, g_ref, wr_hi_ref, wr_lo_ref, br_ref, tri_ref,
                   vpk_ref, idx_ref, rank_ref, pw_ref, cnt_ref, carry_s, *, first_steps):
    step = pl.program_id(0)
    rows = h1a_ref.shape[0]

    @pl.when(step == 0)
    def _():
        carry_s[...] = jnp.zeros_like(carry_s)

    v = _rms(_two_groups(step, first_steps, h1a_ref, h1b_ref), g_ref[...])
    vpk_ref[...] = _pack_rows(v)
    logits = _dot3(v, wr_hi_ref[...], wr_lo_ref[...]) + br_ref[...]
    lt = jnp.transpose(logits)[:N_EXPERTS, :]

    eio = lax.broadcasted_iota(i32, (N_EXPERTS, rows), 0).astype(f32)
    work = lt
    vals, idxs, sels = [], [], []
    for _ in range(TOP_K):
        m = jnp.max(work, axis=0, keepdims=True)
        ik = jnp.min(jnp.where(work == m, eio, float(N_EXPERTS)), axis=0, keepdims=True)
        sel = eio == ik
        vals.append(m)
        idxs.append(ik)
        sels.append(sel)
        work = jnp.where(sel, -jnp.inf, work)
    exps = [jnp.exp(val - vals[0]) for val in vals]
    denom = exps[0] + exps[1] + exps[2] + exps[3]
    probs = [e / denom for e in exps]

    chosen = sels[0] | sels[1] | sels[2] | sels[3]
    onehot = jnp.where(chosen, 1.0, 0.0)
    before = _dot(onehot.astype(bf16), tri_ref[...])
    base = carry_s[...] + before
    ranks = [jnp.sum(jnp.where(sel, base, 0.0), axis=0, keepdims=True) for sel in sels]
    carry_s[...] = carry_s[...] + jnp.sum(onehot, axis=1, keepdims=True)

    idx_ref[...] = jnp.concatenate(idxs, axis=0).astype(i32)
    rank_ref[...] = jnp.concatenate(ranks, axis=0).astype(i32)
    pad = jnp.zeros((LANES - TOP_K, rows), f32)
    pw_ref[...] = jnp.transpose(jnp.concatenate(probs + [pad], axis=0))
    cnt_ref[...] = carry_s[:, :LANES].astype(i32)


def _group_specs(rows, width, na, nb):
    first = na // rows
    return (pl.BlockSpec((rows, width), lambda i: (jnp.minimum(i, first - 1), 0)),
            pl.BlockSpec((rows, width), lambda i: (jnp.maximum(i - first, 0), 0)))


def _router(h1a, h1b, g, wr_hi, wr_lo, br, tri):
    na, nb = h1a.shape[0], h1b.shape[0]
    t = na + nb
    rows = ROUTE_ROWS
    assert na % rows == 0 and nb % rows == 0
    return pl.pallas_call(
        functools.partial(_router_kernel, first_steps=na // rows),
        grid=(t // rows,),
        in_specs=[*_group_specs(rows, D_MODEL, na, nb),
                  _full(g.shape), _full(wr_hi.shape), _full(wr_lo.shape), _full(br.shape), _full(tri.shape)],
        out_specs=[pl.BlockSpec((rows, HALF), lambda i: (i, 0)),
                   pl.BlockSpec((TOP_K, rows), lambda i: (0, i)),
                   pl.BlockSpec((TOP_K, rows), lambda i: (0, i)),
                   pl.BlockSpec((rows, LANES), lambda i: (i, 0)),
                   _full((N_EXPERTS, LANES))],
        out_shape=[jax.ShapeDtypeStruct((t, HALF), u32),
                   jax.ShapeDtypeStruct((TOP_K, t), i32),
                   jax.ShapeDtypeStruct((TOP_K, t), i32),
                   jax.ShapeDtypeStruct((t, LANES), f32),
                   jax.ShapeDtypeStruct((N_EXPERTS, LANES), i32)],
        scratch_shapes=[pltpu.VMEM((N_EXPERTS, rows), f32)],
        compiler_params=pltpu.CompilerParams(dimension_semantics=("arbitrary",),
                                             vmem_limit_bytes=VMEM_LIMIT),
        name="router",
    )(h1a, h1b, g, wr_hi, wr_lo, br, tri)


def _sc_mesh():
    return plsc.VectorSubcoreMesh(core_axis_name="core", subcore_axis_name="subcore")


def _sc_params():
    return dataclasses.replace(pltpu.CompilerParams(), needs_layout_passes=False)


def _sc_worker():
    return lax.axis_index("subcore") * SC_CORES + lax.axis_index("core")


def _sc_positions(idx_v, rank_v, off_v, n):
    @pl.loop(0, n // SC_LANES)
    def _(i):
        sl = pl.ds(pl.multiple_of(i * SC_LANES, SC_LANES), SC_LANES)
        idx_v[sl] = plsc.load_gather(off_v, [idx_v[sl]]) + rank_v[sl]


def _sc_copy_rows(src_hbm, idx_v, dst_hbm, dst_row0, n, rows_v, gsem, wsem):
    n_ch = n // SC_CHUNK
    n_pair = n_ch // 2

    def gather(c, b):
        o = pl.multiple_of(c * SC_CHUNK, SC_CHUNK)
        return pltpu.make_async_copy(src_hbm.at[idx_v.at[pl.ds(o, SC_CHUNK)]], rows_v.at[b], gsem.at[b])

    def write(c, b):
        o = pl.multiple_of(c * SC_CHUNK, SC_CHUNK)
        return pltpu.make_async_copy(rows_v.at[b], dst_hbm.at[pl.ds(dst_row0 + o, SC_CHUNK)], wsem.at[b])

    gather(0, 0).start()

    @pl.loop(0, n_pair)
    def _(i):
        c0 = 2 * i
        gather(c0, 0).wait()

        @pl.when(i > 0)
        def _():
            write(c0 - 1, 1).wait()

        gather(c0 + 1, 1).start()
        write(c0, 0).start()
        gather(c0 + 1, 1).wait()
        write(c0, 0).wait()

        @pl.when(c0 + 2 < n_ch)
        def _():
            gather(c0 + 2, 0).start()

        write(c0 + 1, 1).start()

    if n_ch % 2:
        gather(n_ch - 1, 0).wait()
        write(n_ch - 1, 0).start()
        write(n_ch - 1, 0).wait()
    if n_pair:
        write(2 * n_pair - 1, 1).wait()


def _sc_dispatch(vpk, idx_flat, rank_flat, off, n_out):
    n_tok = vpk.shape[0]
    per_w = n_out // SC_WORKERS
    assert n_out % (SC_WORKERS * SC_CHUNK) == 0 and n_tok % SC_LANES == 0

    @pl.kernel(out_type=jax.ShapeDtypeStruct((n_out, HALF), u32), mesh=_sc_mesh(),
               compiler_params=_sc_params(), name="dispatch",
               scratch_types=[pltpu.VMEM((per_w,), i32), pltpu.VMEM((n_tok,), i32), pltpu.VMEM((n_tok,), i32),
                              pltpu.VMEM((N_EXPERTS,), i32), pltpu.VMEM((2, SC_CHUNK, HALF), u32),
                              pltpu.SemaphoreType.DMA((2,)), pltpu.SemaphoreType.DMA((2,))])
    def k(v_hbm, i_hbm, r_hbm, off_hbm, o_hbm, src_v, idx_v, rank_v, off_v, rows_v, gsem, wsem):
        lo = _sc_worker() * per_w
        pltpu.sync_copy(off_hbm, off_v)

        @pl.loop(0, per_w // SC_LANES)
        def _(i):
            o = pl.multiple_of(i * SC_LANES, SC_LANES)
            src_v[pl.ds(o, SC_LANES)] = lax.iota(i32, SC_LANES) + lax.rem(lo + o, n_tok)

        @pl.loop(0, TOP_K)
        def _(kk):
            pltpu.sync_copy(i_hbm.at[pl.ds(kk * n_tok, n_tok)], idx_v)
            pltpu.sync_copy(r_hbm.at[pl.ds(kk * n_tok, n_tok)], rank_v)

            @plsc.parallel_loop(0, n_tok, step=SC_LANES, unroll=8)
            def _(o):
                sl = pl.ds(pl.multiple_of(o, SC_LANES), SC_LANES)
                p = plsc.load_gather(off_v, [idx_v[sl]]) + rank_v[sl] - lo
                mine = (p >= 0) & (p < per_w)
                tok = lax.iota(i32, SC_LANES) + o
                plsc.store_scatter(src_v, [jnp.where(mine, p, 0)], tok, mask=mine)

        _sc_copy_rows(v_hbm, src_v, o_hbm, lo, per_w, rows_v, gsem, wsem)

    return k(vpk, idx_flat, rank_flat, off)


def _sc_collect(ys, idx_flat, rank_flat, off):
    n = idx_flat.shape[0]
    per_w = n // SC_WORKERS
    assert n % (SC_WORKERS * SC_CHUNK) == 0

    @pl.kernel(out_type=jax.ShapeDtypeStruct((n, HALF), u32), mesh=_sc_mesh(),
               compiler_params=_sc_params(), name="collect",
               scratch_types=[pltpu.VMEM((per_w,), i32), pltpu.VMEM((per_w,), i32),
                              pltpu.VMEM((N_EXPERTS,), i32), pltpu.VMEM((2, SC_CHUNK, HALF), u32),
                              pltpu.SemaphoreType.DMA((2,)), pltpu.SemaphoreType.DMA((2,))])
    def k(y_hbm, i_hbm, r_hbm, off_hbm, o_hbm, idx_v, rank_v, off_v, rows_v, gsem, wsem):
        lo = _sc_worker() * per_w
        pltpu.sync_copy(off_hbm, off_v)
        pltpu.sync_copy(i_hbm.at[pl.ds(lo, per_w)], idx_v)
        pltpu.sync_copy(r_hbm.at[pl.ds(lo, per_w)], rank_v)
        _sc_positions(idx_v, rank_v, off_v, per_w)
        _sc_copy_rows(y_hbm, idx_v, o_hbm, lo, per_w, rows_v, gsem, wsem)

    return k(ys, idx_flat, rank_flat, off)


def _ffn_kernel(te_ref, tf_ref, nv_ref, xs_ref, wg_ref, bg_ref, wu_ref, bu_ref, wd_ref, bd_ref,
                ys_ref, wg_s, wu_s, wd_s):
    step = pl.program_id(0)

    @pl.when(step < nv_ref[0])
    def _():
        @pl.when(tf_ref[step] == 1)
        def _():
            wg_s[...] = wg_ref[0].astype(bf16)
            wu_s[...] = wu_ref[0].astype(bf16)
            wd_s[...] = wd_ref[0].astype(bf16)

        x = _unpack_rows(xs_ref[...]).astype(bf16)
        g = jnp.minimum(_dot(x, wg_s[...]) + bg_ref[0], SWIGLU_LIMIT)
        u = jnp.clip(_dot(x, wu_s[...]) + bu_ref[0], -SWIGLU_LIMIT, SWIGLU_LIMIT)
        hid = (u + 1.0) * (g * jax.nn.sigmoid(SWIGLU_ALPHA * g))
        y = _dot(hid.astype(bf16), wd_s[...]) + bd_ref[0]
        ys_ref[...] = _pack_rows(y)

    @pl.when(step >= nv_ref[0])
    def _():
        ys_ref[...] = jnp.zeros_like(ys_ref)


def _ffn(tile_e, tile_first, n_valid, xs, w_gate, b_gate, w_up, b_up, w_down, b_down):
    n_rows = xs.shape[0]
    n_tiles = n_rows // FFN_ROWS

    def row_map(i, te, tf, nv):
        return (jnp.minimum(i, nv[0] - 1), 0)

    def w_map(i, te, tf, nv):
        return (te[i], 0, 0)

    w_spec = pl.BlockSpec((1, D_MODEL, D_FF), w_map)
    b_spec = pl.BlockSpec((1, 1, D_FF), w_map)
    return pl.pallas_call(
        _ffn_kernel,
        grid_spec=pltpu.PrefetchScalarGridSpec(
            num_scalar_prefetch=3,
            grid=(n_tiles,),
            in_specs=[pl.BlockSpec((FFN_ROWS, HALF), row_map),
                      w_spec, b_spec, w_spec, b_spec, w_spec, b_spec],
            out_specs=pl.BlockSpec((FFN_ROWS, HALF), lambda i, *_: (i, 0)),
            scratch_shapes=[pltpu.VMEM((D_MODEL, D_FF), bf16),
                            pltpu.VMEM((D_MODEL, D_FF), bf16),
                            pltpu.VMEM((D_FF, D_MODEL), bf16)]),
        out_shape=jax.ShapeDtypeStruct((n_rows, HALF), u32),
        compiler_params=pltpu.CompilerParams(dimension_semantics=("arbitrary",),
                                             vmem_limit_bytes=VMEM_LIMIT),
        name="experts",
    )(tile_e, tile_first, n_valid, xs, w_gate, b_gate.reshape(N_EXPERTS, 1, D_FF),
      w_up, b_up.reshape(N_EXPERTS, 1, D_FF), w_down, b_down.reshape(N_EXPERTS, 1, D_MODEL))


def _finish_kernel(g_ref, h1a_ref, h1b_ref, pw_ref, pa_ref, pb_ref, g_post_ref, w_ple_ref, w_pg_ref, b_pg_ref,
                   g_ple_ref, outa_ref, outb_ref, *, first_steps):
    step = pl.program_id(0)
    rows = h1a_ref.shape[0]
    pw = pw_ref[...]
    f = jnp.zeros((rows, D_MODEL), f32)
    for k in range(TOP_K):
        f = f + pw[:, k:k + 1] * _unpack_rows(g_ref[k])
    h2 = _two_groups(step, first_steps, h1a_ref, h1b_ref) + _rms(f, g_post_ref[...])
    gate = jax.nn.sigmoid(_dot(h2.astype(bf16), w_pg_ref[...]) + b_pg_ref[...])
    pin = _two_groups(step, first_steps, pa_ref, pb_ref).astype(bf16)
    pe = _dot(pin, w_ple_ref[...]) * gate
    out = h2 + _rms(pe, g_ple_ref[...])

    @pl.when(step < first_steps)
    def _():
        outa_ref[...] = out

    @pl.when(step >= first_steps)
    def _():
        outb_ref[...] = out


def _finish(g, h1a, h1b, pw, pa, pb, g_post, w_ple, w_pg, b_pg, g_ple):
    na, nb = h1a.shape[0], h1b.shape[0]
    t = na + nb
    rows = FIN_ROWS
    assert na % rows == 0 and nb % rows == 0
    return pl.pallas_call(
        functools.partial(_finish_kernel, first_steps=na // rows),
        grid=(t // rows,),
        in_specs=[pl.BlockSpec((TOP_K, rows, HALF), lambda i: (0, i, 0)),
                  *_group_specs(rows, D_MODEL, na, nb),
                  pl.BlockSpec((rows, LANES), lambda i: (i, 0)),
                  *_group_specs(rows, PLE_DIM, na, nb),
                  _full(g_post.shape), _full(w_ple.shape), _full(w_pg.shape),
                  _full(b_pg.shape), _full(g_ple.shape)],
        out_specs=list(_group_specs(rows, D_MODEL, na, nb)),
        out_shape=[jax.ShapeDtypeStruct((na, D_MODEL), f32), jax.ShapeDtypeStruct((nb, D_MODEL), f32)],
        compiler_params=pltpu.CompilerParams(dimension_semantics=("arbitrary",),
                                             vmem_limit_bytes=VMEM_LIMIT),
        name="finish",
    )(g, h1a, h1b, pw, pa, pb, g_post, w_ple, w_pg, b_pg, g_ple)


def _block_diag_pairs(w):
    per = MXU_DIM // HEAD_A
    out = jnp.zeros((N_HEADS_A // per, MXU_DIM, MXU_DIM), w.dtype)
    for h in range(N_HEADS_A):
        j, q = divmod(h, per)
        out = out.at[j, q * HEAD_A:(q + 1) * HEAD_A, q * HEAD_A:(q + 1) * HEAD_A].set(w[h])
    return out


def _time_major(x):
    x = jnp.swapaxes(x, 0, 1)
    return x.reshape((x.shape[0] * x.shape[1],) + x.shape[2:])


def _batch_major(x, b):
    return jnp.swapaxes(x.reshape(x.shape[0] // b, b, x.shape[1]), 0, 1)


def _routing_tables(counts, n_tiles):
    tiles = (counts + FFN_ROWS - 1) // FFN_ROWS
    tile_ends = jnp.cumsum(tiles)
    off = (tile_ends - tiles) * FFN_ROWS
    n_valid = tile_ends[-1:]
    tile_ids = jnp.minimum(jnp.arange(n_tiles, dtype=i32), n_valid - 1)
    tile_e = jnp.sum((tile_ids[:, None] >= tile_ends[None, :]).astype(i32), axis=1)
    tile_first = jnp.concatenate([jnp.ones((1,), i32), (tile_e[1:] != tile_e[:-1]).astype(i32)])
    return off.astype(i32), tile_e.astype(i32), tile_first, n_valid.astype(i32)


def _layer(xp, xs_tm, pp, ps_tm, state_h, state_conv, state_pool, lw):
    (norm_mix_pre, w_in, conv_w, conv_b, w_rgate, b_rgate, w_igate, b_igate, lru_lambda,
     pool_w, pool_b, pool_scale, norm_group_a, norm_group_b, w_out, norm_mix_post,
     norm_ffn_pre, w_router, b_router, w_gate, b_gate, w_up, b_up, w_down, b_down, norm_ffn_post,
     w_ple, w_ple_gate, b_ple_gate, norm_ple) = lw
    b_p, s_p = xp.shape[0], xp.shape[1]
    n_p, n_s = b_p * s_p, xs_tm.shape[0]
    b_s = state_h.shape[0]
    row = lambda a: a.reshape(1, -1)

    wr_hi, wr_lo = _split(_block_diag_pairs(w_rgate))
    wi_hi, wi_lo = _split(_block_diag_pairs(w_igate))
    mix_w = (row(norm_mix_pre), w_in.astype(bf16), conv_w, row(conv_b),
             wr_hi, wr_lo, row(b_rgate), wi_hi, wi_lo, row(b_igate), row(lru_lambda),
             pool_w.astype(bf16), row(pool_b), row(pool_scale), row(norm_group_a), row(norm_group_b),
             w_out.astype(bf16), row(norm_mix_post))

    zeros = lambda *s: jnp.zeros(s, f32)
    h1_p, hfin_p, cfin_p, pfin_p = _mixer(
        xp, zeros((CONV_W - 1) * b_p, W_A), zeros(POOL_BUF * b_p, W_B), zeros(b_p, W_A),
        mix_w, bt=b_p, start=0)
    h1_s, hfin_s, cfin_s, pfin_s = _mixer(
        xs_tm, _time_major(state_conv), _time_major(state_pool), state_h,
        mix_w, bt=b_s, start=PAST_LEN)
    h1_p = h1_p.reshape(n_p, D_MODEL)
    t = n_p + n_s

    w_router_pad = jnp.zeros((D_MODEL, LANES), f32).at[:, :N_EXPERTS].set(w_router)
    b_router_pad = jnp.zeros((1, LANES), f32).at[0, :N_EXPERTS].set(b_router)
    rt_hi, rt_lo = _split(w_router_pad)
    tri = jnp.triu(jnp.ones((ROUTE_ROWS, ROUTE_ROWS), bf16), k=1)
    vpk, idx_t, rank_t, pw, cnt = _router(h1_p, h1_s, row(norm_ffn_pre), rt_hi, rt_lo, b_router_pad, tri)

    n_tiles = (t * TOP_K) // FFN_ROWS + N_EXPERTS
    off, tile_e, tile_first, n_valid = _routing_tables(cnt[:, 0], n_tiles)
    idx_flat, rank_flat = idx_t.reshape(-1), rank_t.reshape(-1)

    xs = _sc_dispatch(vpk, idx_flat, rank_flat, off, n_tiles * FFN_ROWS)
    ys = _ffn(tile_e, tile_first, n_valid, xs, w_gate, b_gate, w_up, b_up, w_down, b_down)
    g = _sc_collect(ys, idx_flat, rank_flat, off).reshape(TOP_K, t, HALF)
    out_p, out_s = _finish(g, h1_p, h1_s, pw, pp.reshape(n_p, PLE_DIM), ps_tm, row(norm_ffn_post),
                           w_ple.astype(bf16), w_ple_gate.astype(bf16), row(b_ple_gate), row(norm_ple))

    states = (hfin_p, _batch_major(cfin_p, b_p), _batch_major(pfin_p, b_p),
              hfin_s, _batch_major(cfin_s, b_s), _batch_major(pfin_s, b_s))
    return out_p.reshape(b_p, s_p, D_MODEL), out_s, states


def kernel(x_prompt, x_sample, state_rglru_h, state_rglru_conv, state_pool, p_prompt, p_sample, norm_mix_pre, w_in, conv_w, conv_b, w_rgate, b_rgate, w_igate, b_igate, lru_lambda, pool_w, pool_b, pool_scale, norm_group_a, norm_group_b, w_out, norm_mix_post, norm_ffn_pre, w_router, b_router, w_gate, b_gate, w_up, b_up, w_down, b_down, norm_ffn_post, w_ple, w_ple_gate, b_ple_gate, norm_ple):
    depth = w_in.shape[0]
    b_p, b_s = x_prompt.shape[0], x_sample.shape[0]
    per_layer = (norm_mix_pre, w_in, conv_w, conv_b, w_rgate, b_rgate, w_igate, b_igate, lru_lambda,
                 pool_w, pool_b, pool_scale, norm_group_a, norm_group_b, w_out, norm_mix_post,
                 norm_ffn_pre, w_router, b_router, w_gate, b_gate, w_up, b_up, w_down, b_down,
                 norm_ffn_post, w_ple, w_ple_gate, b_ple_gate, norm_ple)
    hp, hs = x_prompt, _time_major(x_sample)
    collected = []
    for i in range(depth):
        hp, hs, states = _layer(hp, hs, p_prompt[i], _time_major(p_sample[i]),
                                state_rglru_h[i], state_rglru_conv[i], state_pool[i],
                                tuple(w[i] for w in per_layer))
        collected.append(states)
    stacked = tuple(jnp.stack([c[j] for c in collected]) for j in range(6))
    return (hp, _batch_major(hs, b_s)) + stacked
```

```python
import dataclasses
import functools

import jax
import jax.numpy as jnp
from jax import lax
from jax.experimental import pallas as pl
from jax.experimental.pallas import tpu as pltpu
from jax.experimental.pallas import tpu_sc as plsc

D_MODEL = 1024
W_A = 512
W_B = 512
N_HEADS_A = 8
HEAD_A = W_A // N_HEADS_A
CONV_W = 4
LRU_C = 8.0
POOL_WINDOWS = (2, 4, 8, 16)
GROUP_B = W_B // len(POOL_WINDOWS)
POOL_BUF = max(POOL_WINDOWS) - 1
N_EXPERTS = 32
TOP_K = 4
D_FF = 1024
SWIGLU_LIMIT = 7.0
SWIGLU_ALPHA = 1.702
PLE_DIM = 256
EPS = 1e-6
PAST_LEN = 16384

LANES = 128
MXU_DIM = 256
HALF = D_MODEL // 2
SC_CORES = 2
SC_SUBCORES = 16
SC_WORKERS = SC_CORES * SC_SUBCORES
SC_LANES = 16
SC_CHUNK = 64

MIX_ROWS = 512
ROUTE_ROWS = 512
FFN_ROWS = 512
FFN_SUB = 256
FIN_ROWS = 256
VMEM_LIMIT = 56 * 1024 * 1024

f32 = jnp.float32
bf16 = jnp.bfloat16
u32 = jnp.uint32
i32 = jnp.int32


def _rms(x, g):
    return x * lax.rsqrt(jnp.mean(x * x, axis=-1, keepdims=True) + EPS) * g


def _split(x):
    hi = x.astype(bf16)
    lo = (x - hi.astype(f32)).astype(bf16)
    return hi, lo


def _dot(a, b):
    return jnp.dot(a, b, preferred_element_type=f32)


def _dot3(x, w_hi, w_lo):
    x_hi, x_lo = _split(x)
    return _dot(x_hi, w_hi) + (_dot(x_hi, w_lo) + _dot(x_lo, w_hi))


def _pack_rows(x):
    bits = lax.bitcast_convert_type(x.astype(bf16).astype(f32), u32)
    return (bits[:, HALF:] & jnp.uint32(0xFFFF0000)) | (bits[:, :HALF] >> 16)


def _unpack_rows(p):
    lo = lax.bitcast_convert_type(p << 16, f32)
    hi = lax.bitcast_convert_type(p & jnp.uint32(0xFFFF0000), f32)
    return jnp.concatenate([lo, hi], axis=-1)


def _mixer_kernel(x_ref, conv0_ref, pool0_ref, h0_ref, g_pre_ref, w_in_ref, conv_w_ref, conv_b_ref,
                  w_gates_ref, br_ref, bi_ref, lam_ref,
                  pool_w_ref, pool_b_ref, pool_scale_ref, ng_a_ref, ng_b_ref, w_out_ref, g_post_ref,
                  h1_ref, hfin_ref, convfin_ref, poolfin_ref,
                  h_s, conv_s, pool_s, a_s, b_s, hs_s, *, bt, tt, start):
    step = pl.program_id(0)
    rows = bt * tt

    @pl.when(step == 0)
    def _():
        h_s[...] = h0_ref[...]
        conv_s[...] = conv0_ref[...]
        pool_s[...] = pool0_ref[...]

    if x_ref.ndim == 3:
        x = pltpu.einshape("btd->tbd", x_ref[...]).reshape(rows, D_MODEL)
    else:
        x = x_ref[...]
    u = _rms(x, g_pre_ref[...]).astype(bf16)
    z = _dot(u, w_in_ref[...])
    xa, ga, xb = z[:, :W_A], z[:, W_A:2 * W_A], z[:, 2 * W_A:]

    ext_a = jnp.concatenate([conv_s[...], xa], axis=0)
    xc = conv_b_ref[...]
    for k in range(CONV_W):
        xc = xc + ext_a[k * bt:k * bt + rows] * conv_w_ref[k:k + 1, :]
    conv_s[...] = ext_a[rows:]

    xc16 = xc.astype(bf16)
    gates = [_dot(xc16[:, j * MXU_DIM:(j + 1) * MXU_DIM], w_gates_ref[j]) for j in range(W_A // MXU_DIM)]
    r = jax.nn.sigmoid(jnp.concatenate([gj[:, :MXU_DIM] for gj in gates], axis=-1) + br_ref[...])
    ig = jax.nn.sigmoid(jnp.concatenate([gj[:, MXU_DIM:] for gj in gates], axis=-1) + bi_ref[...])
    lam = lam_ref[...]
    softplus_neg = jnp.maximum(-lam, 0.0) + jnp.log1p(jnp.exp(-jnp.abs(lam)))
    log_a = (-LRU_C) * r * softplus_neg
    a_s[...] = jnp.exp(log_a)
    th = jnp.tanh(log_a)
    b_s[...] = jnp.sqrt(-2.0 * th / (1.0 - th)) * (ig * xc)

    def scan_step(t, h):
        sl = pl.ds(pl.multiple_of(t * bt, bt), bt)
        h = a_s[sl, :] * h + b_s[sl, :]
        hs_s[sl, :] = h
        return h

    h_last = lax.fori_loop(0, tt, scan_step, h_s[...], unroll=True)
    h_s[...] = h_last
    ya = hs_s[...] * jax.nn.gelu(ga)

    ext_b = jnp.concatenate([pool_s[...], xb], axis=0)
    pool_s[...] = ext_b[rows:]
    s2 = ext_b[bt:, :] + ext_b[:-bt, :]
    s4 = s2[2 * bt:, GROUP_B:] + s2[:-2 * bt, GROUP_B:]
    s8 = s4[4 * bt:, GROUP_B:] + s4[:-4 * bt, GROUP_B:]
    s16 = s8[8 * bt:, GROUP_B:] + s8[:-8 * bt, GROUP_B:]
    wins = (s2[14 * bt:, :GROUP_B], s4[12 * bt:, :GROUP_B], s8[8 * bt:, :GROUP_B], s16)
    t_idx = lax.broadcasted_iota(i32, (rows, GROUP_B), 0) // bt
    pos1 = (t_idx + (step * tt + start + 1)).astype(f32)
    yb_parts = []
    for g, w in enumerate(POOL_WINDOWS):
        cnt = jnp.minimum(jnp.float32(w), pos1)
        d = wins[g] / cnt - xb[:, g * GROUP_B:(g + 1) * GROUP_B]
        yb_parts.append(_dot(d.astype(bf16), pool_w_ref[g]))
    yb = (jnp.concatenate(yb_parts, axis=-1) + pool_b_ref[...]) * pool_scale_ref[...]

    na = _rms(ya, ng_a_ref[...])
    nb = _rms(yb, ng_b_ref[...])
    m = _dot(jnp.concatenate([na, nb], axis=-1).astype(bf16), w_out_ref[...])
    h1 = x + _rms(m, g_post_ref[...])
    if h1_ref.ndim == 3:
        h1_ref[...] = pltpu.einshape("tbd->btd", h1.reshape(tt, bt, D_MODEL))
    else:
        h1_ref[...] = h1

    hfin_ref[...] = h_last
    convfin_ref[...] = conv_s[...]
    poolfin_ref[...] = pool_s[...]


def _full(shape):
    return pl.BlockSpec(shape, lambda i, *_: (0,) * len(shape))


def _mixer(x, conv0, pool0, h0, wts, *, bt, start):
    tt = MIX_ROWS // bt
    rows = bt * tt
    if x.ndim == 3:
        assert x.shape[0] == bt and x.shape[1] % tt == 0
        n_steps = x.shape[1] // tt
        x_spec = pl.BlockSpec((bt, tt, D_MODEL), lambda i: (0, i, 0))
    else:
        assert x.shape[0] % rows == 0
        n_steps = x.shape[0] // rows
        x_spec = pl.BlockSpec((rows, D_MODEL), lambda i: (i, 0))
    kern = functools.partial(_mixer_kernel, bt=bt, tt=tt, start=start)
    small = [conv0, pool0, h0] + list(wts)
    return pl.pallas_call(
        kern,
        grid=(n_steps,),
        in_specs=[x_spec] + [_full(a.shape) for a in small],
        out_specs=[x_spec,
                   _full((bt, W_A)), _full(((CONV_W - 1) * bt, W_A)), _full((POOL_BUF * bt, W_B))],
        out_shape=[jax.ShapeDtypeStruct(x.shape, f32),
                   jax.ShapeDtypeStruct((bt, W_A), f32),
                   jax.ShapeDtypeStruct(((CONV_W - 1) * bt, W_A), f32),
                   jax.ShapeDtypeStruct((POOL_BUF * bt, W_B), f32)],
        scratch_shapes=[pltpu.VMEM((bt, W_A), f32),
                        pltpu.VMEM(((CONV_W - 1) * bt, W_A), f32),
                        pltpu.VMEM((POOL_BUF * bt, W_B), f32),
                        pltpu.VMEM((rows, W_A), f32),
                        pltpu.VMEM((rows, W_A), f32),
                        pltpu.VMEM((rows, W_A), f32)],
        compiler_params=pltpu.CompilerParams(dimension_semantics=("arbitrary",),
                                             vmem_limit_bytes=VMEM_LIMIT),
        name="mixer",
    )(x, *small)


def _two_groups(step, first_steps, a_ref, b_ref):
    return jnp.where(step < first_steps, a_ref[...], b_ref[...])


def _router_kernel(h1a_ref, h1b_ref, g_ref, wr_hi_ref, wr_lo_ref, br_ref, tri_ref,
                   vpk_ref, idx_ref, rank_ref, pw_ref, cnt_ref, carry_s, *, first_steps):
    step = pl.program_id(0)
    rows = h1a_ref.shape[0]

    @pl.when(step == 0)
    def _():
        carry_s[...] = jnp.zeros_like(carry_s)

    v = _rms(_two_groups(step, first_steps, h1a_ref, h1b_ref), g_ref[...])
    vpk_ref[...] = _pack_rows(v)
    logits = _dot3(v, wr_hi_ref[...], wr_lo_ref[...]) + br_ref[...]
    lt = jnp.transpose(logits)[:N_EXPERTS, :]

    eio = lax.broadcasted_iota(i32, (N_EXPERTS, rows), 0).astype(f32)
    work = lt
    vals, idxs, sels = [], [], []
    for _ in range(TOP_K):
        m = jnp.max(work, axis=0, keepdims=True)
        ik = jnp.min(jnp.where(work == m, eio, float(N_EXPERTS)), axis=0, keepdims=True)
        sel = eio == ik
        vals.append(m)
        idxs.append(ik)
        sels.append(sel)
        work = jnp.where(sel, -jnp.inf, work)
    exps = [jnp.exp(val - vals[0]) for val in vals]
    denom = exps[0] + exps[1] + exps[2] + exps[3]
    probs = [e / denom for e in exps]

    chosen = sels[0] | sels[1] | sels[2] | sels[3]
    onehot = jnp.where(chosen, 1.0, 0.0)
    before = _dot(onehot.astype(bf16), tri_ref[...])
    base = carry_s[...] + before
    ranks = [jnp.sum(jnp.where(sel, base, 0.0), axis=0, keepdims=True) for sel in sels]
    carry_s[...] = carry_s[...] + jnp.sum(onehot, axis=1, keepdims=True)

    idx_ref[...] = jnp.concatenate(idxs, axis=0).astype(i32)
    rank_ref[...] = jnp.concatenate(ranks, axis=0).astype(i32)
    pad = jnp.zeros((LANES - TOP_K, rows), f32)
    pw_ref[...] = jnp.transpose(jnp.concatenate(probs + [pad], axis=0))
    cnt_ref[...] = carry_s[:, :LANES].astype(i32)


def _group_specs(rows, width, na, nb):
    first = na // rows
    return (pl.BlockSpec((rows, width), lambda i: (jnp.minimum(i, first - 1), 0)),
            pl.BlockSpec((rows, width), lambda i: (jnp.maximum(i - first, 0), 0)))


def _router(h1a, h1b, g, wr_hi, wr_lo, br, tri):
    na, nb = h1a.shape[0], h1b.shape[0]
    t = na + nb
    rows = ROUTE_ROWS
    assert na % rows == 0 and nb % rows == 0
    return pl.pallas_call(
        functools.partial(_router_kernel, first_steps=na // rows),
        grid=(t // rows,),
        in_specs=[*_group_specs(rows, D_MODEL, na, nb),
                  _full(g.shape), _full(wr_hi.shape), _full(wr_lo.shape), _full(br.shape), _full(tri.shape)],
        out_specs=[pl.BlockSpec((rows, HALF), lambda i: (i, 0)),
                   pl.BlockSpec((TOP_K, rows), lambda i: (0, i)),
                   pl.BlockSpec((TOP_K, rows), lambda i: (0, i)),
                   pl.BlockSpec((rows, LANES), lambda i: (i, 0)),
                   _full((N_EXPERTS, LANES))],
        out_shape=[jax.ShapeDtypeStruct((t, HALF), u32),
                   jax.ShapeDtypeStruct((TOP_K, t), i32),
                   jax.ShapeDtypeStruct((TOP_K, t), i32),
                   jax.ShapeDtypeStruct((t, LANES), f32),
                   jax.ShapeDtypeStruct((N_EXPERTS, LANES), i32)],
        scratch_shapes=[pltpu.VMEM((N_EXPERTS, rows), f32)],
        compiler_params=pltpu.CompilerParams(dimension_semantics=("arbitrary",),
                                             vmem_limit_bytes=VMEM_LIMIT),
        name="router",
    )(h1a, h1b, g, wr_hi, wr_lo, br, tri)


def _sc_mesh():
    return plsc.VectorSubcoreMesh(core_axis_name="core", subcore_axis_name="subcore")


def _sc_params():
    return dataclasses.replace(pltpu.CompilerParams(), needs_layout_passes=False)


def _sc_worker():
    return lax.axis_index("subcore") * SC_CORES + lax.axis_index("core")


def _sc_positions(idx_v, rank_v, off_v, n):
    @pl.loop(0, n // SC_LANES)
    def _(i):
        sl = pl.ds(pl.multiple_of(i * SC_LANES, SC_LANES), SC_LANES)
        idx_v[sl] = plsc.load_gather(off_v, [idx_v[sl]]) + rank_v[sl]


def _sc_copy_rows(src_hbm, idx_v, dst_hbm, dst_row0, n, rows_v, gsem, wsem):
    n_ch = n // SC_CHUNK
    n_pair = n_ch // 2

    def gather(c, b):
        o = pl.multiple_of(c * SC_CHUNK, SC_CHUNK)
        return pltpu.make_async_copy(src_hbm.at[idx_v.at[pl.ds(o, SC_CHUNK)]], rows_v.at[b], gsem.at[b])

    def write(c, b):
        o = pl.multiple_of(c * SC_CHUNK, SC_CHUNK)
        return pltpu.make_async_copy(rows_v.at[b], dst_hbm.at[pl.ds(dst_row0 + o, SC_CHUNK)], wsem.at[b])

    gather(0, 0).start()

    @pl.loop(0, n_pair)
    def _(i):
        c0 = 2 * i
        gather(c0, 0).wait()

        @pl.when(i > 0)
        def _():
            write(c0 - 1, 1).wait()

        gather(c0 + 1, 1).start()
        write(c0, 0).start()
        gather(c0 + 1, 1).wait()
        write(c0, 0).wait()

        @pl.when(c0 + 2 < n_ch)
        def _():
            gather(c0 + 2, 0).start()

        write(c0 + 1, 1).start()

    if n_ch % 2:
        gather(n_ch - 1, 0).wait()
        write(n_ch - 1, 0).start()
        write(n_ch - 1, 0).wait()
    if n_pair:
        write(2 * n_pair - 1, 1).wait()


def _sc_dispatch(vpk, idx_flat, rank_flat, off, n_out):
    n_tok = vpk.shape[0]
    per_w = n_out // SC_WORKERS
    assert n_out % (SC_WORKERS * SC_CHUNK) == 0 and n_tok % SC_LANES == 0

    @pl.kernel(out_type=jax.ShapeDtypeStruct((n_out, HALF), u32), mesh=_sc_mesh(),
               compiler_params=_sc_params(), name="dispatch",
               scratch_types=[pltpu.VMEM((per_w,), i32), pltpu.VMEM((n_tok,), i32), pltpu.VMEM((n_tok,), i32),
                              pltpu.VMEM((N_EXPERTS,), i32), pltpu.VMEM((2, SC_CHUNK, HALF), u32),
                              pltpu.SemaphoreType.DMA((2,)), pltpu.SemaphoreType.DMA((2,))])
    def k(v_hbm, i_hbm, r_hbm, off_hbm, o_hbm, src_v, idx_v, rank_v, off_v, rows_v, gsem, wsem):
        lo = _sc_worker() * per_w
        pltpu.sync_copy(off_hbm, off_v)

        @pl.loop(0, per_w // SC_LANES)
        def _(i):
            o = pl.multiple_of(i * SC_LANES, SC_LANES)
            src_v[pl.ds(o, SC_LANES)] = lax.iota(i32, SC_LANES) + lax.rem(lo + o, n_tok)

        @pl.loop(0, TOP_K)
        def _(kk):
            pltpu.sync_copy(i_hbm.at[pl.ds(kk * n_tok, n_tok)], idx_v)
            pltpu.sync_copy(r_hbm.at[pl.ds(kk * n_tok, n_tok)], rank_v)

            @plsc.parallel_loop(0, n_tok, step=SC_LANES, unroll=8)
            def _(o):
                sl = pl.ds(pl.multiple_of(o, SC_LANES), SC_LANES)
                p = plsc.load_gather(off_v, [idx_v[sl]]) + rank_v[sl] - lo
                mine = (p >= 0) & (p < per_w)
                tok = lax.iota(i32, SC_LANES) + o
                plsc.store_scatter(src_v, [jnp.where(mine, p, 0)], tok, mask=mine)

        _sc_copy_rows(v_hbm, src_v, o_hbm, lo, per_w, rows_v, gsem, wsem)

    return k(vpk, idx_flat, rank_flat, off)


def _sc_collect(ys, idx_flat, rank_flat, off):
    n = idx_flat.shape[0]
    per_w = n // SC_WORKERS
    assert n % (SC_WORKERS * SC_CHUNK) == 0

    @pl.kernel(out_type=jax.ShapeDtypeStruct((n, HALF), u32), mesh=_sc_mesh(),
               compiler_params=_sc_params(), name="collect",
               scratch_types=[pltpu.VMEM((per_w,), i32), pltpu.VMEM((per_w,), i32),
                              pltpu.VMEM((N_EXPERTS,), i32), pltpu.VMEM((2, SC_CHUNK, HALF), u32),
                              pltpu.SemaphoreType.DMA((2,)), pltpu.SemaphoreType.DMA((2,))])
    def k(y_hbm, i_hbm, r_hbm, off_hbm, o_hbm, idx_v, rank_v, off_v, rows_v, gsem, wsem):
        lo = _sc_worker() * per_w
        pltpu.sync_copy(off_hbm, off_v)
        pltpu.sync_copy(i_hbm.at[pl.ds(lo, per_w)], idx_v)
        pltpu.sync_copy(r_hbm.at[pl.ds(lo, per_w)], rank_v)
        _sc_positions(idx_v, rank_v, off_v, per_w)
        _sc_copy_rows(y_hbm, idx_v, o_hbm, lo, per_w, rows_v, gsem, wsem)

    return k(ys, idx_flat, rank_flat, off)


def _ffn_kernel(te_ref, tf_ref, slot_ref, next_ref, nv_ref, xs_ref, wg_hbm, bg_ref, wu_hbm, bu_ref, wd_hbm, bd_ref,
                ys_ref, wbuf, wg_s, wu_s, wd_s, wsem):
    step = pl.program_id(0)
    w_hbm = (wg_hbm, wu_hbm, wd_hbm)
    w_bf16 = (wg_s, wu_s, wd_s)

    def fetch(e, slot):
        return [pltpu.make_async_copy(w_hbm[m].at[e], wbuf.at[slot, m], wsem.at[slot, m]) for m in range(3)]

    @pl.when(step < nv_ref[0])
    def _():
        @pl.when(tf_ref[step] == 1)
        def _():
            slot = slot_ref[step]

            @pl.when(step == 0)
            def _():
                for cp in fetch(te_ref[0], slot):
                    cp.start()

            for cp in fetch(te_ref[step], slot):
                cp.wait()
            for s in range(2):
                @pl.when(slot == s)
                def _():
                    for m in range(3):
                        w_bf16[m][...] = wbuf[s, m].astype(bf16)

            @pl.when(next_ref[step] >= 0)
            def _():
                for cp in fetch(next_ref[step], 1 - slot):
                    cp.start()

        for rb in range(FFN_ROWS // FFN_SUB):
            rs = slice(rb * FFN_SUB, (rb + 1) * FFN_SUB)
            x = _unpack_rows(xs_ref[rs, :]).astype(bf16)
            g = jnp.minimum(_dot(x, wg_s[...]) + bg_ref[0], SWIGLU_LIMIT)
            u = jnp.clip(_dot(x, wu_s[...]) + bu_ref[0], -SWIGLU_LIMIT, SWIGLU_LIMIT)
            hid = (u + 1.0) * (g * jax.nn.sigmoid(SWIGLU_ALPHA * g))
            y = _dot(hid.astype(bf16), wd_s[...]) + bd_ref[0]
            ys_ref[rs, :] = _pack_rows(y)

    @pl.when(step >= nv_ref[0])
    def _():
        ys_ref[...] = jnp.zeros_like(ys_ref)


def _ffn(tables, xs, w_gate, b_gate, w_up, b_up, w_down, b_down):
    n_rows = xs.shape[0]
    n_tiles = n_rows // FFN_ROWS
    assert D_FF == D_MODEL

    def row_map(i, te, tf, sl, nx, nv):
        return (jnp.minimum(i, nv[0] - 1), 0)

    def b_map(i, te, tf, sl, nx, nv):
        return (te[i], 0, 0)

    w_spec = pl.BlockSpec(memory_space=pl.ANY)
    b_spec = pl.BlockSpec((1, 1, D_FF), b_map)
    return pl.pallas_call(
        _ffn_kernel,
        grid_spec=pltpu.PrefetchScalarGridSpec(
            num_scalar_prefetch=5,
            grid=(n_tiles,),
            in_specs=[pl.BlockSpec((FFN_ROWS, HALF), row_map),
                      w_spec, b_spec, w_spec, b_spec, w_spec, b_spec],
            out_specs=pl.BlockSpec((FFN_ROWS, HALF), lambda i, *_: (i, 0)),
            scratch_shapes=[pltpu.VMEM((2, 3, D_MODEL, D_FF), f32),
                            pltpu.VMEM((D_MODEL, D_FF), bf16),
                            pltpu.VMEM((D_MODEL, D_FF), bf16),
                            pltpu.VMEM((D_FF, D_MODEL), bf16),
                            pltpu.SemaphoreType.DMA((2, 3))]),
        out_shape=jax.ShapeDtypeStruct((n_rows, HALF), u32),
        compiler_params=pltpu.CompilerParams(dimension_semantics=("arbitrary",),
                                             vmem_limit_bytes=VMEM_LIMIT),
        name="experts",
    )(*tables, xs, w_gate, b_gate.reshape(N_EXPERTS, 1, D_FF),
      w_up, b_up.reshape(N_EXPERTS, 1, D_FF), w_down, b_down.reshape(N_EXPERTS, 1, D_MODEL))


def _finish_kernel(g_ref, h1a_ref, h1b_ref, pw_ref, pa_ref, pb_ref, g_post_ref, w_ple_ref, w_pg_ref, b_pg_ref,
                   g_ple_ref, outa_ref, outb_ref, *, first_steps):
    step = pl.program_id(0)
    rows = h1a_ref.shape[0]
    pw = pw_ref[...]
    f = jnp.zeros((rows, D_MODEL), f32)
    for k in range(TOP_K):
        f = f + pw[:, k:k + 1] * _unpack_rows(g_ref[k])
    h2 = _two_groups(step, first_steps, h1a_ref, h1b_ref) + _rms(f, g_post_ref[...])
    gate = jax.nn.sigmoid(_dot(h2.astype(bf16), w_pg_ref[...]) + b_pg_ref[...])
    pin = _two_groups(step, first_steps, pa_ref, pb_ref).astype(bf16)
    pe = _dot(pin, w_ple_ref[...]) * gate
    out = h2 + _rms(pe, g_ple_ref[...])

    @pl.when(step < first_steps)
    def _():
        outa_ref[...] = out

    @pl.when(step >= first_steps)
    def _():
        outb_ref[...] = out


def _finish(g, h1a, h1b, pw, pa, pb, g_post, w_ple, w_pg, b_pg, g_ple):
    na, nb = h1a.shape[0], h1b.shape[0]
    t = na + nb
    rows = FIN_ROWS
    assert na % rows == 0 and nb % rows == 0
    return pl.pallas_call(
        functools.partial(_finish_kernel, first_steps=na // rows),
        grid=(t // rows,),
        in_specs=[pl.BlockSpec((TOP_K, rows, HALF), lambda i: (0, i, 0)),
                  *_group_specs(rows, D_MODEL, na, nb),
                  pl.BlockSpec((rows, LANES), lambda i: (i, 0)),
                  *_group_specs(rows, PLE_DIM, na, nb),
                  _full(g_post.shape), _full(w_ple.shape), _full(w_pg.shape),
                  _full(b_pg.shape), _full(g_ple.shape)],
        out_specs=list(_group_specs(rows, D_MODEL, na, nb)),
        out_shape=[jax.ShapeDtypeStruct((na, D_MODEL), f32), jax.ShapeDtypeStruct((nb, D_MODEL), f32)],
        compiler_params=pltpu.CompilerParams(dimension_semantics=("arbitrary",),
                                             vmem_limit_bytes=VMEM_LIMIT),
        name="finish",
    )(g, h1a, h1b, pw, pa, pb, g_post, w_ple, w_pg, b_pg, g_ple)


def _block_diag_pairs(w):
    per = MXU_DIM // HEAD_A
    out = jnp.zeros((N_HEADS_A // per, MXU_DIM, MXU_DIM), w.dtype)
    for h in range(N_HEADS_A):
        j, q = divmod(h, per)
        out = out.at[j, q * HEAD_A:(q + 1) * HEAD_A, q * HEAD_A:(q + 1) * HEAD_A].set(w[h])
    return out


def _time_major(x):
    x = jnp.swapaxes(x, 0, 1)
    return x.reshape((x.shape[0] * x.shape[1],) + x.shape[2:])


def _batch_major(x, b):
    return jnp.swapaxes(x.reshape(x.shape[0] // b, b, x.shape[1]), 0, 1)


def _routing_tables(counts, n_tiles):
    tiles = (counts + FFN_ROWS - 1) // FFN_ROWS
    tile_ends = jnp.cumsum(tiles)
    off = (tile_ends - tiles) * FFN_ROWS
    n_valid = tile_ends[-1:]
    tile_ids = jnp.minimum(jnp.arange(n_tiles, dtype=i32), n_valid - 1)
    tile_e = jnp.sum((tile_ids[:, None] >= tile_ends[None, :]).astype(i32), axis=1)
    tile_first = jnp.concatenate([jnp.ones((1,), i32), (tile_e[1:] != tile_e[:-1]).astype(i32)])
    tile_slot = (jnp.cumsum(tile_first) - 1) % 2
    cand = jnp.where(tiles > 0, jnp.arange(N_EXPERTS, dtype=i32), N_EXPERTS)
    suffix_min = lax.cummin(cand[::-1])[::-1]
    nxt = jnp.concatenate([suffix_min[1:], jnp.full((1,), N_EXPERTS, i32)])
    tile_next = jnp.where(nxt < N_EXPERTS, nxt, -1)[tile_e]
    tables = (tile_e.astype(i32), tile_first, tile_slot.astype(i32), tile_next.astype(i32), n_valid.astype(i32))
    return off.astype(i32), tables


def _layer(xp, xs_tm, pp, ps_tm, state_h, state_conv, state_pool, lw):
    (norm_mix_pre, w_in, conv_w, conv_b, w_rgate, b_rgate, w_igate, b_igate, lru_lambda,
     pool_w, pool_b, pool_scale, norm_group_a, norm_group_b, w_out, norm_mix_post,
     norm_ffn_pre, w_router, b_router, w_gate, b_gate, w_up, b_up, w_down, b_down, norm_ffn_post,
     w_ple, w_ple_gate, b_ple_gate, norm_ple) = lw
    b_p, s_p = xp.shape[0], xp.shape[1]
    n_p, n_s = b_p * s_p, xs_tm.shape[0]
    b_s = state_h.shape[0]
    row = lambda a: a.reshape(1, -1)

    w_gates = jnp.concatenate([_block_diag_pairs(w_rgate), _block_diag_pairs(w_igate)], axis=-1).astype(bf16)
    mix_w = (row(norm_mix_pre), w_in.astype(bf16), conv_w, row(conv_b),
             w_gates, row(b_rgate), row(b_igate), row(lru_lambda),
             pool_w.astype(bf16), row(pool_b), row(pool_scale), row(norm_group_a), row(norm_group_b),
             w_out.astype(bf16), row(norm_mix_post))

    zeros = lambda *s: jnp.zeros(s, f32)
    h1_p, hfin_p, cfin_p, pfin_p = _mixer(
        xp, zeros((CONV_W - 1) * b_p, W_A), zeros(POOL_BUF * b_p, W_B), zeros(b_p, W_A),
        mix_w, bt=b_p, start=0)
    h1_s, hfin_s, cfin_s, pfin_s = _mixer(
        xs_tm, _time_major(state_conv), _time_major(state_pool), state_h,
        mix_w, bt=b_s, start=PAST_LEN)
    h1_p = h1_p.reshape(n_p, D_MODEL)
    t = n_p + n_s

    w_router_pad = jnp.zeros((D_MODEL, LANES), f32).at[:, :N_EXPERTS].set(w_router)
    b_router_pad = jnp.zeros((1, LANES), f32).at[0, :N_EXPERTS].set(b_router)
    rt_hi, rt_lo = _split(w_router_pad)
    tri = jnp.triu(jnp.ones((ROUTE_ROWS, ROUTE_ROWS), bf16), k=1)
    vpk, idx_t, rank_t, pw, cnt = _router(h1_p, h1_s, row(norm_ffn_pre), rt_hi, rt_lo, b_router_pad, tri)

    n_tiles = (t * TOP_K) // FFN_ROWS + N_EXPERTS
    off, tables = _routing_tables(cnt[:, 0], n_tiles)
    idx_flat, rank_flat = idx_t.reshape(-1), rank_t.reshape(-1)

    xs = _sc_dispatch(vpk, idx_flat, rank_flat, off, n_tiles * FFN_ROWS)
    ys = _ffn(tables, xs, w_gate, b_gate, w_up, b_up, w_down, b_down)
    g = _sc_collect(ys, idx_flat, rank_flat, off).reshape(TOP_K, t, HALF)
    out_p, out_s = _finish(g, h1_p, h1_s, pw, pp.reshape(n_p, PLE_DIM), ps_tm, row(norm_ffn_post),
                           w_ple.astype(bf16), w_ple_gate.astype(bf16), row(b_ple_gate), row(norm_ple))

    states = (hfin_p, _batch_major(cfin_p, b_p), _batch_major(pfin_p, b_p),
              hfin_s, _batch_major(cfin_s, b_s), _batch_major(pfin_s, b_s))
    return out_p.reshape(b_p, s_p, D_MODEL), out_s, states


def kernel(x_prompt, x_sample, state_rglru_h, state_rglru_conv, state_pool, p_prompt, p_sample, norm_mix_pre, w_in, conv_w, conv_b, w_rgate, b_rgate, w_igate, b_igate, lru_lambda, pool_w, pool_b, pool_scale, norm_group_a, norm_group_b, w_out, norm_mix_post, norm_ffn_pre, w_router, b_router, w_gate, b_gate, w_up, b_up, w_down, b_down, norm_ffn_post, w_ple, w_ple_gate, b_ple_gate, norm_ple):
    depth = w_in.shape[0]
    b_p, b_s = x_prompt.shape[0], x_sample.shape[0]
    per_layer = (norm_mix_pre, w_in, conv_w, conv_b, w_rgate, b_rgate, w_igate, b_igate, lru_lambda,
                 pool_w, pool_b, pool_scale, norm_group_a, norm_group_b, w_out, norm_mix_post,
                 norm_ffn_pre, w_router, b_router, w_gate, b_gate, w_up, b_up, w_down, b_down,
                 norm_ffn_post, w_ple, w_ple_gate, b_ple_gate, norm_ple)
    hp, hs = x_prompt, _time_major(x_sample)
    collected = []
    for i in range(depth):
        hp, hs, states = _layer(hp, hs, p_prompt[i], _time_major(p_sample[i]),
                                state_rglru_h[i], state_rglru_conv[i], state_pool[i],
                                tuple(w[i] for w in per_layer))
        collected.append(states)
    stacked = tuple(jnp.stack([c[j] for c in collected]) for j in range(6))
    return (hp, _batch_major(hs, b_s)) + stacked
```

```python
import dataclasses
import functools

import jax
import jax.numpy as jnp
from jax import lax
from jax.experimental import pallas as pl
from jax.experimental.pallas import tpu as pltpu
from jax.experimental.pallas import tpu_sc as plsc

D_MODEL = 1024
W_A = 512
W_B = 512
N_HEADS_A = 8
HEAD_A = W_A // N_HEADS_A
CONV_W = 4
LRU_C = 8.0
POOL_WINDOWS = (2, 4, 8, 16)
GROUP_B = W_B // len(POOL_WINDOWS)
POOL_BUF = max(POOL_WINDOWS) - 1
N_EXPERTS = 32
TOP_K = 4
D_FF = 1024
SWIGLU_LIMIT = 7.0
SWIGLU_ALPHA = 1.702
PLE_DIM = 256
EPS = 1e-6
PAST_LEN = 16384

LANES = 128
MXU_DIM = 256
HALF = D_MODEL // 2
SC_CORES = 2
SC_SUBCORES = 16
SC_WORKERS = SC_CORES * SC_SUBCORES
SC_LANES = 16
SC_CHUNK = 64

MIX_ROWS = 512
ROUTE_ROWS = 512
FFN_ROWS = 512
FFN_SUB = 256
FIN_ROWS = 512
VMEM_LIMIT = 56 * 1024 * 1024

f32 = jnp.float32
bf16 = jnp.bfloat16
u32 = jnp.uint32
i32 = jnp.int32


def _rms(x, g):
    return x * lax.rsqrt(jnp.mean(x * x, axis=-1, keepdims=True) + EPS) * g


def _dot(a, b):
    return jnp.dot(a, b, preferred_element_type=f32)


def _pack_rows(x):
    bits = lax.bitcast_convert_type(x.astype(bf16).astype(f32), u32)
    return (bits[:, HALF:] & jnp.uint32(0xFFFF0000)) | (bits[:, :HALF] >> 16)


def _unpack_rows(p):
    lo = lax.bitcast_convert_type(p << 16, f32)
    hi = lax.bitcast_convert_type(p & jnp.uint32(0xFFFF0000), f32)
    return jnp.concatenate([lo, hi], axis=-1)


def _mixer_kernel(x_ref, conv0_ref, pool0_ref, h0_ref, g_pre_ref, w_in_ref, conv_w_ref, conv_b_ref,
                  w_gates_ref, br_ref, bi_ref, lam_ref,
                  pool_w_ref, pool_b_ref, pool_scale_ref, ng_a_ref, ng_b_ref, w_out_ref, g_post_ref,
                  h1_ref, hfin_ref, convfin_ref, poolfin_ref,
                  h_s, conv_s, pool_s, a_s, b_s, hs_s, *, bt, tt, start):
    step = pl.program_id(0)
    rows = bt * tt

    @pl.when(step == 0)
    def _():
        h_s[...] = h0_ref[...]
        conv_s[...] = conv0_ref[...]
        pool_s[...] = pool0_ref[...]

    if x_ref.ndim == 3:
        x = pltpu.einshape("btd->tbd", x_ref[...]).reshape(rows, D_MODEL)
    else:
        x = x_ref[...]
    u = _rms(x, g_pre_ref[...]).astype(bf16)
    z = _dot(u, w_in_ref[...])
    xa, ga, xb = z[:, :W_A], z[:, W_A:2 * W_A], z[:, 2 * W_A:]

    ext_a = jnp.concatenate([conv_s[...], xa], axis=0)
    xc = conv_b_ref[...]
    for k in range(CONV_W):
        xc = xc + ext_a[k * bt:k * bt + rows] * conv_w_ref[k:k + 1, :]
    conv_s[...] = ext_a[rows:]

    xc16 = xc.astype(bf16)
    gates = [_dot(xc16[:, j * MXU_DIM:(j + 1) * MXU_DIM], w_gates_ref[j]) for j in range(W_A // MXU_DIM)]
    r = jax.nn.sigmoid(jnp.concatenate([gj[:, :MXU_DIM] for gj in gates], axis=-1) + br_ref[...])
    ig = jax.nn.sigmoid(jnp.concatenate([gj[:, MXU_DIM:] for gj in gates], axis=-1) + bi_ref[...])
    lam = lam_ref[...]
    softplus_neg = jnp.maximum(-lam, 0.0) + jnp.log1p(jnp.exp(-jnp.abs(lam)))
    log_a = (-LRU_C) * r * softplus_neg
    a_s[...] = jnp.exp(log_a)
    th = jnp.tanh(log_a)
    b_s[...] = jnp.sqrt(-2.0 * th / (1.0 - th)) * (ig * xc)

    def scan_step(t, h):
        sl = pl.ds(pl.multiple_of(t * bt, bt), bt)
        h = a_s[sl, :] * h + b_s[sl, :]
        hs_s[sl, :] = h
        return h

    h_last = lax.fori_loop(0, tt, scan_step, h_s[...], unroll=True)
    h_s[...] = h_last
    ya = hs_s[...] * jax.nn.gelu(ga)

    ext_b = jnp.concatenate([pool_s[...], xb], axis=0)
    pool_s[...] = ext_b[rows:]
    s2 = ext_b[bt:, :] + ext_b[:-bt, :]
    s4 = s2[2 * bt:, GROUP_B:] + s2[:-2 * bt, GROUP_B:]
    s8 = s4[4 * bt:, GROUP_B:] + s4[:-4 * bt, GROUP_B:]
    s16 = s8[8 * bt:, GROUP_B:] + s8[:-8 * bt, GROUP_B:]
    wins = (s2[14 * bt:, :GROUP_B], s4[12 * bt:, :GROUP_B], s8[8 * bt:, :GROUP_B], s16)
    t_idx = lax.broadcasted_iota(i32, (rows, GROUP_B), 0) // bt
    pos1 = (t_idx + (step * tt + start + 1)).astype(f32)
    yb_parts = []
    for g, w in enumerate(POOL_WINDOWS):
        cnt = jnp.minimum(jnp.float32(w), pos1)
        d = wins[g] / cnt - xb[:, g * GROUP_B:(g + 1) * GROUP_B]
        yb_parts.append(_dot(d.astype(bf16), pool_w_ref[g]))
    yb = (jnp.concatenate(yb_parts, axis=-1) + pool_b_ref[...]) * pool_scale_ref[...]

    na = _rms(ya, ng_a_ref[...])
    nb = _rms(yb, ng_b_ref[...])
    m = _dot(jnp.concatenate([na, nb], axis=-1).astype(bf16), w_out_ref[...])
    h1 = x + _rms(m, g_post_ref[...])
    if h1_ref.ndim == 3:
        h1_ref[...] = pltpu.einshape("tbd->btd", h1.reshape(tt, bt, D_MODEL))
    else:
        h1_ref[...] = h1

    hfin_ref[...] = h_last
    convfin_ref[...] = conv_s[...]
    poolfin_ref[...] = pool_s[...]


def _full(shape):
    return pl.BlockSpec(shape, lambda i, *_: (0,) * len(shape))


def _mixer(x, conv0, pool0, h0, wts, *, bt, start):
    tt = MIX_ROWS // bt
    rows = bt * tt
    if x.ndim == 3:
        assert x.shape[0] == bt and x.shape[1] % tt == 0
        n_steps = x.shape[1] // tt
        x_spec = pl.BlockSpec((bt, tt, D_MODEL), lambda i: (0, i, 0))
    else:
        assert x.shape[0] % rows == 0
        n_steps = x.shape[0] // rows
        x_spec = pl.BlockSpec((rows, D_MODEL), lambda i: (i, 0))
    kern = functools.partial(_mixer_kernel, bt=bt, tt=tt, start=start)
    small = [conv0, pool0, h0] + list(wts)
    return pl.pallas_call(
        kern,
        grid=(n_steps,),
        in_specs=[x_spec] + [_full(a.shape) for a in small],
        out_specs=[x_spec,
                   _full((bt, W_A)), _full(((CONV_W - 1) * bt, W_A)), _full((POOL_BUF * bt, W_B))],
        out_shape=[jax.ShapeDtypeStruct(x.shape, f32),
                   jax.ShapeDtypeStruct((bt, W_A), f32),
                   jax.ShapeDtypeStruct(((CONV_W - 1) * bt, W_A), f32),
                   jax.ShapeDtypeStruct((POOL_BUF * bt, W_B), f32)],
        scratch_shapes=[pltpu.VMEM((bt, W_A), f32),
                        pltpu.VMEM(((CONV_W - 1) * bt, W_A), f32),
                        pltpu.VMEM((POOL_BUF * bt, W_B), f32),
                        pltpu.VMEM((rows, W_A), f32),
                        pltpu.VMEM((rows, W_A), f32),
                        pltpu.VMEM((rows, W_A), f32)],
        compiler_params=pltpu.CompilerParams(dimension_semantics=("arbitrary",),
                                             vmem_limit_bytes=VMEM_LIMIT),
        name="mixer",
    )(x, *small)


def _two_groups(step, first_steps, a_ref, b_ref):
    return jnp.where(step < first_steps, a_ref[...], b_ref[...])


def _router_kernel(h1a_ref, h1b_ref, g_ref, wr_ref, br_ref, tri_ref,
                   vpk_ref, idx_ref, rank_ref, pw_ref, cnt_ref, carry_s, *, first_steps):
    step = pl.program_id(0)
    rows = h1a_ref.shape[0]

    @pl.when(step == 0)
    def _():
        carry_s[...] = jnp.zeros_like(carry_s)

    v = _rms(_two_groups(step, first_steps, h1a_ref, h1b_ref), g_ref[...])
    vpk_ref[...] = _pack_rows(v)
    logits = _dot(v.astype(bf16), wr_ref[...]) + br_ref[...]
    lt = jnp.transpose(logits)[:N_EXPERTS, :]

    eio = lax.broadcasted_iota(i32, (N_EXPERTS, rows), 0).astype(f32)
    work = lt
    vals, idxs, sels = [], [], []
    for _ in range(TOP_K):
        m = jnp.max(work, axis=0, keepdims=True)
        ik = jnp.min(jnp.where(work == m, eio, float(N_EXPERTS)), axis=0, keepdims=True)
        sel = eio == ik
        vals.append(m)
        idxs.append(ik)
        sels.append(sel)
        work = jnp.where(sel, -jnp.inf, work)
    exps = [jnp.exp(val - vals[0]) for val in vals]
    denom = exps[0] + exps[1] + exps[2] + exps[3]
    probs = [e / denom for e in exps]

    chosen = sels[0] | sels[1] | sels[2] | sels[3]
    onehot = jnp.where(chosen, 1.0, 0.0)
    before = _dot(onehot.astype(bf16), tri_ref[...])
    base = carry_s[...] + before
    ranks = [jnp.sum(jnp.where(sel, base, 0.0), axis=0, keepdims=True) for sel in sels]
    carry_s[...] = carry_s[...] + jnp.sum(onehot, axis=1, keepdims=True)

    idx_ref[...] = jnp.concatenate(idxs, axis=0).astype(i32)
    rank_ref[...] = jnp.concatenate(ranks, axis=0).astype(i32)
    pad = jnp.zeros((LANES - TOP_K, rows), f32)
    pw_ref[...] = jnp.transpose(jnp.concatenate(probs + [pad], axis=0))
    cnt_ref[...] = carry_s[:, :LANES].astype(i32)


def _group_specs(rows, width, na, nb):
    first = na // rows
    return (pl.BlockSpec((rows, width), lambda i: (jnp.minimum(i, first - 1), 0)),
            pl.BlockSpec((rows, width), lambda i: (jnp.maximum(i - first, 0), 0)))


def _router(h1a, h1b, g, wr, br, tri):
    na, nb = h1a.shape[0], h1b.shape[0]
    t = na + nb
    rows = ROUTE_ROWS
    assert na % rows == 0 and nb % rows == 0
    return pl.pallas_call(
        functools.partial(_router_kernel, first_steps=na // rows),
        grid=(t // rows,),
        in_specs=[*_group_specs(rows, D_MODEL, na, nb),
                  _full(g.shape), _full(wr.shape), _full(br.shape), _full(tri.shape)],
        out_specs=[pl.BlockSpec((rows, HALF), lambda i: (i, 0)),
                   pl.BlockSpec((TOP_K, rows), lambda i: (0, i)),
                   pl.BlockSpec((TOP_K, rows), lambda i: (0, i)),
                   pl.BlockSpec((rows, LANES), lambda i: (i, 0)),
                   _full((N_EXPERTS, LANES))],
        out_shape=[jax.ShapeDtypeStruct((t, HALF), u32),
                   jax.ShapeDtypeStruct((TOP_K, t), i32),
                   jax.ShapeDtypeStruct((TOP_K, t), i32),
                   jax.ShapeDtypeStruct((t, LANES), f32),
                   jax.ShapeDtypeStruct((N_EXPERTS, LANES), i32)],
        scratch_shapes=[pltpu.VMEM((N_EXPERTS, rows), f32)],
        compiler_params=pltpu.CompilerParams(dimension_semantics=("arbitrary",),
                                             vmem_limit_bytes=VMEM_LIMIT),
        name="router",
    )(h1a, h1b, g, wr, br, tri)


def _sc_mesh():
    return plsc.VectorSubcoreMesh(core_axis_name="core", subcore_axis_name="subcore")


def _sc_params():
    return dataclasses.replace(pltpu.CompilerParams(), needs_layout_passes=False)


def _sc_worker():
    return lax.axis_index("subcore") * SC_CORES + lax.axis_index("core")


def _sc_positions(idx_v, rank_v, off_v, n):
    @pl.loop(0, n // SC_LANES)
    def _(i):
        sl = pl.ds(pl.multiple_of(i * SC_LANES, SC_LANES), SC_LANES)
        idx_v[sl] = plsc.load_gather(off_v, [idx_v[sl]]) + rank_v[sl]


def _sc_copy_rows(src_hbm, idx_v, dst_hbm, dst_row0, n, rows_v, gsem, wsem):
    n_ch = n // SC_CHUNK
    n_pair = n_ch // 2

    def gather(c, b):
        o = pl.multiple_of(c * SC_CHUNK, SC_CHUNK)
        return pltpu.make_async_copy(src_hbm.at[idx_v.at[pl.ds(o, SC_CHUNK)]], rows_v.at[b], gsem.at[b])

    def write(c, b):
        o = pl.multiple_of(c * SC_CHUNK, SC_CHUNK)
        return pltpu.make_async_copy(rows_v.at[b], dst_hbm.at[pl.ds(dst_row0 + o, SC_CHUNK)], wsem.at[b])

    gather(0, 0).start()

    @pl.loop(0, n_pair)
    def _(i):
        c0 = 2 * i
        gather(c0, 0).wait()

        @pl.when(i > 0)
        def _():
            write(c0 - 1, 1).wait()

        gather(c0 + 1, 1).start()
        write(c0, 0).start()
        gather(c0 + 1, 1).wait()
        write(c0, 0).wait()

        @pl.when(c0 + 2 < n_ch)
        def _():
            gather(c0 + 2, 0).start()

        write(c0 + 1, 1).start()

    if n_ch % 2:
        gather(n_ch - 1, 0).wait()
        write(n_ch - 1, 0).start()
        write(n_ch - 1, 0).wait()
    if n_pair:
        write(2 * n_pair - 1, 1).wait()


def _sc_dispatch(vpk, idx_flat, rank_flat, off, n_out):
    n_tok = vpk.shape[0]
    per_w = n_out // SC_WORKERS
    assert n_out % (SC_WORKERS * SC_CHUNK) == 0 and n_tok % SC_LANES == 0

    @pl.kernel(out_type=jax.ShapeDtypeStruct((n_out, HALF), u32), mesh=_sc_mesh(),
               compiler_params=_sc_params(), name="dispatch",
               scratch_types=[pltpu.VMEM((per_w,), i32), pltpu.VMEM((n_tok,), i32), pltpu.VMEM((n_tok,), i32),
                              pltpu.VMEM((N_EXPERTS,), i32), pltpu.VMEM((2, SC_CHUNK, HALF), u32),
                              pltpu.SemaphoreType.DMA((2,)), pltpu.SemaphoreType.DMA((2,))])
    def k(v_hbm, i_hbm, r_hbm, off_hbm, o_hbm, src_v, idx_v, rank_v, off_v, rows_v, gsem, wsem):
        lo = _sc_worker() * per_w
        pltpu.sync_copy(off_hbm, off_v)

        @pl.loop(0, per_w // SC_LANES)
        def _(i):
            o = pl.multiple_of(i * SC_LANES, SC_LANES)
            src_v[pl.ds(o, SC_LANES)] = lax.iota(i32, SC_LANES) + lax.rem(lo + o, n_tok)

        @pl.loop(0, TOP_K)
        def _(kk):
            pltpu.sync_copy(i_hbm.at[pl.ds(kk * n_tok, n_tok)], idx_v)
            pltpu.sync_copy(r_hbm.at[pl.ds(kk * n_tok, n_tok)], rank_v)

            @plsc.parallel_loop(0, n_tok, step=SC_LANES, unroll=8)
            def _(o):
                sl = pl.ds(pl.multiple_of(o, SC_LANES), SC_LANES)
                p = plsc.load_gather(off_v, [idx_v[sl]]) + rank_v[sl] - lo
                mine = (p >= 0) & (p < per_w)
                tok = lax.iota(i32, SC_LANES) + o
                plsc.store_scatter(src_v, [jnp.where(mine, p, 0)], tok, mask=mine)

        _sc_copy_rows(v_hbm, src_v, o_hbm, lo, per_w, rows_v, gsem, wsem)

    return k(vpk, idx_flat, rank_flat, off)


def _sc_collect(ys, idx_flat, rank_flat, off):
    n = idx_flat.shape[0]
    per_w = n // SC_WORKERS
    assert n % (SC_WORKERS * SC_CHUNK) == 0

    @pl.kernel(out_type=jax.ShapeDtypeStruct((n, HALF), u32), mesh=_sc_mesh(),
               compiler_params=_sc_params(), name="collect",
               scratch_types=[pltpu.VMEM((per_w,), i32), pltpu.VMEM((per_w,), i32),
                              pltpu.VMEM((N_EXPERTS,), i32), pltpu.VMEM((2, SC_CHUNK, HALF), u32),
                              pltpu.SemaphoreType.DMA((2,)), pltpu.SemaphoreType.DMA((2,))])
    def k(y_hbm, i_hbm, r_hbm, off_hbm, o_hbm, idx_v, rank_v, off_v, rows_v, gsem, wsem):
        lo = _sc_worker() * per_w
        pltpu.sync_copy(off_hbm, off_v)
        pltpu.sync_copy(i_hbm.at[pl.ds(lo, per_w)], idx_v)
        pltpu.sync_copy(r_hbm.at[pl.ds(lo, per_w)], rank_v)
        _sc_positions(idx_v, rank_v, off_v, per_w)
        _sc_copy_rows(y_hbm, idx_v, o_hbm, lo, per_w, rows_v, gsem, wsem)

    return k(ys, idx_flat, rank_flat, off)


def _ffn_kernel(te_ref, tf_ref, slot_ref, next_ref, rows_ref, nv_ref, xs_ref, wg_hbm, bg_ref, wu_hbm, bu_ref, wd_hbm, bd_ref,
                ys_ref, wbuf, wg_s, wu_s, wd_s, wsem):
    step = pl.program_id(0)
    w_hbm = (wg_hbm, wu_hbm, wd_hbm)
    w_bf16 = (wg_s, wu_s, wd_s)

    def fetch(e, slot):
        return [pltpu.make_async_copy(w_hbm[m].at[e], wbuf.at[slot, m], wsem.at[slot, m]) for m in range(3)]

    @pl.when(step < nv_ref[0])
    def _():
        @pl.when(tf_ref[step] == 1)
        def _():
            slot = slot_ref[step]

            @pl.when(step == 0)
            def _():
                for cp in fetch(te_ref[0], slot):
                    cp.start()

            for cp in fetch(te_ref[step], slot):
                cp.wait()
            for s in range(2):
                @pl.when(slot == s)
                def _():
                    for m in range(3):
                        w_bf16[m][...] = wbuf[s, m].astype(bf16)

            @pl.when(next_ref[step] >= 0)
            def _():
                for cp in fetch(next_ref[step], 1 - slot):
                    cp.start()

        def rows_block(rb):
            rs = slice(rb * FFN_SUB, (rb + 1) * FFN_SUB)
            x = _unpack_rows(xs_ref[rs, :]).astype(bf16)
            g = jnp.minimum(_dot(x, wg_s[...]) + bg_ref[0], SWIGLU_LIMIT)
            u = jnp.clip(_dot(x, wu_s[...]) + bu_ref[0], -SWIGLU_LIMIT, SWIGLU_LIMIT)
            hid = (u + 1.0) * (g * jax.nn.sigmoid(SWIGLU_ALPHA * g))
            y = _dot(hid.astype(bf16), wd_s[...]) + bd_ref[0]
            ys_ref[rs, :] = _pack_rows(y)

        @pl.when(rows_ref[step] > FFN_SUB)
        def _():
            rows_block(0)
            rows_block(1)

        @pl.when(rows_ref[step] <= FFN_SUB)
        def _():
            rows_block(0)
            ys_ref[FFN_SUB:, :] = jnp.zeros((FFN_ROWS - FFN_SUB, HALF), u32)

    @pl.when(step >= nv_ref[0])
    def _():
        ys_ref[...] = jnp.zeros_like(ys_ref)


def _ffn(tables, xs, w_gate, b_gate, w_up, b_up, w_down, b_down):
    n_rows = xs.shape[0]
    n_tiles = n_rows // FFN_ROWS
    assert D_FF == D_MODEL
    assert FFN_ROWS == 2 * FFN_SUB

    def row_map(i, te, tf, sl, nx, rw, nv):
        return (jnp.minimum(i, nv[0] - 1), 0)

    def b_map(i, te, tf, sl, nx, rw, nv):
        return (te[i], 0, 0)

    w_spec = pl.BlockSpec(memory_space=pl.ANY)
    b_spec = pl.BlockSpec((1, 1, D_FF), b_map)
    return pl.pallas_call(
        _ffn_kernel,
        grid_spec=pltpu.PrefetchScalarGridSpec(
            num_scalar_prefetch=6,
            grid=(n_tiles,),
            in_specs=[pl.BlockSpec((FFN_ROWS, HALF), row_map),
                      w_spec, b_spec, w_spec, b_spec, w_spec, b_spec],
            out_specs=pl.BlockSpec((FFN_ROWS, HALF), lambda i, *_: (i, 0)),
            scratch_shapes=[pltpu.VMEM((2, 3, D_MODEL, D_FF), f32),
                            pltpu.VMEM((D_MODEL, D_FF), bf16),
                            pltpu.VMEM((D_MODEL, D_FF), bf16),
                            pltpu.VMEM((D_FF, D_MODEL), bf16),
                            pltpu.SemaphoreType.DMA((2, 3))]),
        out_shape=jax.ShapeDtypeStruct((n_rows, HALF), u32),
        compiler_params=pltpu.CompilerParams(dimension_semantics=("arbitrary",),
                                             vmem_limit_bytes=VMEM_LIMIT),
        name="experts",
    )(*tables, xs, w_gate, b_gate.reshape(N_EXPERTS, 1, D_FF),
      w_up, b_up.reshape(N_EXPERTS, 1, D_FF), w_down, b_down.reshape(N_EXPERTS, 1, D_MODEL))


def _finish_kernel(g_ref, h1a_ref, h1b_ref, pw_ref, pa_ref, pb_ref, g_post_ref, w_ple_ref, w_pg_ref, b_pg_ref,
                   g_ple_ref, outa_ref, outb_ref, *, first_steps):
    step = pl.program_id(0)
    rows = h1a_ref.shape[0]
    pw = pw_ref[...]
    f = jnp.zeros((rows, D_MODEL), f32)
    for k in range(TOP_K):
        f = f + pw[:, k:k + 1] * _unpack_rows(g_ref[k])
    h2 = _two_groups(step, first_steps, h1a_ref, h1b_ref) + _rms(f, g_post_ref[...])
    gate = jax.nn.sigmoid(_dot(h2.astype(bf16), w_pg_ref[...]) + b_pg_ref[...])
    pin = _two_groups(step, first_steps, pa_ref, pb_ref).astype(bf16)
    pe = _dot(pin, w_ple_ref[...]) * gate
    out = h2 + _rms(pe, g_ple_ref[...])

    @pl.when(step < first_steps)
    def _():
        outa_ref[...] = out

    @pl.when(step >= first_steps)
    def _():
        outb_ref[...] = out


def _finish(g, h1a, h1b, pw, pa, pb, g_post, w_ple, w_pg, b_pg, g_ple):
    na, nb = h1a.shape[0], h1b.shape[0]
    t = na + nb
    rows = FIN_ROWS
    assert na % rows == 0 and nb % rows == 0
    return pl.pallas_call(
        functools.partial(_finish_kernel, first_steps=na // rows),
        grid=(t // rows,),
        in_specs=[pl.BlockSpec((TOP_K, rows, HALF), lambda i: (0, i, 0)),
                  *_group_specs(rows, D_MODEL, na, nb),
                  pl.BlockSpec((rows, LANES), lambda i: (i, 0)),
                  *_group_specs(rows, PLE_DIM, na, nb),
                  _full(g_post.shape), _full(w_ple.shape), _full(w_pg.shape),
                  _full(b_pg.shape), _full(g_ple.shape)],
        out_specs=list(_group_specs(rows, D_MODEL, na, nb)),
        out_shape=[jax.ShapeDtypeStruct((na, D_MODEL), f32), jax.ShapeDtypeStruct((nb, D_MODEL), f32)],
        compiler_params=pltpu.CompilerParams(dimension_semantics=("arbitrary",),
                                             vmem_limit_bytes=VMEM_LIMIT),
        name="finish",
    )(g, h1a, h1b, pw, pa, pb, g_post, w_ple, w_pg, b_pg, g_ple)


def _block_diag_pairs(w):
    per = MXU_DIM // HEAD_A
    w4 = w.reshape(N_HEADS_A // per, per, HEAD_A, HEAD_A)
    blocks = jnp.where(jnp.eye(per, dtype=bool)[None, :, None, :, None], w4[:, :, :, None, :], 0.0)
    return blocks.reshape(N_HEADS_A // per, MXU_DIM, MXU_DIM)


def _time_major(x):
    x = jnp.swapaxes(x, 0, 1)
    return x.reshape((x.shape[0] * x.shape[1],) + x.shape[2:])


def _batch_major(x, b):
    return jnp.swapaxes(x.reshape(x.shape[0] // b, b, x.shape[1]), 0, 1)


def _routing_tables(counts, n_tiles):
    tiles = (counts + FFN_ROWS - 1) // FFN_ROWS
    tile_ends = jnp.cumsum(tiles)
    off = (tile_ends - tiles) * FFN_ROWS
    n_valid = tile_ends[-1:]
    tile_ids = jnp.minimum(jnp.arange(n_tiles, dtype=i32), n_valid - 1)
    tile_e = jnp.sum((tile_ids[:, None] >= tile_ends[None, :]).astype(i32), axis=1)
    tile_first = jnp.concatenate([jnp.ones((1,), i32), (tile_e[1:] != tile_e[:-1]).astype(i32)])
    tile_slot = (jnp.cumsum(tile_first) - 1) % 2
    cand = jnp.where(tiles > 0, jnp.arange(N_EXPERTS, dtype=i32), N_EXPERTS)
    suffix_min = lax.cummin(cand[::-1])[::-1]
    nxt = jnp.concatenate([suffix_min[1:], jnp.full((1,), N_EXPERTS, i32)])
    tile_next = jnp.where(nxt < N_EXPERTS, nxt, -1)[tile_e]
    tile_rows = jnp.clip(counts[tile_e] - (tile_ids - (tile_ends - tiles)[tile_e]) * FFN_ROWS, 0, FFN_ROWS)
    tables = (tile_e.astype(i32), tile_first, tile_slot.astype(i32), tile_next.astype(i32),
              tile_rows.astype(i32), n_valid.astype(i32))
    return off.astype(i32), tables


def _layer(xp, xs_tm, pp, ps_tm, state_h, state_conv, state_pool, lw):
    (norm_mix_pre, w_in, conv_w, conv_b, w_rgate, b_rgate, w_igate, b_igate, lru_lambda,
     pool_w, pool_b, pool_scale, norm_group_a, norm_group_b, w_out, norm_mix_post,
     norm_ffn_pre, w_router, b_router, w_gate, b_gate, w_up, b_up, w_down, b_down, norm_ffn_post,
     w_ple, w_ple_gate, b_ple_gate, norm_ple) = lw
    b_p, s_p = xp.shape[0], xp.shape[1]
    n_p, n_s = b_p * s_p, xs_tm.shape[0]
    b_s = state_h.shape[0]
    row = lambda a: a.reshape(1, -1)

    w_gates = jnp.concatenate([_block_diag_pairs(w_rgate), _block_diag_pairs(w_igate)], axis=-1).astype(bf16)
    mix_w = (row(norm_mix_pre), w_in.astype(bf16), conv_w, row(conv_b),
             w_gates, row(b_rgate), row(b_igate), row(lru_lambda),
             pool_w.astype(bf16), row(pool_b), row(pool_scale), row(norm_group_a), row(norm_group_b),
             w_out.astype(bf16), row(norm_mix_post))

    zeros = lambda *s: jnp.zeros(s, f32)
    h1_p, hfin_p, cfin_p, pfin_p = _mixer(
        xp, zeros((CONV_W - 1) * b_p, W_A), zeros(POOL_BUF * b_p, W_B), zeros(b_p, W_A),
        mix_w, bt=b_p, start=0)
    h1_s, hfin_s, cfin_s, pfin_s = _mixer(
        xs_tm, _time_major(state_conv), _time_major(state_pool), state_h,
        mix_w, bt=b_s, start=PAST_LEN)
    h1_p = h1_p.reshape(n_p, D_MODEL)
    t = n_p + n_s

    w_router_pad = jnp.zeros((D_MODEL, LANES), f32).at[:, :N_EXPERTS].set(w_router)
    b_router_pad = jnp.zeros((1, LANES), f32).at[0, :N_EXPERTS].set(b_router)
    tri = jnp.triu(jnp.ones((ROUTE_ROWS, ROUTE_ROWS), bf16), k=1)
    vpk, idx_t, rank_t, pw, cnt = _router(h1_p, h1_s, row(norm_ffn_pre), w_router_pad.astype(bf16),
                                          b_router_pad, tri)

    n_tiles = (t * TOP_K) // FFN_ROWS + N_EXPERTS
    off, tables = _routing_tables(cnt[:, 0], n_tiles)
    idx_flat, rank_flat = idx_t.reshape(-1), rank_t.reshape(-1)

    xs = _sc_dispatch(vpk, idx_flat, rank_flat, off, n_tiles * FFN_ROWS)
    ys = _ffn(tables, xs, w_gate, b_gate, w_up, b_up, w_down, b_down)
    g = _sc_collect(ys, idx_flat, rank_flat, off).reshape(TOP_K, t, HALF)
    out_p, out_s = _finish(g, h1_p, h1_s, pw, pp.reshape(n_p, PLE_DIM), ps_tm, row(norm_ffn_post),
                           w_ple.astype(bf16), w_ple_gate.astype(bf16), row(b_ple_gate), row(norm_ple))

    states = (hfin_p, _batch_major(cfin_p, b_p), _batch_major(pfin_p, b_p),
              hfin_s, _batch_major(cfin_s, b_s), _batch_major(pfin_s, b_s))
    return out_p.reshape(b_p, s_p, D_MODEL), out_s, states


def kernel(x_prompt, x_sample, state_rglru_h, state_rglru_conv, state_pool, p_prompt, p_sample, norm_mix_pre, w_in, conv_w, conv_b, w_rgate, b_rgate, w_igate, b_igate, lru_lambda, pool_w, pool_b, pool_scale, norm_group_a, norm_group_b, w_out, norm_mix_post, norm_ffn_pre, w_router, b_router, w_gate, b_gate, w_up, b_up, w_down, b_down, norm_ffn_post, w_ple, w_ple_gate, b_ple_gate, norm_ple):
    depth = w_in.shape[0]
    b_p, b_s = x_prompt.shape[0], x_sample.shape[0]
    per_layer = (norm_mix_pre, w_in, conv_w, conv_b, w_rgate, b_rgate, w_igate, b_igate, lru_lambda,
                 pool_w, pool_b, pool_scale, norm_group_a, norm_group_b, w_out, norm_mix_post,
                 norm_ffn_pre, w_router, b_router, w_gate, b_gate, w_up, b_up, w_down, b_down,
                 norm_ffn_post, w_ple, w_ple_gate, b_ple_gate, norm_ple)
    hp, hs = x_prompt, _time_major(x_sample)
    collected = []
    for i in range(depth):
        hp, hs, states = _layer(hp, hs, p_prompt[i], _time_major(p_sample[i]),
                                state_rglru_h[i], state_rglru_conv[i], state_pool[i],
                                tuple(w[i] for w in per_layer))
        collected.append(states)
    stacked = tuple(jnp.stack([c[j] for c in collected]) for j in range(6))
    return (hp, _batch_major(hs, b_s)) + stacked
```

```python
import dataclasses
import functools

import jax
import jax.numpy as jnp
from jax import lax
from jax.experimental import pallas as pl
from jax.experimental.pallas import tpu as pltpu
from jax.experimental.pallas import tpu_sc as plsc

D_MODEL = 1024
W_A = 512
W_B = 512
N_HEADS_A = 8
HEAD_A = W_A // N_HEADS_A
CONV_W = 4
LRU_C = 8.0
POOL_WINDOWS = (2, 4, 8, 16)
GROUP_B = W_B // len(POOL_WINDOWS)
POOL_BUF = max(POOL_WINDOWS) - 1
N_EXPERTS = 32
TOP_K = 4
D_FF = 1024
SWIGLU_LIMIT = 7.0
SWIGLU_ALPHA = 1.702
PLE_DIM = 256
EPS = 1e-6
PAST_LEN = 16384

LANES = 128
MXU_DIM = 256
HALF = D_MODEL // 2
SC_CORES = 2
SC_SUBCORES = 16
SC_WORKERS = SC_CORES * SC_SUBCORES
SC_LANES = 16
SC_CHUNK = 64

MIX_ROWS = 512
ROUTE_ROWS = 512
FFN_ROWS = 512
FFN_SUB = 256
FIN_ROWS = 512
VMEM_LIMIT = 56 * 1024 * 1024

f32 = jnp.float32
bf16 = jnp.bfloat16
u32 = jnp.uint32
i32 = jnp.int32


def _rms(x, g):
    return x * lax.rsqrt(jnp.mean(x * x, axis=-1, keepdims=True) + EPS) * g


def _dot(a, b):
    return jnp.dot(a, b, preferred_element_type=f32)


def _pack_rows(x):
    bits = lax.bitcast_convert_type(x.astype(bf16).astype(f32), u32)
    return (bits[:, HALF:] & jnp.uint32(0xFFFF0000)) | (bits[:, :HALF] >> 16)


def _unpack_rows(p):
    lo = lax.bitcast_convert_type(p << 16, f32)
    hi = lax.bitcast_convert_type(p & jnp.uint32(0xFFFF0000), f32)
    return jnp.concatenate([lo, hi], axis=-1)


def _mixer_kernel(x_ref, conv0_ref, pool0_ref, h0_ref, g_pre_ref, w_in_ref, conv_w_ref, conv_b_ref,
                  w_gates_ref, br_ref, bi_ref, lam_ref,
                  pool_w_ref, pool_b_ref, pool_scale_ref, ng_a_ref, ng_b_ref, w_out_ref, g_post_ref,
                  h1_ref, hfin_ref, convfin_ref, poolfin_ref,
                  h_s, conv_s, pool_s, a_s, b_s, hs_s, *, bt, tt, start):
    step = pl.program_id(0)
    rows = bt * tt

    @pl.when(step == 0)
    def _():
        h_s[...] = h0_ref[...]
        conv_s[...] = conv0_ref[...]
        pool_s[...] = pool0_ref[...]

    if x_ref.ndim == 3:
        x = pltpu.einshape("btd->tbd", x_ref[...]).reshape(rows, D_MODEL)
    else:
        x = x_ref[...]
    u = _rms(x, g_pre_ref[...]).astype(bf16)
    z = _dot(u, w_in_ref[...])
    xa, ga, xb = z[:, :W_A], z[:, W_A:2 * W_A], z[:, 2 * W_A:]

    ext_a = jnp.concatenate([conv_s[...], xa], axis=0)
    xc = conv_b_ref[...]
    for k in range(CONV_W):
        xc = xc + ext_a[k * bt:k * bt + rows] * conv_w_ref[k:k + 1, :]
    conv_s[...] = ext_a[rows:]

    xc16 = xc.astype(bf16)
    gates = [_dot(xc16[:, j * MXU_DIM:(j + 1) * MXU_DIM], w_gates_ref[j]) for j in range(W_A // MXU_DIM)]
    r = jax.nn.sigmoid(jnp.concatenate([gj[:, :MXU_DIM] for gj in gates], axis=-1) + br_ref[...])
    ig = jax.nn.sigmoid(jnp.concatenate([gj[:, MXU_DIM:] for gj in gates], axis=-1) + bi_ref[...])
    lam = lam_ref[...]
    softplus_neg = jnp.maximum(-lam, 0.0) + jnp.log1p(jnp.exp(-jnp.abs(lam)))
    log_a = (-LRU_C) * r * softplus_neg
    a_s[...] = jnp.exp(log_a)
    th = jnp.tanh(log_a)
    b_s[...] = jnp.sqrt(-2.0 * th / (1.0 - th)) * (ig * xc)

    def scan_step(t, h):
        sl = pl.ds(pl.multiple_of(t * bt, bt), bt)
        h = a_s[sl, :] * h + b_s[sl, :]
        hs_s[sl, :] = h
        return h

    h_last = lax.fori_loop(0, tt, scan_step, h_s[...], unroll=True)
    h_s[...] = h_last
    ya = hs_s[...] * jax.nn.gelu(ga)

    ext_b = jnp.concatenate([pool_s[...], xb], axis=0)
    pool_s[...] = ext_b[rows:]
    s2 = ext_b[bt:, :] + ext_b[:-bt, :]
    s4 = s2[2 * bt:, GROUP_B:] + s2[:-2 * bt, GROUP_B:]
    s8 = s4[4 * bt:, GROUP_B:] + s4[:-4 * bt, GROUP_B:]
    s16 = s8[8 * bt:, GROUP_B:] + s8[:-8 * bt, GROUP_B:]
    wins = (s2[14 * bt:, :GROUP_B], s4[12 * bt:, :GROUP_B], s8[8 * bt:, :GROUP_B], s16)
    t_idx = lax.broadcasted_iota(i32, (rows, GROUP_B), 0) // bt
    pos1 = (t_idx + (step * tt + start + 1)).astype(f32)
    yb_parts = []
    for g, w in enumerate(POOL_WINDOWS):
        cnt = jnp.minimum(jnp.float32(w), pos1)
        d = wins[g] / cnt - xb[:, g * GROUP_B:(g + 1) * GROUP_B]
        yb_parts.append(_dot(d.astype(bf16), pool_w_ref[g]))
    yb = (jnp.concatenate(yb_parts, axis=-1) + pool_b_ref[...]) * pool_scale_ref[...]

    na = _rms(ya, ng_a_ref[...])
    nb = _rms(yb, ng_b_ref[...])
    m = _dot(jnp.concatenate([na, nb], axis=-1).astype(bf16), w_out_ref[...])
    h1 = x + _rms(m, g_post_ref[...])
    if h1_ref.ndim == 3:
        h1_ref[...] = pltpu.einshape("tbd->btd", h1.reshape(tt, bt, D_MODEL))
    else:
        h1_ref[...] = h1

    hfin_ref[...] = h_last
    convfin_ref[...] = conv_s[...]
    poolfin_ref[...] = pool_s[...]


def _full(shape):
    return pl.BlockSpec(shape, lambda i, *_: (0,) * len(shape))


def _mixer(x, conv0, pool0, h0, wts, *, bt, start):
    tt = MIX_ROWS // bt
    rows = bt * tt
    if x.ndim == 3:
        assert x.shape[0] == bt and x.shape[1] % tt == 0
        n_steps = x.shape[1] // tt
        x_spec = pl.BlockSpec((bt, tt, D_MODEL), lambda i: (0, i, 0))
    else:
        assert x.shape[0] % rows == 0
        n_steps = x.shape[0] // rows
        x_spec = pl.BlockSpec((rows, D_MODEL), lambda i: (i, 0))
    kern = functools.partial(_mixer_kernel, bt=bt, tt=tt, start=start)
    small = [conv0, pool0, h0] + list(wts)
    return pl.pallas_call(
        kern,
        grid=(n_steps,),
        in_specs=[x_spec] + [_full(a.shape) for a in small],
        out_specs=[x_spec,
                   _full((bt, W_A)), _full(((CONV_W - 1) * bt, W_A)), _full((POOL_BUF * bt, W_B))],
        out_shape=[jax.ShapeDtypeStruct(x.shape, f32),
                   jax.ShapeDtypeStruct((bt, W_A), f32),
                   jax.ShapeDtypeStruct(((CONV_W - 1) * bt, W_A), f32),
                   jax.ShapeDtypeStruct((POOL_BUF * bt, W_B), f32)],
        scratch_shapes=[pltpu.VMEM((bt, W_A), f32),
                        pltpu.VMEM(((CONV_W - 1) * bt, W_A), f32),
                        pltpu.VMEM((POOL_BUF * bt, W_B), f32),
                        pltpu.VMEM((rows, W_A), f32),
                        pltpu.VMEM((rows, W_A), f32),
                        pltpu.VMEM((rows, W_A), f32)],
        compiler_params=pltpu.CompilerParams(dimension_semantics=("arbitrary",),
                                             vmem_limit_bytes=VMEM_LIMIT),
        name="mixer",
    )(x, *small)


def _two_groups(step, first_steps, a_ref, b_ref):
    return jnp.where(step < first_steps, a_ref[...], b_ref[...])


def _router_kernel(h1a_ref, h1b_ref, g_ref, wr_ref, br_ref, tri_ref,
                   vpk_ref, idx_ref, rank_ref, pw_ref, cnt_ref, carry_s, *, first_steps):
    step = pl.program_id(0)
    rows = h1a_ref.shape[0]

    @pl.when(step == 0)
    def _():
        carry_s[...] = jnp.zeros_like(carry_s)

    v = _rms(_two_groups(step, first_steps, h1a_ref, h1b_ref), g_ref[...])
    vpk_ref[...] = _pack_rows(v)
    logits = _dot(v.astype(bf16), wr_ref[...]) + br_ref[...]
    lt = jnp.transpose(logits)[:N_EXPERTS, :]

    eio = lax.broadcasted_iota(i32, (N_EXPERTS, rows), 0).astype(f32)
    work = lt
    vals, idxs, sels = [], [], []
    for _ in range(TOP_K):
        m = jnp.max(work, axis=0, keepdims=True)
        ik = jnp.min(jnp.where(work == m, eio, float(N_EXPERTS)), axis=0, keepdims=True)
        sel = eio == ik
        vals.append(m)
        idxs.append(ik)
        sels.append(sel)
        work = jnp.where(sel, -jnp.inf, work)
    exps = [jnp.exp(val - vals[0]) for val in vals]
    denom = exps[0] + exps[1] + exps[2] + exps[3]
    probs = [e / denom for e in exps]

    chosen = sels[0] | sels[1] | sels[2] | sels[3]
    onehot = jnp.where(chosen, 1.0, 0.0)
    before = _dot(onehot.astype(bf16), tri_ref[...])
    base = carry_s[...] + before
    ranks = [jnp.sum(jnp.where(sel, base, 0.0), axis=0, keepdims=True) for sel in sels]
    carry_s[...] = carry_s[...] + jnp.sum(onehot, axis=1, keepdims=True)

    idx_ref[...] = jnp.concatenate(idxs, axis=0).astype(i32)
    rank_ref[...] = jnp.concatenate(ranks, axis=0).astype(i32)
    pad = jnp.zeros((LANES - TOP_K, rows), f32)
    pw_ref[...] = jnp.transpose(jnp.concatenate(probs + [pad], axis=0))
    cnt_ref[...] = carry_s[:, :LANES].astype(i32)


def _group_specs(rows, width, na, nb):
    first = na // rows
    return (pl.BlockSpec((rows, width), lambda i: (jnp.minimum(i, first - 1), 0)),
            pl.BlockSpec((rows, width), lambda i: (jnp.maximum(i - first, 0), 0)))


def _router(h1a, h1b, g, wr, br, tri):
    na, nb = h1a.shape[0], h1b.shape[0]
    t = na + nb
    rows = ROUTE_ROWS
    assert na % rows == 0 and nb % rows == 0
    return pl.pallas_call(
        functools.partial(_router_kernel, first_steps=na // rows),
        grid=(t // rows,),
        in_specs=[*_group_specs(rows, D_MODEL, na, nb),
                  _full(g.shape), _full(wr.shape), _full(br.shape), _full(tri.shape)],
        out_specs=[pl.BlockSpec((rows, HALF), lambda i: (i, 0)),
                   pl.BlockSpec((TOP_K, rows), lambda i: (0, i)),
                   pl.BlockSpec((TOP_K, rows), lambda i: (0, i)),
                   pl.BlockSpec((rows, LANES), lambda i: (i, 0)),
                   _full((N_EXPERTS, LANES))],
        out_shape=[jax.ShapeDtypeStruct((t, HALF), u32),
                   jax.ShapeDtypeStruct((TOP_K, t), i32),
                   jax.ShapeDtypeStruct((TOP_K, t), i32),
                   jax.ShapeDtypeStruct((t, LANES), f32),
                   jax.ShapeDtypeStruct((N_EXPERTS, LANES), i32)],
        scratch_shapes=[pltpu.VMEM((N_EXPERTS, rows), f32)],
        compiler_params=pltpu.CompilerParams(dimension_semantics=("arbitrary",),
                                             vmem_limit_bytes=VMEM_LIMIT),
        name="router",
    )(h1a, h1b, g, wr, br, tri)


def _sc_mesh():
    return plsc.VectorSubcoreMesh(core_axis_name="core", subcore_axis_name="subcore")


def _sc_params():
    return dataclasses.replace(pltpu.CompilerParams(), needs_layout_passes=False)


def _sc_worker():
    return lax.axis_index("subcore") * SC_CORES + lax.axis_index("core")


def _sc_positions(idx_v, rank_v, off_v, n):
    @pl.loop(0, n // SC_LANES)
    def _(i):
        sl = pl.ds(pl.multiple_of(i * SC_LANES, SC_LANES), SC_LANES)
        idx_v[sl] = plsc.load_gather(off_v, [idx_v[sl]]) + rank_v[sl]


def _sc_copy_rows(src_hbm, idx_v, dst_hbm, dst_row0, n, rows_v, gsem, wsem):
    n_ch = n // SC_CHUNK
    n_pair = n_ch // 2

    def gather(c, b):
        o = pl.multiple_of(c * SC_CHUNK, SC_CHUNK)
        return pltpu.make_async_copy(src_hbm.at[idx_v.at[pl.ds(o, SC_CHUNK)]], rows_v.at[b], gsem.at[b])

    def write(c, b):
        o = pl.multiple_of(c * SC_CHUNK, SC_CHUNK)
        return pltpu.make_async_copy(rows_v.at[b], dst_hbm.at[pl.ds(dst_row0 + o, SC_CHUNK)], wsem.at[b])

    gather(0, 0).start()

    @pl.loop(0, n_pair)
    def _(i):
        c0 = 2 * i
        gather(c0, 0).wait()

        @pl.when(i > 0)
        def _():
            write(c0 - 1, 1).wait()

        gather(c0 + 1, 1).start()
        write(c0, 0).start()
        gather(c0 + 1, 1).wait()
        write(c0, 0).wait()

        @pl.when(c0 + 2 < n_ch)
        def _():
            gather(c0 + 2, 0).start()

        write(c0 + 1, 1).start()

    if n_ch % 2:
        gather(n_ch - 1, 0).wait()
        write(n_ch - 1, 0).start()
        write(n_ch - 1, 0).wait()
    if n_pair:
        write(2 * n_pair - 1, 1).wait()


def _sc_dispatch(vpk, idx_flat, rank_flat, off, n_out):
    n_tok = vpk.shape[0]
    per_w = n_out // SC_WORKERS
    assert n_out % (SC_WORKERS * SC_CHUNK) == 0 and n_tok % SC_LANES == 0

    @pl.kernel(out_type=jax.ShapeDtypeStruct((n_out, HALF), u32), mesh=_sc_mesh(),
               compiler_params=_sc_params(), name="dispatch",
               scratch_types=[pltpu.VMEM((per_w,), i32), pltpu.VMEM((n_tok,), i32), pltpu.VMEM((n_tok,), i32),
                              pltpu.VMEM((N_EXPERTS,), i32), pltpu.VMEM((2, SC_CHUNK, HALF), u32),
                              pltpu.SemaphoreType.DMA((2,)), pltpu.SemaphoreType.DMA((2,))])
    def k(v_hbm, i_hbm, r_hbm, off_hbm, o_hbm, src_v, idx_v, rank_v, off_v, rows_v, gsem, wsem):
        lo = _sc_worker() * per_w
        pltpu.sync_copy(off_hbm, off_v)

        @pl.loop(0, per_w // SC_LANES)
        def _(i):
            o = pl.multiple_of(i * SC_LANES, SC_LANES)
            src_v[pl.ds(o, SC_LANES)] = lax.iota(i32, SC_LANES) + lax.rem(lo + o, n_tok)

        @pl.loop(0, TOP_K)
        def _(kk):
            pltpu.sync_copy(i_hbm.at[pl.ds(kk * n_tok, n_tok)], idx_v)
            pltpu.sync_copy(r_hbm.at[pl.ds(kk * n_tok, n_tok)], rank_v)

            @plsc.parallel_loop(0, n_tok, step=SC_LANES, unroll=8)
            def _(o):
                sl = pl.ds(pl.multiple_of(o, SC_LANES), SC_LANES)
                p = plsc.load_gather(off_v, [idx_v[sl]]) + rank_v[sl] - lo
                mine = (p >= 0) & (p < per_w)
                tok = lax.iota(i32, SC_LANES) + o
                plsc.store_scatter(src_v, [jnp.where(mine, p, 0)], tok, mask=mine)

        _sc_copy_rows(v_hbm, src_v, o_hbm, lo, per_w, rows_v, gsem, wsem)

    return k(vpk, idx_flat, rank_flat, off)


def _sc_collect(ys, idx_flat, rank_flat, off, t0, n):
    n_tok = idx_flat.shape[0] // TOP_K
    per_w = TOP_K * n // SC_WORKERS
    per_k = SC_WORKERS // TOP_K
    assert (TOP_K * n) % (SC_WORKERS * SC_CHUNK) == 0 and t0 % 8 == 0

    @pl.kernel(out_type=jax.ShapeDtypeStruct((TOP_K * n, HALF), u32), mesh=_sc_mesh(),
               compiler_params=_sc_params(), name="collect",
               scratch_types=[pltpu.VMEM((per_w,), i32), pltpu.VMEM((per_w,), i32),
                              pltpu.VMEM((N_EXPERTS,), i32), pltpu.VMEM((2, SC_CHUNK, HALF), u32),
                              pltpu.SemaphoreType.DMA((2,)), pltpu.SemaphoreType.DMA((2,))])
    def k(y_hbm, i_hbm, r_hbm, off_hbm, o_hbm, idx_v, rank_v, off_v, rows_v, gsem, wsem):
        w = _sc_worker()
        src = lax.div(w, per_k) * n_tok + t0 + lax.rem(w, per_k) * per_w
        pltpu.sync_copy(off_hbm, off_v)
        pltpu.sync_copy(i_hbm.at[pl.ds(pl.multiple_of(src, 8), per_w)], idx_v)
        pltpu.sync_copy(r_hbm.at[pl.ds(pl.multiple_of(src, 8), per_w)], rank_v)
        _sc_positions(idx_v, rank_v, off_v, per_w)
        _sc_copy_rows(y_hbm, idx_v, o_hbm, w * per_w, per_w, rows_v, gsem, wsem)

    return k(ys, idx_flat, rank_flat, off)


def _ffn_kernel(te_ref, tf_ref, slot_ref, next_ref, rows_ref, nv_ref, xs_ref, wg_hbm, bg_ref, wu_hbm, bu_ref, wd_hbm, bd_ref,
                ys_ref, wbuf, wg_s, wu_s, wd_s, wsem):
    step = pl.program_id(0)
    w_hbm = (wg_hbm, wu_hbm, wd_hbm)
    w_bf16 = (wg_s, wu_s, wd_s)

    def fetch(e, slot):
        return [pltpu.make_async_copy(w_hbm[m].at[e], wbuf.at[slot, m], wsem.at[slot, m]) for m in range(3)]

    @pl.when(step < nv_ref[0])
    def _():
        @pl.when(tf_ref[step] == 1)
        def _():
            slot = slot_ref[step]

            @pl.when(step == 0)
            def _():
                for cp in fetch(te_ref[0], slot):
                    cp.start()

            for cp in fetch(te_ref[step], slot):
                cp.wait()
            for s in range(2):
                @pl.when(slot == s)
                def _():
                    for m in range(3):
                        w_bf16[m][...] = wbuf[s, m].astype(bf16)

            @pl.when(next_ref[step] >= 0)
            def _():
                for cp in fetch(next_ref[step], 1 - slot):
                    cp.start()

        def rows_block(rb):
            rs = slice(rb * FFN_SUB, (rb + 1) * FFN_SUB)
            x = _unpack_rows(xs_ref[rs, :]).astype(bf16)
            g = jnp.minimum(_dot(x, wg_s[...]) + bg_ref[0], SWIGLU_LIMIT)
            u = jnp.clip(_dot(x, wu_s[...]) + bu_ref[0], -SWIGLU_LIMIT, SWIGLU_LIMIT)
            hid = (u + 1.0) * (g * jax.nn.sigmoid(SWIGLU_ALPHA * g))
            y = _dot(hid.astype(bf16), wd_s[...]) + bd_ref[0]
            ys_ref[rs, :] = _pack_rows(y)

        @pl.when(rows_ref[step] > FFN_SUB)
        def _():
            rows_block(0)
            rows_block(1)

        @pl.when(rows_ref[step] <= FFN_SUB)
        def _():
            rows_block(0)
            ys_ref[FFN_SUB:, :] = jnp.zeros((FFN_ROWS - FFN_SUB, HALF), u32)

    @pl.when(step >= nv_ref[0])
    def _():
        ys_ref[...] = jnp.zeros_like(ys_ref)


def _ffn(tables, xs, w_gate, b_gate, w_up, b_up, w_down, b_down):
    n_rows = xs.shape[0]
    n_tiles = n_rows // FFN_ROWS
    assert D_FF == D_MODEL
    assert FFN_ROWS == 2 * FFN_SUB

    def row_map(i, te, tf, sl, nx, rw, nv):
        return (jnp.minimum(i, nv[0] - 1), 0)

    def b_map(i, te, tf, sl, nx, rw, nv):
        return (te[i], 0, 0)

    w_spec = pl.BlockSpec(memory_space=pl.ANY)
    b_spec = pl.BlockSpec((1, 1, D_FF), b_map)
    return pl.pallas_call(
        _ffn_kernel,
        grid_spec=pltpu.PrefetchScalarGridSpec(
            num_scalar_prefetch=6,
            grid=(n_tiles,),
            in_specs=[pl.BlockSpec((FFN_ROWS, HALF), row_map),
                      w_spec, b_spec, w_spec, b_spec, w_spec, b_spec],
            out_specs=pl.BlockSpec((FFN_ROWS, HALF), lambda i, *_: (i, 0)),
            scratch_shapes=[pltpu.VMEM((2, 3, D_MODEL, D_FF), f32),
                            pltpu.VMEM((D_MODEL, D_FF), bf16),
                            pltpu.VMEM((D_MODEL, D_FF), bf16),
                            pltpu.VMEM((D_FF, D_MODEL), bf16),
                            pltpu.SemaphoreType.DMA((2, 3))]),
        out_shape=jax.ShapeDtypeStruct((n_rows, HALF), u32),
        compiler_params=pltpu.CompilerParams(dimension_semantics=("arbitrary",),
                                             vmem_limit_bytes=VMEM_LIMIT),
        name="experts",
    )(*tables, xs, w_gate, b_gate.reshape(N_EXPERTS, 1, D_FF),
      w_up, b_up.reshape(N_EXPERTS, 1, D_FF), w_down, b_down.reshape(N_EXPERTS, 1, D_MODEL))


def _finish_kernel(g_ref, h1a_ref, h1b_ref, pw_ref, pa_ref, pb_ref, g_post_ref, w_ple_ref, w_pg_ref, b_pg_ref,
                   g_ple_ref, *rest, first_steps):
    outa_ref, outb_ref = rest[-2:]
    step = pl.program_id(0)
    rows = h1a_ref.shape[0]
    pw = pw_ref[...]
    f = jnp.zeros((rows, D_MODEL), f32)
    for k in range(TOP_K):
        f = f + pw[:, k:k + 1] * _unpack_rows(g_ref[k])
    h2 = _two_groups(step, first_steps, h1a_ref, h1b_ref) + _rms(f, g_post_ref[...])
    gate = jax.nn.sigmoid(_dot(h2.astype(bf16), w_pg_ref[...]) + b_pg_ref[...])
    pin = _two_groups(step, first_steps, pa_ref, pb_ref).astype(bf16)
    pe = _dot(pin, w_ple_ref[...]) * gate
    out = h2 + _rms(pe, g_ple_ref[...])

    @pl.when(step < first_steps)
    def _():
        outa_ref[...] = out

    @pl.when(step >= first_steps)
    def _():
        outb_ref[...] = out


def _finish(g, h1a, h1b, pw, pa, pb, g_post, w_ple, w_pg, b_pg, g_ple, *, a0, a_steps, b_steps, prev=None):
    na, nb = h1a.shape[0], h1b.shape[0]
    rows = FIN_ROWS
    assert na % rows == 0 and nb % rows == 0 and a_steps >= 1

    def spec_a(width):
        return pl.BlockSpec((rows, width), lambda i: (a0 + jnp.minimum(i, a_steps - 1), 0))

    def spec_b(width):
        return pl.BlockSpec((rows, width), lambda i: (jnp.maximum(i - a_steps, 0), 0))

    in_specs = [pl.BlockSpec((TOP_K, rows, HALF), lambda i: (0, i, 0)),
                spec_a(D_MODEL), spec_b(D_MODEL),
                pl.BlockSpec((rows, LANES), lambda i: (a0 + i, 0)),
                spec_a(PLE_DIM), spec_b(PLE_DIM),
                _full(g_post.shape), _full(w_ple.shape), _full(w_pg.shape),
                _full(b_pg.shape), _full(g_ple.shape)]
    args = [g, h1a, h1b, pw, pa, pb, g_post, w_ple, w_pg, b_pg, g_ple]
    aliases = {}
    if prev is not None:
        in_specs.append(pl.BlockSpec(memory_space=pl.ANY))
        aliases = {len(args): 0}
        args.append(prev)
    return pl.pallas_call(
        functools.partial(_finish_kernel, first_steps=a_steps),
        grid=(a_steps + b_steps,),
        in_specs=in_specs,
        out_specs=[spec_a(D_MODEL), spec_b(D_MODEL)],
        out_shape=[jax.ShapeDtypeStruct((na, D_MODEL), f32), jax.ShapeDtypeStruct((nb, D_MODEL), f32)],
        input_output_aliases=aliases,
        compiler_params=pltpu.CompilerParams(dimension_semantics=("arbitrary",),
                                             vmem_limit_bytes=VMEM_LIMIT),
        name="finish",
    )(*args)


def _block_diag_pairs(w):
    per = MXU_DIM // HEAD_A
    w4 = w.reshape(N_HEADS_A // per, per, HEAD_A, HEAD_A)
    blocks = jnp.where(jnp.eye(per, dtype=bool)[None, :, None, :, None], w4[:, :, :, None, :], 0.0)
    return blocks.reshape(N_HEADS_A // per, MXU_DIM, MXU_DIM)


def _time_major(x):
    x = jnp.swapaxes(x, 0, 1)
    return x.reshape((x.shape[0] * x.shape[1],) + x.shape[2:])


def _batch_major(x, b):
    return jnp.swapaxes(x.reshape(x.shape[0] // b, b, x.shape[1]), 0, 1)


def _routing_tables(counts, n_tiles):
    tiles = (counts + FFN_ROWS - 1) // FFN_ROWS
    tile_ends = jnp.cumsum(tiles)
    off = (tile_ends - tiles) * FFN_ROWS
    n_valid = tile_ends[-1:]
    tile_ids = jnp.minimum(jnp.arange(n_tiles, dtype=i32), n_valid - 1)
    tile_e = jnp.sum((tile_ids[:, None] >= tile_ends[None, :]).astype(i32), axis=1)
    tile_first = jnp.concatenate([jnp.ones((1,), i32), (tile_e[1:] != tile_e[:-1]).astype(i32)])
    tile_slot = (jnp.cumsum(tile_first) - 1) % 2
    cand = jnp.where(tiles > 0, jnp.arange(N_EXPERTS, dtype=i32), N_EXPERTS)
    suffix_min = lax.cummin(cand[::-1])[::-1]
    nxt = jnp.concatenate([suffix_min[1:], jnp.full((1,), N_EXPERTS, i32)])
    onehot = (tile_e[:, None] == jnp.arange(N_EXPERTS, dtype=i32)[None, :]).astype(i32)
    at_tile = lambda per_expert: jnp.sum(onehot * per_expert[None, :].astype(i32), axis=1)
    tile_next = at_tile(jnp.where(nxt < N_EXPERTS, nxt, -1))
    tile_rows = jnp.clip(at_tile(counts) - (tile_ids - at_tile(tile_ends - tiles)) * FFN_ROWS, 0, FFN_ROWS)
    tables = (tile_e.astype(i32), tile_first, tile_slot.astype(i32), tile_next.astype(i32),
              tile_rows.astype(i32), n_valid.astype(i32))
    return off.astype(i32), tables


def _layer(xp, xs_tm, pp, ps_tm, state_h, state_conv, state_pool, lw):
    (norm_mix_pre, w_in, conv_w, conv_b, w_rgate, b_rgate, w_igate, b_igate, lru_lambda,
     pool_w, pool_b, pool_scale, norm_group_a, norm_group_b, w_out, norm_mix_post,
     norm_ffn_pre, w_router, b_router, w_gate, b_gate, w_up, b_up, w_down, b_down, norm_ffn_post,
     w_ple, w_ple_gate, b_ple_gate, norm_ple) = lw
    b_p, s_p = xp.shape[0], xp.shape[1]
    n_p, n_s = b_p * s_p, xs_tm.shape[0]
    b_s = state_h.shape[0]
    row = lambda a: a.reshape(1, -1)

    w_gates = jnp.concatenate([_block_diag_pairs(w_rgate), _block_diag_pairs(w_igate)], axis=-1).astype(bf16)
    mix_w = (row(norm_mix_pre), w_in.astype(bf16), conv_w, row(conv_b),
             w_gates, row(b_rgate), row(b_igate), row(lru_lambda),
             pool_w.astype(bf16), row(pool_b), row(pool_scale), row(norm_group_a), row(norm_group_b),
             w_out.astype(bf16), row(norm_mix_post))

    zeros = lambda *s: jnp.zeros(s, f32)
    h1_p, hfin_p, cfin_p, pfin_p = _mixer(
        xp, zeros((CONV_W - 1) * b_p, W_A), zeros(POOL_BUF * b_p, W_B), zeros(b_p, W_A),
        mix_w, bt=b_p, start=0)
    h1_s, hfin_s, cfin_s, pfin_s = _mixer(
        xs_tm, _time_major(state_conv), _time_major(state_pool), state_h,
        mix_w, bt=b_s, start=PAST_LEN)
    h1_p = h1_p.reshape(n_p, D_MODEL)
    t = n_p + n_s

    w_router_pad = jnp.zeros((D_MODEL, LANES), f32).at[:, :N_EXPERTS].set(w_router)
    b_router_pad = jnp.zeros((1, LANES), f32).at[0, :N_EXPERTS].set(b_router)
    tri = jnp.triu(jnp.ones((ROUTE_ROWS, ROUTE_ROWS), bf16), k=1)
    vpk, idx_t, rank_t, pw, cnt = _router(h1_p, h1_s, row(norm_ffn_pre), w_router_pad.astype(bf16),
                                          b_router_pad, tri)

    n_tiles = (t * TOP_K) // FFN_ROWS + N_EXPERTS
    off, tables = _routing_tables(cnt[:, 0], n_tiles)
    idx_flat, rank_flat = idx_t.reshape(-1), rank_t.reshape(-1)

    xs = _sc_dispatch(vpk, idx_flat, rank_flat, off, n_tiles * FFN_ROWS)
    ys = _ffn(tables, xs, w_gate, b_gate, w_up, b_up, w_down, b_down)
    fin_w = (row(norm_ffn_post), w_ple.astype(bf16), w_ple_gate.astype(bf16), row(b_ple_gate), row(norm_ple))
    a_blocks, b_blocks = n_p // FIN_ROWS, n_s // FIN_ROWS
    first = a_blocks // 2
    n1, n2 = first * FIN_ROWS, t - first * FIN_ROWS
    pp2 = pp.reshape(n_p, PLE_DIM)
    g1 = _sc_collect(ys, idx_flat, rank_flat, off, 0, n1).reshape(TOP_K, n1, HALF)
    out_p, _ = _finish(g1, h1_p, h1_s, pw, pp2, ps_tm, *fin_w, a0=0, a_steps=first, b_steps=0)
    g2 = _sc_collect(ys, idx_flat, rank_flat, off, n1, n2).reshape(TOP_K, n2, HALF)
    out_p, out_s = _finish(g2, h1_p, h1_s, pw, pp2, ps_tm, *fin_w, a0=first, a_steps=a_blocks - first,
                           b_steps=b_blocks, prev=out_p)

    states = (hfin_p, _batch_major(cfin_p, b_p), _batch_major(pfin_p, b_p),
              hfin_s, _batch_major(cfin_s, b_s), _batch_major(pfin_s, b_s))
    return out_p.reshape(b_p, s_p, D_MODEL), out_s, states


def kernel(x_prompt, x_sample, state_rglru_h, state_rglru_conv, state_pool, p_prompt, p_sample, norm_mix_pre, w_in, conv_w, conv_b, w_rgate, b_rgate, w_igate, b_igate, lru_lambda, pool_w, pool_b, pool_scale, norm_group_a, norm_group_b, w_out, norm_mix_post, norm_ffn_pre, w_router, b_router, w_gate, b_gate, w_up, b_up, w_down, b_down, norm_ffn_post, w_ple, w_ple_gate, b_ple_gate, norm_ple):
    depth = w_in.shape[0]
    b_p, b_s = x_prompt.shape[0], x_sample.shape[0]
    per_layer = (norm_mix_pre, w_in, conv_w, conv_b, w_rgate, b_rgate, w_igate, b_igate, lru_lambda,
                 pool_w, pool_b, pool_scale, norm_group_a, norm_group_b, w_out, norm_mix_post,
                 norm_ffn_pre, w_router, b_router, w_gate, b_gate, w_up, b_up, w_down, b_down,
                 norm_ffn_post, w_ple, w_ple_gate, b_ple_gate, norm_ple)
    hp, hs = x_prompt, _time_major(x_sample)
    collected = []
    for i in range(depth):
        hp, hs, states = _layer(hp, hs, p_prompt[i], _time_major(p_sample[i]),
                                state_rglru_h[i], state_rglru_conv[i], state_pool[i],
                                tuple(w[i] for w in per_layer))
        collected.append(states)
    stacked = tuple(jnp.stack([c[j] for c in collected]) for j in range(6))
    return (hp, _batch_major(hs, b_s)) + stacked
```

```python
import dataclasses
import functools

import jax
import jax.numpy as jnp
from jax import lax
from jax.experimental import pallas as pl
from jax.experimental.pallas import tpu as pltpu
from jax.experimental.pallas import tpu_sc as plsc

D_MODEL = 1024
W_A = 512
W_B = 512
N_HEADS_A = 8
HEAD_A = W_A // N_HEADS_A
CONV_W = 4
LRU_C = 8.0
POOL_WINDOWS = (2, 4, 8, 16)
GROUP_B = W_B // len(POOL_WINDOWS)
POOL_BUF = max(POOL_WINDOWS) - 1
N_EXPERTS = 32
TOP_K = 4
D_FF = 1024
SWIGLU_LIMIT = 7.0
SWIGLU_ALPHA = 1.702
PLE_DIM = 256
EPS = 1e-6
PAST_LEN = 16384

LANES = 128
MXU_DIM = 256
HALF = D_MODEL // 2
SC_CORES = 2
SC_SUBCORES = 16
SC_WORKERS = SC_CORES * SC_SUBCORES
SC_LANES = 16
SC_CHUNK = 64

MIX_ROWS = 512
ROUTE_ROWS = 512
FFN_ROWS = 512
FFN_SUB = 256
FIN_ROWS = 512
VMEM_LIMIT = 56 * 1024 * 1024

f32 = jnp.float32
bf16 = jnp.bfloat16
u32 = jnp.uint32
i32 = jnp.int32


def _rms(x, g):
    return x * lax.rsqrt(jnp.mean(x * x, axis=-1, keepdims=True) + EPS) * g


def _dot(a, b):
    return jnp.dot(a, b, preferred_element_type=f32)


def _pack_rows(x):
    bits = lax.bitcast_convert_type(x.astype(bf16).astype(f32), u32)
    return (bits[:, HALF:] & jnp.uint32(0xFFFF0000)) | (bits[:, :HALF] >> 16)


def _unpack_rows(p):
    lo = lax.bitcast_convert_type(p << 16, f32)
    hi = lax.bitcast_convert_type(p & jnp.uint32(0xFFFF0000), f32)
    return jnp.concatenate([lo, hi], axis=-1)


def _mixer_kernel(x_ref, conv0_ref, pool0_ref, h0_ref, g_pre_ref, w_in_ref, conv_w_ref, conv_b_ref,
                  w_gates_ref, br_ref, bi_ref, lam_ref,
                  pool_w_ref, pool_b_ref, pool_scale_ref, ng_a_ref, ng_b_ref, w_out_ref, g_post_ref,
                  h1_ref, hfin_ref, convfin_ref, poolfin_ref,
                  h_s, conv_s, pool_s, a_s, b_s, hs_s, *, bt, tt, start):
    step = pl.program_id(0)
    rows = bt * tt

    @pl.when(step == 0)
    def _():
        h_s[...] = h0_ref[...]
        conv_s[...] = conv0_ref[...]
        pool_s[...] = pool0_ref[...]

    if x_ref.ndim == 3:
        x = pltpu.einshape("btd->tbd", x_ref[...]).reshape(rows, D_MODEL)
    else:
        x = x_ref[...]
    u = _rms(x, g_pre_ref[...]).astype(bf16)
    z = _dot(u, w_in_ref[...])
    xa, ga, xb = z[:, :W_A], z[:, W_A:2 * W_A], z[:, 2 * W_A:]

    ext_a = jnp.concatenate([conv_s[...], xa], axis=0)
    xc = conv_b_ref[...]
    for k in range(CONV_W):
        xc = xc + ext_a[k * bt:k * bt + rows] * conv_w_ref[k:k + 1, :]
    conv_s[...] = ext_a[rows:]

    xc16 = xc.astype(bf16)
    gates = [_dot(xc16[:, j * MXU_DIM:(j + 1) * MXU_DIM], w_gates_ref[j]) for j in range(W_A // MXU_DIM)]
    r = jax.nn.sigmoid(jnp.concatenate([gj[:, :MXU_DIM] for gj in gates], axis=-1) + br_ref[...])
    ig = jax.nn.sigmoid(jnp.concatenate([gj[:, MXU_DIM:] for gj in gates], axis=-1) + bi_ref[...])
    lam = lam_ref[...]
    softplus_neg = jnp.maximum(-lam, 0.0) + jnp.log1p(jnp.exp(-jnp.abs(lam)))
    log_a = (-LRU_C) * r * softplus_neg
    a_s[...] = jnp.exp(log_a)
    th = jnp.tanh(log_a)
    b_s[...] = jnp.sqrt(-2.0 * th / (1.0 - th)) * (ig * xc)

    def scan_step(t, h):
        sl = pl.ds(pl.multiple_of(t * bt, bt), bt)
        h = a_s[sl, :] * h + b_s[sl, :]
        hs_s[sl, :] = h
        return h

    h_last = lax.fori_loop(0, tt, scan_step, h_s[...], unroll=True)
    h_s[...] = h_last
    ya = hs_s[...] * jax.nn.gelu(ga)

    ext_b = jnp.concatenate([pool_s[...], xb], axis=0)
    pool_s[...] = ext_b[rows:]
    s2 = ext_b[bt:, :] + ext_b[:-bt, :]
    s4 = s2[2 * bt:, GROUP_B:] + s2[:-2 * bt, GROUP_B:]
    s8 = s4[4 * bt:, GROUP_B:] + s4[:-4 * bt, GROUP_B:]
    s16 = s8[8 * bt:, GROUP_B:] + s8[:-8 * bt, GROUP_B:]
    wins = (s2[14 * bt:, :GROUP_B], s4[12 * bt:, :GROUP_B], s8[8 * bt:, :GROUP_B], s16)
    t_idx = lax.broadcasted_iota(i32, (rows, GROUP_B), 0) // bt
    pos1 = (t_idx + (step * tt + start + 1)).astype(f32)
    yb_parts = []
    for g, w in enumerate(POOL_WINDOWS):
        cnt = jnp.minimum(jnp.float32(w), pos1)
        d = wins[g] / cnt - xb[:, g * GROUP_B:(g + 1) * GROUP_B]
        yb_parts.append(_dot(d.astype(bf16), pool_w_ref[g]))
    yb = (jnp.concatenate(yb_parts, axis=-1) + pool_b_ref[...]) * pool_scale_ref[...]

    na = _rms(ya, ng_a_ref[...])
    nb = _rms(yb, ng_b_ref[...])
    m = _dot(jnp.concatenate([na, nb], axis=-1).astype(bf16), w_out_ref[...])
    h1 = x + _rms(m, g_post_ref[...])
    if h1_ref.ndim == 3:
        h1_ref[...] = pltpu.einshape("tbd->btd", h1.reshape(tt, bt, D_MODEL))
    else:
        h1_ref[...] = h1

    hfin_ref[...] = h_last
    convfin_ref[...] = conv_s[...]
    poolfin_ref[...] = pool_s[...]


def _full(shape):
    return pl.BlockSpec(shape, lambda i, *_: (0,) * len(shape))


def _mixer(x, conv0, pool0, h0, wts, *, bt, start):
    tt = MIX_ROWS // bt
    rows = bt * tt
    if x.ndim == 3:
        assert x.shape[0] == bt and x.shape[1] % tt == 0
        n_steps = x.shape[1] // tt
        x_spec = pl.BlockSpec((bt, tt, D_MODEL), lambda i: (0, i, 0))
    else:
        assert x.shape[0] % rows == 0
        n_steps = x.shape[0] // rows
        x_spec = pl.BlockSpec((rows, D_MODEL), lambda i: (i, 0))
    kern = functools.partial(_mixer_kernel, bt=bt, tt=tt, start=start)
    small = [conv0, pool0, h0] + list(wts)
    return pl.pallas_call(
        kern,
        grid=(n_steps,),
        in_specs=[x_spec] + [_full(a.shape) for a in small],
        out_specs=[x_spec,
                   _full((bt, W_A)), _full(((CONV_W - 1) * bt, W_A)), _full((POOL_BUF * bt, W_B))],
        out_shape=[jax.ShapeDtypeStruct(x.shape, f32),
                   jax.ShapeDtypeStruct((bt, W_A), f32),
                   jax.ShapeDtypeStruct(((CONV_W - 1) * bt, W_A), f32),
                   jax.ShapeDtypeStruct((POOL_BUF * bt, W_B), f32)],
        scratch_shapes=[pltpu.VMEM((bt, W_A), f32),
                        pltpu.VMEM(((CONV_W - 1) * bt, W_A), f32),
                        pltpu.VMEM((POOL_BUF * bt, W_B), f32),
                        pltpu.VMEM((rows, W_A), f32),
                        pltpu.VMEM((rows, W_A), f32),
                        pltpu.VMEM((rows, W_A), f32)],
        compiler_params=pltpu.CompilerParams(dimension_semantics=("arbitrary",),
                                             vmem_limit_bytes=VMEM_LIMIT),
        name="mixer",
    )(x, *small)


def _two_groups(step, first_steps, a_ref, b_ref):
    return jnp.where(step < first_steps, a_ref[...], b_ref[...])


def _router_kernel(h1a_ref, h1b_ref, g_ref, wr_ref, br_ref, tri_ref,
                   vpk_ref, idx_ref, rank_ref, pw_ref, cnt_ref, carry_s, *, first_steps):
    step = pl.program_id(0)
    rows = h1a_ref.shape[0]

    @pl.when(step == 0)
    def _():
        carry_s[...] = jnp.zeros_like(carry_s)

    v = _rms(_two_groups(step, first_steps, h1a_ref, h1b_ref), g_ref[...])
    vpk_ref[...] = _pack_rows(v)
    logits = _dot(v.astype(bf16), wr_ref[...]) + br_ref[...]
    lt = jnp.transpose(logits)[:N_EXPERTS, :]

    eio = lax.broadcasted_iota(i32, (N_EXPERTS, rows), 0).astype(f32)
    work = lt
    vals, idxs, sels = [], [], []
    for _ in range(TOP_K):
        m = jnp.max(work, axis=0, keepdims=True)
        ik = jnp.min(jnp.where(work == m, eio, float(N_EXPERTS)), axis=0, keepdims=True)
        sel = eio == ik
        vals.append(m)
        idxs.append(ik)
        sels.append(sel)
        work = jnp.where(sel, -jnp.inf, work)
    exps = [jnp.exp(val - vals[0]) for val in vals]
    denom = exps[0] + exps[1] + exps[2] + exps[3]
    probs = [e / denom for e in exps]

    chosen = sels[0] | sels[1] | sels[2] | sels[3]
    onehot = jnp.where(chosen, 1.0, 0.0)
    before = _dot(onehot.astype(bf16), tri_ref[...])
    base = carry_s[...] + before
    ranks = [jnp.sum(jnp.where(sel, base, 0.0), axis=0, keepdims=True) for sel in sels]
    carry_s[...] = carry_s[...] + jnp.sum(onehot, axis=1, keepdims=True)

    idx_ref[...] = jnp.concatenate(idxs, axis=0).astype(i32)
    rank_ref[...] = jnp.concatenate(ranks, axis=0).astype(i32)
    pad = jnp.zeros((LANES - TOP_K, rows), f32)
    pw_ref[...] = jnp.transpose(jnp.concatenate(probs + [pad], axis=0))
    cnt_ref[...] = carry_s[:, :LANES].astype(i32)


def _group_specs(rows, width, na, nb):
    first = na // rows
    return (pl.BlockSpec((rows, width), lambda i: (jnp.minimum(i, first - 1), 0)),
            pl.BlockSpec((rows, width), lambda i: (jnp.maximum(i - first, 0), 0)))


def _router(h1a, h1b, g, wr, br, tri):
    na, nb = h1a.shape[0], h1b.shape[0]
    t = na + nb
    rows = ROUTE_ROWS
    assert na % rows == 0 and nb % rows == 0
    return pl.pallas_call(
        functools.partial(_router_kernel, first_steps=na // rows),
        grid=(t // rows,),
        in_specs=[*_group_specs(rows, D_MODEL, na, nb),
                  _full(g.shape), _full(wr.shape), _full(br.shape), _full(tri.shape)],
        out_specs=[pl.BlockSpec((rows, HALF), lambda i: (i, 0)),
                   pl.BlockSpec((TOP_K, rows), lambda i: (0, i)),
                   pl.BlockSpec((TOP_K, rows), lambda i: (0, i)),
                   pl.BlockSpec((rows, LANES), lambda i: (i, 0)),
                   _full((N_EXPERTS, LANES))],
        out_shape=[jax.ShapeDtypeStruct((t, HALF), u32),
                   jax.ShapeDtypeStruct((TOP_K, t), i32),
                   jax.ShapeDtypeStruct((TOP_K, t), i32),
                   jax.ShapeDtypeStruct((t, LANES), f32),
                   jax.ShapeDtypeStruct((N_EXPERTS, LANES), i32)],
        scratch_shapes=[pltpu.VMEM((N_EXPERTS, rows), f32)],
        compiler_params=pltpu.CompilerParams(dimension_semantics=("arbitrary",),
                                             vmem_limit_bytes=VMEM_LIMIT),
        name="router",
    )(h1a, h1b, g, wr, br, tri)


def _sc_mesh():
    return plsc.VectorSubcoreMesh(core_axis_name="core", subcore_axis_name="subcore")


def _sc_params():
    return dataclasses.replace(pltpu.CompilerParams(), needs_layout_passes=False)


def _sc_worker():
    return lax.axis_index("subcore") * SC_CORES + lax.axis_index("core")


def _sc_positions(idx_v, rank_v, off_v, n):
    @pl.loop(0, n // SC_LANES)
    def _(i):
        sl = pl.ds(pl.multiple_of(i * SC_LANES, SC_LANES), SC_LANES)
        idx_v[sl] = plsc.load_gather(off_v, [idx_v[sl]]) + rank_v[sl]


def _sc_copy_rows(src_hbm, idx_v, dst_hbm, dst_row0, n, rows_v, gsem, wsem):
    n_ch = n // SC_CHUNK
    n_pair = n_ch // 2

    def gather(c, b):
        o = pl.multiple_of(c * SC_CHUNK, SC_CHUNK)
        return pltpu.make_async_copy(src_hbm.at[idx_v.at[pl.ds(o, SC_CHUNK)]], rows_v.at[b], gsem.at[b])

    def write(c, b):
        o = pl.multiple_of(c * SC_CHUNK, SC_CHUNK)
        return pltpu.make_async_copy(rows_v.at[b], dst_hbm.at[pl.ds(dst_row0 + o, SC_CHUNK)], wsem.at[b])

    gather(0, 0).start()

    @pl.loop(0, n_pair)
    def _(i):
        c0 = 2 * i
        gather(c0, 0).wait()

        @pl.when(i > 0)
        def _():
            write(c0 - 1, 1).wait()

        gather(c0 + 1, 1).start()
        write(c0, 0).start()
        gather(c0 + 1, 1).wait()
        write(c0, 0).wait()

        @pl.when(c0 + 2 < n_ch)
        def _():
            gather(c0 + 2, 0).start()

        write(c0 + 1, 1).start()

    if n_ch % 2:
        gather(n_ch - 1, 0).wait()
        write(n_ch - 1, 0).start()
        write(n_ch - 1, 0).wait()
    if n_pair:
        write(2 * n_pair - 1, 1).wait()


def _sc_dispatch(vpk, idx_flat, rank_flat, off, row0, n_out):
    n_tok = vpk.shape[0]
    per_w = n_out // SC_WORKERS
    assert n_out % (SC_WORKERS * SC_CHUNK) == 0 and n_tok % SC_LANES == 0 and row0 % SC_LANES == 0

    @pl.kernel(out_type=jax.ShapeDtypeStruct((n_out, HALF), u32), mesh=_sc_mesh(),
               compiler_params=_sc_params(), name="dispatch",
               scratch_types=[pltpu.VMEM((per_w,), i32), pltpu.VMEM((n_tok,), i32), pltpu.VMEM((n_tok,), i32),
                              pltpu.VMEM((N_EXPERTS,), i32), pltpu.VMEM((2, SC_CHUNK, HALF), u32),
                              pltpu.SemaphoreType.DMA((2,)), pltpu.SemaphoreType.DMA((2,))])
    def k(v_hbm, i_hbm, r_hbm, off_hbm, o_hbm, src_v, idx_v, rank_v, off_v, rows_v, gsem, wsem):
        out0 = _sc_worker() * per_w
        lo = row0 + out0
        pltpu.sync_copy(off_hbm, off_v)

        @pl.loop(0, per_w // SC_LANES)
        def _(i):
            o = pl.multiple_of(i * SC_LANES, SC_LANES)
            src_v[pl.ds(o, SC_LANES)] = lax.iota(i32, SC_LANES) + lax.rem(lo + o, n_tok)

        @pl.loop(0, TOP_K)
        def _(kk):
            pltpu.sync_copy(i_hbm.at[pl.ds(kk * n_tok, n_tok)], idx_v)
            pltpu.sync_copy(r_hbm.at[pl.ds(kk * n_tok, n_tok)], rank_v)

            @plsc.parallel_loop(0, n_tok, step=SC_LANES, unroll=8)
            def _(o):
                sl = pl.ds(pl.multiple_of(o, SC_LANES), SC_LANES)
                p = plsc.load_gather(off_v, [idx_v[sl]]) + rank_v[sl] - lo
                mine = (p >= 0) & (p < per_w)
                tok = lax.iota(i32, SC_LANES) + o
                plsc.store_scatter(src_v, [jnp.where(mine, p, 0)], tok, mask=mine)

        _sc_copy_rows(v_hbm, src_v, o_hbm, out0, per_w, rows_v, gsem, wsem)

    return k(vpk, idx_flat, rank_flat, off)


def _sc_collect(ys, idx_flat, rank_flat, off, t0, n):
    n_tok = idx_flat.shape[0] // TOP_K
    per_w = TOP_K * n // SC_WORKERS
    per_k = SC_WORKERS // TOP_K
    assert (TOP_K * n) % (SC_WORKERS * SC_CHUNK) == 0 and t0 % 8 == 0

    @pl.kernel(out_type=jax.ShapeDtypeStruct((TOP_K * n, HALF), u32), mesh=_sc_mesh(),
               compiler_params=_sc_params(), name="collect",
               scratch_types=[pltpu.VMEM((per_w,), i32), pltpu.VMEM((per_w,), i32),
                              pltpu.VMEM((N_EXPERTS,), i32), pltpu.VMEM((2, SC_CHUNK, HALF), u32),
                              pltpu.SemaphoreType.DMA((2,)), pltpu.SemaphoreType.DMA((2,))])
    def k(y_hbm, i_hbm, r_hbm, off_hbm, o_hbm, idx_v, rank_v, off_v, rows_v, gsem, wsem):
        w = _sc_worker()
        src = lax.div(w, per_k) * n_tok + t0 + lax.rem(w, per_k) * per_w
        pltpu.sync_copy(off_hbm, off_v)
        pltpu.sync_copy(i_hbm.at[pl.ds(pl.multiple_of(src, 8), per_w)], idx_v)
        pltpu.sync_copy(r_hbm.at[pl.ds(pl.multiple_of(src, 8), per_w)], rank_v)
        _sc_positions(idx_v, rank_v, off_v, per_w)
        _sc_copy_rows(y_hbm, idx_v, o_hbm, w * per_w, per_w, rows_v, gsem, wsem)

    return k(ys, idx_flat, rank_flat, off)


def _ffn_kernel(te_ref, tf_ref, slot_ref, next_ref, rows_ref, nv_ref, xs_ref, wg_hbm, bg_ref, wu_hbm, bu_ref, wd_hbm, bd_ref,
                *rest):
    ys_ref, wbuf, wg_s, wu_s, wd_s, wsem = rest[-6:]
    step = pl.program_id(0)
    w_hbm = (wg_hbm, wu_hbm, wd_hbm)
    w_bf16 = (wg_s, wu_s, wd_s)

    def fetch(e, slot):
        return [pltpu.make_async_copy(w_hbm[m].at[e], wbuf.at[slot, m], wsem.at[slot, m]) for m in range(3)]

    @pl.when(step < nv_ref[0])
    def _():
        @pl.when(tf_ref[step] == 1)
        def _():
            slot = slot_ref[step]

            @pl.when(step == 0)
            def _():
                for cp in fetch(te_ref[0], slot):
                    cp.start()

            for cp in fetch(te_ref[step], slot):
                cp.wait()
            for s in range(2):
                @pl.when(slot == s)
                def _():
                    for m in range(3):
                        w_bf16[m][...] = wbuf[s, m].astype(bf16)

            @pl.when(next_ref[step] >= 0)
            def _():
                for cp in fetch(next_ref[step], 1 - slot):
                    cp.start()

        def rows_block(rb):
            rs = slice(rb * FFN_SUB, (rb + 1) * FFN_SUB)
            x = _unpack_rows(xs_ref[rs, :]).astype(bf16)
            g = jnp.minimum(_dot(x, wg_s[...]) + bg_ref[0], SWIGLU_LIMIT)
            u = jnp.clip(_dot(x, wu_s[...]) + bu_ref[0], -SWIGLU_LIMIT, SWIGLU_LIMIT)
            hid = (u + 1.0) * (g * jax.nn.sigmoid(SWIGLU_ALPHA * g))
            y = _dot(hid.astype(bf16), wd_s[...]) + bd_ref[0]
            ys_ref[rs, :] = _pack_rows(y)

        @pl.when(rows_ref[step] > FFN_SUB)
        def _():
            rows_block(0)
            rows_block(1)

        @pl.when(rows_ref[step] <= FFN_SUB)
        def _():
            rows_block(0)
            ys_ref[FFN_SUB:, :] = jnp.zeros((FFN_ROWS - FFN_SUB, HALF), u32)

    @pl.when(step >= nv_ref[0])
    def _():
        ys_ref[...] = jnp.zeros_like(ys_ref)


def _ffn(tables, xs, n_rows_total, tile0, w_gate, b_gate, w_up, b_up, w_down, b_down, prev=None):
    n_tiles = xs.shape[0] // FFN_ROWS
    assert D_FF == D_MODEL
    assert FFN_ROWS == 2 * FFN_SUB

    def row_map(i, te, tf, sl, nx, rw, nv):
        return (jnp.minimum(i, jnp.maximum(nv[0] - 1, 0)), 0)

    def b_map(i, te, tf, sl, nx, rw, nv):
        return (te[i], 0, 0)

    w_spec = pl.BlockSpec(memory_space=pl.ANY)
    b_spec = pl.BlockSpec((1, 1, D_FF), b_map)
    in_specs = [pl.BlockSpec((FFN_ROWS, HALF), row_map), w_spec, b_spec, w_spec, b_spec, w_spec, b_spec]
    args = [xs, w_gate, b_gate.reshape(N_EXPERTS, 1, D_FF), w_up, b_up.reshape(N_EXPERTS, 1, D_FF),
            w_down, b_down.reshape(N_EXPERTS, 1, D_MODEL)]
    aliases = {}
    if prev is not None:
        in_specs.append(pl.BlockSpec(memory_space=pl.ANY))
        aliases = {len(tables) + len(args): 0}
        args.append(prev)
    return pl.pallas_call(
        _ffn_kernel,
        grid_spec=pltpu.PrefetchScalarGridSpec(
            num_scalar_prefetch=6,
            grid=(n_tiles,),
            in_specs=in_specs,
            out_specs=pl.BlockSpec((FFN_ROWS, HALF), lambda i, *_: (tile0 + i, 0)),
            scratch_shapes=[pltpu.VMEM((2, 3, D_MODEL, D_FF), f32),
                            pltpu.VMEM((D_MODEL, D_FF), bf16),
                            pltpu.VMEM((D_MODEL, D_FF), bf16),
                            pltpu.VMEM((D_FF, D_MODEL), bf16),
                            pltpu.SemaphoreType.DMA((2, 3))]),
        out_shape=jax.ShapeDtypeStruct((n_rows_total, HALF), u32),
        input_output_aliases=aliases,
        compiler_params=pltpu.CompilerParams(dimension_semantics=("arbitrary",),
                                             vmem_limit_bytes=VMEM_LIMIT),
        name="experts",
    )(*tables, *args)


def _finish_kernel(g_ref, h1a_ref, h1b_ref, pw_ref, pa_ref, pb_ref, g_post_ref, w_ple_ref, w_pg_ref, b_pg_ref,
                   g_ple_ref, *rest, first_steps):
    outa_ref, outb_ref = rest[-2:]
    step = pl.program_id(0)
    rows = h1a_ref.shape[0]
    pw = pw_ref[...]
    f = jnp.zeros((rows, D_MODEL), f32)
    for k in range(TOP_K):
        f = f + pw[:, k:k + 1] * _unpack_rows(g_ref[k])
    h2 = _two_groups(step, first_steps, h1a_ref, h1b_ref) + _rms(f, g_post_ref[...])
    gate = jax.nn.sigmoid(_dot(h2.astype(bf16), w_pg_ref[...]) + b_pg_ref[...])
    pin = _two_groups(step, first_steps, pa_ref, pb_ref).astype(bf16)
    pe = _dot(pin, w_ple_ref[...]) * gate
    out = h2 + _rms(pe, g_ple_ref[...])

    @pl.when(step < first_steps)
    def _():
        outa_ref[...] = out

    @pl.when(step >= first_steps)
    def _():
        outb_ref[...] = out


def _finish(g, h1a, h1b, pw, pa, pb, g_post, w_ple, w_pg, b_pg, g_ple, *, a0, a_steps, b_steps, prev=None):
    na, nb = h1a.shape[0], h1b.shape[0]
    rows = FIN_ROWS
    assert na % rows == 0 and nb % rows == 0 and a_steps >= 1

    def spec_a(width):
        return pl.BlockSpec((rows, width), lambda i: (a0 + jnp.minimum(i, a_steps - 1), 0))

    def spec_b(width):
        return pl.BlockSpec((rows, width), lambda i: (jnp.maximum(i - a_steps, 0), 0))

    in_specs = [pl.BlockSpec((TOP_K, rows, HALF), lambda i: (0, i, 0)),
                spec_a(D_MODEL), spec_b(D_MODEL),
                pl.BlockSpec((rows, LANES), lambda i: (a0 + i, 0)),
                spec_a(PLE_DIM), spec_b(PLE_DIM),
                _full(g_post.shape), _full(w_ple.shape), _full(w_pg.shape),
                _full(b_pg.shape), _full(g_ple.shape)]
    args = [g, h1a, h1b, pw, pa, pb, g_post, w_ple, w_pg, b_pg, g_ple]
    aliases = {}
    if prev is not None:
        in_specs.append(pl.BlockSpec(memory_space=pl.ANY))
        aliases = {len(args): 0}
        args.append(prev)
    return pl.pallas_call(
        functools.partial(_finish_kernel, first_steps=a_steps),
        grid=(a_steps + b_steps,),
        in_specs=in_specs,
        out_specs=[spec_a(D_MODEL), spec_b(D_MODEL)],
        out_shape=[jax.ShapeDtypeStruct((na, D_MODEL), f32), jax.ShapeDtypeStruct((nb, D_MODEL), f32)],
        input_output_aliases=aliases,
        compiler_params=pltpu.CompilerParams(dimension_semantics=("arbitrary",),
                                             vmem_limit_bytes=VMEM_LIMIT),
        name="finish",
    )(*args)


def _block_diag_pairs(w):
    per = MXU_DIM // HEAD_A
    w4 = w.reshape(N_HEADS_A // per, per, HEAD_A, HEAD_A)
    blocks = jnp.where(jnp.eye(per, dtype=bool)[None, :, None, :, None], w4[:, :, :, None, :], 0.0)
    return blocks.reshape(N_HEADS_A // per, MXU_DIM, MXU_DIM)


def _time_major(x):
    x = jnp.swapaxes(x, 0, 1)
    return x.reshape((x.shape[0] * x.shape[1],) + x.shape[2:])


def _batch_major(x, b):
    return jnp.swapaxes(x.reshape(x.shape[0] // b, b, x.shape[1]), 0, 1)


def _routing_tables(counts, n_tiles):
    tiles = (counts + FFN_ROWS - 1) // FFN_ROWS
    tile_ends = jnp.cumsum(tiles)
    off = (tile_ends - tiles) * FFN_ROWS
    n_valid = tile_ends[-1:]
    tile_ids = jnp.minimum(jnp.arange(n_tiles, dtype=i32), n_valid - 1)
    tile_e = jnp.sum((tile_ids[:, None] >= tile_ends[None, :]).astype(i32), axis=1)
    cand = jnp.where(tiles > 0, jnp.arange(N_EXPERTS, dtype=i32), N_EXPERTS)
    suffix_min = lax.cummin(cand[::-1])[::-1]
    nxt = jnp.concatenate([suffix_min[1:], jnp.full((1,), N_EXPERTS, i32)])
    onehot = (tile_e[:, None] == jnp.arange(N_EXPERTS, dtype=i32)[None, :]).astype(i32)
    at_tile = lambda per_expert: jnp.sum(onehot * per_expert[None, :].astype(i32), axis=1)
    tile_next = at_tile(jnp.where(nxt < N_EXPERTS, nxt, -1))
    tile_rows = jnp.clip(at_tile(counts) - (tile_ids - at_tile(tile_ends - tiles)) * FFN_ROWS, 0, FFN_ROWS)
    tile_group_end = at_tile(tile_ends)
    return off.astype(i32), (tile_e.astype(i32), tile_next.astype(i32), tile_group_end.astype(i32),
                             tile_rows.astype(i32), n_valid.astype(i32))


def _tile_tables(full, tile0, n):
    tile_e, tile_next, tile_group_end, tile_rows = (a[tile0:tile0 + n] for a in full[:4])
    n_valid = full[4]
    first = jnp.concatenate([jnp.ones((1,), i32), (tile_e[1:] != tile_e[:-1]).astype(i32)])
    slot = (jnp.cumsum(first) - 1) % 2
    nxt = jnp.where(tile_group_end < tile0 + n, tile_next, -1)
    return (tile_e, first, slot.astype(i32), nxt.astype(i32), tile_rows, jnp.clip(n_valid - tile0, 0, n))


def _layer(xp, xs_tm, pp, ps_tm, state_h, state_conv, state_pool, lw):
    (norm_mix_pre, w_in, conv_w, conv_b, w_rgate, b_rgate, w_igate, b_igate, lru_lambda,
     pool_w, pool_b, pool_scale, norm_group_a, norm_group_b, w_out, norm_mix_post,
     norm_ffn_pre, w_router, b_router, w_gate, b_gate, w_up, b_up, w_down, b_down, norm_ffn_post,
     w_ple, w_ple_gate, b_ple_gate, norm_ple) = lw
    b_p, s_p = xp.shape[0], xp.shape[1]
    n_p, n_s = b_p * s_p, xs_tm.shape[0]
    b_s = state_h.shape[0]
    row = lambda a: a.reshape(1, -1)

    w_gates = jnp.concatenate([_block_diag_pairs(w_rgate), _block_diag_pairs(w_igate)], axis=-1).astype(bf16)
    mix_w = (row(norm_mix_pre), w_in.astype(bf16), conv_w, row(conv_b),
             w_gates, row(b_rgate), row(b_igate), row(lru_lambda),
             pool_w.astype(bf16), row(pool_b), row(pool_scale), row(norm_group_a), row(norm_group_b),
             w_out.astype(bf16), row(norm_mix_post))

    zeros = lambda *s: jnp.zeros(s, f32)
    h1_p, hfin_p, cfin_p, pfin_p = _mixer(
        xp, zeros((CONV_W - 1) * b_p, W_A), zeros(POOL_BUF * b_p, W_B), zeros(b_p, W_A),
        mix_w, bt=b_p, start=0)
    h1_s, hfin_s, cfin_s, pfin_s = _mixer(
        xs_tm, _time_major(state_conv), _time_major(state_pool), state_h,
        mix_w, bt=b_s, start=PAST_LEN)
    h1_p = h1_p.reshape(n_p, D_MODEL)
    t = n_p + n_s

    w_router_pad = jnp.zeros((D_MODEL, LANES), f32).at[:, :N_EXPERTS].set(w_router)
    b_router_pad = jnp.zeros((1, LANES), f32).at[0, :N_EXPERTS].set(b_router)
    tri = jnp.triu(jnp.ones((ROUTE_ROWS, ROUTE_ROWS), bf16), k=1)
    vpk, idx_t, rank_t, pw, cnt = _router(h1_p, h1_s, row(norm_ffn_pre), w_router_pad.astype(bf16),
                                          b_router_pad, tri)

    n_tiles = (t * TOP_K) // FFN_ROWS + N_EXPERTS
    off, tables = _routing_tables(cnt[:, 0], n_tiles)
    idx_flat, rank_flat = idx_t.reshape(-1), rank_t.reshape(-1)

    step_tiles = SC_WORKERS * SC_CHUNK // FFN_ROWS
    tiles_1 = (n_tiles // 2) // step_tiles * step_tiles
    ys = None
    for tile0, n in ((0, tiles_1), (tiles_1, n_tiles - tiles_1)):
        xs = _sc_dispatch(vpk, idx_flat, rank_flat, off, tile0 * FFN_ROWS, n * FFN_ROWS)
        ys = _ffn(_tile_tables(tables, tile0, n), xs, n_tiles * FFN_ROWS, tile0,
                  w_gate, b_gate, w_up, b_up, w_down, b_down, prev=ys)
    fin_w = (row(norm_ffn_post), w_ple.astype(bf16), w_ple_gate.astype(bf16), row(b_ple_gate), row(norm_ple))
    a_blocks, b_blocks = n_p // FIN_ROWS, n_s // FIN_ROWS
    first = a_blocks // 2
    n1, n2 = first * FIN_ROWS, t - first * FIN_ROWS
    pp2 = pp.reshape(n_p, PLE_DIM)
    g1 = _sc_collect(ys, idx_flat, rank_flat, off, 0, n1).reshape(TOP_K, n1, HALF)
    out_p, _ = _finish(g1, h1_p, h1_s, pw, pp2, ps_tm, *fin_w, a0=0, a_steps=first, b_steps=0)
    g2 = _sc_collect(ys, idx_flat, rank_flat, off, n1, n2).reshape(TOP_K, n2, HALF)
    out_p, out_s = _finish(g2, h1_p, h1_s, pw, pp2, ps_tm, *fin_w, a0=first, a_steps=a_blocks - first,
                           b_steps=b_blocks, prev=out_p)

    states = (hfin_p, _batch_major(cfin_p, b_p), _batch_major(pfin_p, b_p),
              hfin_s, _batch_major(cfin_s, b_s), _batch_major(pfin_s, b_s))
    return out_p.reshape(b_p, s_p, D_MODEL), out_s, states


def kernel(x_prompt, x_sample, state_rglru_h, state_rglru_conv, state_pool, p_prompt, p_sample, norm_mix_pre, w_in, conv_w, conv_b, w_rgate, b_rgate, w_igate, b_igate, lru_lambda, pool_w, pool_b, pool_scale, norm_group_a, norm_group_b, w_out, norm_mix_post, norm_ffn_pre, w_router, b_router, w_gate, b_gate, w_up, b_up, w_down, b_down, norm_ffn_post, w_ple, w_ple_gate, b_ple_gate, norm_ple):
    depth = w_in.shape[0]
    b_p, b_s = x_prompt.shape[0], x_sample.shape[0]
    per_layer = (norm_mix_pre, w_in, conv_w, conv_b, w_rgate, b_rgate, w_igate, b_igate, lru_lambda,
                 pool_w, pool_b, pool_scale, norm_group_a, norm_group_b, w_out, norm_mix_post,
                 norm_ffn_pre, w_router, b_router, w_gate, b_gate, w_up, b_up, w_down, b_down,
                 norm_ffn_post, w_ple, w_ple_gate, b_ple_gate, norm_ple)
    hp, hs = x_prompt, _time_major(x_sample)
    collected = []
    for i in range(depth):
        hp, hs, states = _layer(hp, hs, p_prompt[i], _time_major(p_sample[i]),
                                state_rglru_h[i], state_rglru_conv[i], state_pool[i],
                                tuple(w[i] for w in per_layer))
        collected.append(states)
    stacked = tuple(jnp.stack([c[j] for c in collected]) for j in range(6))
    return (hp, _batch_major(hs, b_s)) + stacked
```

```python
import dataclasses
import functools

import jax
import jax.numpy as jnp
from jax import lax
from jax.experimental import pallas as pl
from jax.experimental.pallas import tpu as pltpu
from jax.experimental.pallas import tpu_sc as plsc

D_MODEL = 1024
W_A = 512
W_B = 512
N_HEADS_A = 8
HEAD_A = W_A // N_HEADS_A
CONV_W = 4
LRU_C = 8.0
POOL_WINDOWS = (2, 4, 8, 16)
GROUP_B = W_B // len(POOL_WINDOWS)
POOL_BUF = max(POOL_WINDOWS) - 1
N_EXPERTS = 32
TOP_K = 4
D_FF = 1024
SWIGLU_LIMIT = 7.0
SWIGLU_ALPHA = 1.702
PLE_DIM = 256
EPS = 1e-6
PAST_LEN = 16384

LANES = 128
MXU_DIM = 256
HALF = D_MODEL // 2
SC_CORES = 2
SC_SUBCORES = 16
SC_WORKERS = SC_CORES * SC_SUBCORES
SC_LANES = 16
SC_CHUNK = 64

MIX_ROWS = 512
ROUTE_ROWS = 512
FFN_ROWS = 512
FFN_SUB = 256
FIN_ROWS = 512
VMEM_LIMIT = 56 * 1024 * 1024

f32 = jnp.float32
bf16 = jnp.bfloat16
u32 = jnp.uint32
i32 = jnp.int32


def _rms(x, g):
    return x * lax.rsqrt(jnp.mean(x * x, axis=-1, keepdims=True) + EPS) * g


def _dot(a, b):
    return jnp.dot(a, b, preferred_element_type=f32)


def _pack_rows(x):
    bits = lax.bitcast_convert_type(x.astype(bf16).astype(f32), u32)
    return (bits[:, HALF:] & jnp.uint32(0xFFFF0000)) | (bits[:, :HALF] >> 16)


def _unpack_rows(p):
    lo = lax.bitcast_convert_type(p << 16, f32)
    hi = lax.bitcast_convert_type(p & jnp.uint32(0xFFFF0000), f32)
    return jnp.concatenate([lo, hi], axis=-1)


def _mixer_kernel(x_ref, conv0_ref, pool0_ref, h0_ref, g_pre_ref, w_in_ref, conv_w_ref, conv_b_ref,
                  w_gates_ref, br_ref, bi_ref, lam_ref,
                  pool_w_ref, pool_b_ref, pool_scale_ref, ng_a_ref, ng_b_ref, w_out_ref, g_post_ref,
                  h1_ref, hfin_ref, convfin_ref, poolfin_ref,
                  h_s, conv_s, pool_s, a_s, b_s, hs_s, *, bt, tt, start):
    step = pl.program_id(0)
    rows = bt * tt

    @pl.when(step == 0)
    def _():
        h_s[...] = h0_ref[...]
        conv_s[...] = conv0_ref[...]
        pool_s[...] = pool0_ref[...]

    if x_ref.ndim == 3:
        x = pltpu.einshape("btd->tbd", x_ref[...]).reshape(rows, D_MODEL)
    else:
        x = x_ref[...]
    u = _rms(x, g_pre_ref[...]).astype(bf16)
    z = _dot(u, w_in_ref[...])
    xa, ga, xb = z[:, :W_A], z[:, W_A:2 * W_A], z[:, 2 * W_A:]

    ext_a = jnp.concatenate([conv_s[...], xa], axis=0)
    xc = conv_b_ref[...]
    for k in range(CONV_W):
        xc = xc + ext_a[k * bt:k * bt + rows] * conv_w_ref[k:k + 1, :]
    conv_s[...] = ext_a[rows:]

    xc16 = xc.astype(bf16)
    gates = [_dot(xc16[:, j * MXU_DIM:(j + 1) * MXU_DIM], w_gates_ref[j]) for j in range(W_A // MXU_DIM)]
    r = jax.nn.sigmoid(jnp.concatenate([gj[:, :MXU_DIM] for gj in gates], axis=-1) + br_ref[...])
    ig = jax.nn.sigmoid(jnp.concatenate([gj[:, MXU_DIM:] for gj in gates], axis=-1) + bi_ref[...])
    lam = lam_ref[...]
    softplus_neg = jnp.maximum(-lam, 0.0) + jnp.log1p(jnp.exp(-jnp.abs(lam)))
    log_a = (-LRU_C) * r * softplus_neg
    a_s[...] = jnp.exp(log_a)
    th = jnp.tanh(log_a)
    b_s[...] = jnp.sqrt(-2.0 * th / (1.0 - th)) * (ig * xc)

    def scan_step(t, h):
        sl = pl.ds(pl.multiple_of(t * bt, bt), bt)
        h = a_s[sl, :] * h + b_s[sl, :]
        hs_s[sl, :] = h
        return h

    h_last = lax.fori_loop(0, tt, scan_step, h_s[...], unroll=True)
    h_s[...] = h_last
    ya = hs_s[...] * jax.nn.gelu(ga)

    ext_b = jnp.concatenate([pool_s[...], xb], axis=0)
    pool_s[...] = ext_b[rows:]
    s2 = ext_b[bt:, :] + ext_b[:-bt, :]
    s4 = s2[2 * bt:, GROUP_B:] + s2[:-2 * bt, GROUP_B:]
    s8 = s4[4 * bt:, GROUP_B:] + s4[:-4 * bt, GROUP_B:]
    s16 = s8[8 * bt:, GROUP_B:] + s8[:-8 * bt, GROUP_B:]
    wins = (s2[14 * bt:, :GROUP_B], s4[12 * bt:, :GROUP_B], s8[8 * bt:, :GROUP_B], s16)
    t_idx = lax.broadcasted_iota(i32, (rows, GROUP_B), 0) // bt
    pos1 = (t_idx + (step * tt + start + 1)).astype(f32)
    yb_parts = []
    for g, w in enumerate(POOL_WINDOWS):
        cnt = jnp.minimum(jnp.float32(w), pos1)
        d = wins[g] / cnt - xb[:, g * GROUP_B:(g + 1) * GROUP_B]
        yb_parts.append(_dot(d.astype(bf16), pool_w_ref[g]))
    yb = (jnp.concatenate(yb_parts, axis=-1) + pool_b_ref[...]) * pool_scale_ref[...]

    na = _rms(ya, ng_a_ref[...])
    nb = _rms(yb, ng_b_ref[...])
    m = _dot(jnp.concatenate([na, nb], axis=-1).astype(bf16), w_out_ref[...])
    h1_ref[...] = x + _rms(m, g_post_ref[...])

    hfin_ref[...] = h_last
    convfin_ref[...] = conv_s[...]
    poolfin_ref[...] = pool_s[...]


def _full(shape):
    return pl.BlockSpec(shape, lambda i, *_: (0,) * len(shape))


def _mixer(x, conv0, pool0, h0, wts, *, bt, start):
    tt = MIX_ROWS // bt
    rows = bt * tt
    if x.ndim == 3:
        assert x.shape[0] == bt and x.shape[1] % tt == 0
        n_steps = x.shape[1] // tt
        x_spec = pl.BlockSpec((bt, tt, D_MODEL), lambda i: (0, i, 0))
    else:
        assert x.shape[0] % rows == 0
        n_steps = x.shape[0] // rows
        x_spec = pl.BlockSpec((rows, D_MODEL), lambda i: (i, 0))
    kern = functools.partial(_mixer_kernel, bt=bt, tt=tt, start=start)
    small = [conv0, pool0, h0] + list(wts)
    return pl.pallas_call(
        kern,
        grid=(n_steps,),
        in_specs=[x_spec] + [_full(a.shape) for a in small],
        out_specs=[pl.BlockSpec((rows, D_MODEL), lambda i: (i, 0)),
                   _full((bt, W_A)), _full(((CONV_W - 1) * bt, W_A)), _full((POOL_BUF * bt, W_B))],
        out_shape=[jax.ShapeDtypeStruct((n_steps * rows, D_MODEL), f32),
                   jax.ShapeDtypeStruct((bt, W_A), f32),
                   jax.ShapeDtypeStruct(((CONV_W - 1) * bt, W_A), f32),
                   jax.ShapeDtypeStruct((POOL_BUF * bt, W_B), f32)],
        scratch_shapes=[pltpu.VMEM((bt, W_A), f32),
                        pltpu.VMEM(((CONV_W - 1) * bt, W_A), f32),
                        pltpu.VMEM((POOL_BUF * bt, W_B), f32),
                        pltpu.VMEM((rows, W_A), f32),
                        pltpu.VMEM((rows, W_A), f32),
                        pltpu.VMEM((rows, W_A), f32)],
        compiler_params=pltpu.CompilerParams(dimension_semantics=("arbitrary",),
                                             vmem_limit_bytes=VMEM_LIMIT),
        name="mixer",
    )(x, *small)


def _two_groups(step, first_steps, a_ref, b_ref):
    return jnp.where(step < first_steps, a_ref[...], b_ref[...])


def _router_kernel(h1a_ref, h1b_ref, g_ref, wr_ref, br_ref, tri_ref,
                   vpk_ref, idx_ref, rank_ref, pw_ref, cnt_ref, carry_s, *, first_steps):
    step = pl.program_id(0)
    rows = h1a_ref.shape[0]

    @pl.when(step == 0)
    def _():
        carry_s[...] = jnp.zeros_like(carry_s)

    v = _rms(_two_groups(step, first_steps, h1a_ref, h1b_ref), g_ref[...])
    vpk_ref[...] = _pack_rows(v)
    logits = _dot(v.astype(bf16), wr_ref[...]) + br_ref[...]
    lt = jnp.transpose(logits)[:N_EXPERTS, :]

    eio = lax.broadcasted_iota(i32, (N_EXPERTS, rows), 0).astype(f32)
    work = lt
    vals, idxs, sels = [], [], []
    for _ in range(TOP_K):
        m = jnp.max(work, axis=0, keepdims=True)
        ik = jnp.min(jnp.where(work == m, eio, float(N_EXPERTS)), axis=0, keepdims=True)
        sel = eio == ik
        vals.append(m)
        idxs.append(ik)
        sels.append(sel)
        work = jnp.where(sel, -jnp.inf, work)
    exps = [jnp.exp(val - vals[0]) for val in vals]
    denom = exps[0] + exps[1] + exps[2] + exps[3]
    probs = [e / denom for e in exps]

    chosen = sels[0] | sels[1] | sels[2] | sels[3]
    onehot = jnp.where(chosen, 1.0, 0.0)
    before = _dot(onehot.astype(bf16), tri_ref[...])
    base = carry_s[...] + before
    ranks = [jnp.sum(jnp.where(sel, base, 0.0), axis=0, keepdims=True) for sel in sels]
    carry_s[...] = carry_s[...] + jnp.sum(onehot, axis=1, keepdims=True)

    idx_ref[...] = jnp.concatenate(idxs, axis=0).astype(i32)
    rank_ref[...] = jnp.concatenate(ranks, axis=0).astype(i32)
    pad = jnp.zeros((LANES - TOP_K, rows), f32)
    pw_ref[...] = jnp.transpose(jnp.concatenate(probs + [pad], axis=0))
    cnt_ref[...] = carry_s[:, :LANES].astype(i32)


def _group_specs(rows, width, na, nb):
    first = na // rows
    return (pl.BlockSpec((rows, width), lambda i: (jnp.minimum(i, first - 1), 0)),
            pl.BlockSpec((rows, width), lambda i: (jnp.maximum(i - first, 0), 0)))


def _router(h1a, h1b, g, wr, br, tri):
    na, nb = h1a.shape[0], h1b.shape[0]
    t = na + nb
    rows = ROUTE_ROWS
    assert na % rows == 0 and nb % rows == 0
    return pl.pallas_call(
        functools.partial(_router_kernel, first_steps=na // rows),
        grid=(t // rows,),
        in_specs=[*_group_specs(rows, D_MODEL, na, nb),
                  _full(g.shape), _full(wr.shape), _full(br.shape), _full(tri.shape)],
        out_specs=[pl.BlockSpec((rows, HALF), lambda i: (i, 0)),
                   pl.BlockSpec((TOP_K, rows), lambda i: (0, i)),
                   pl.BlockSpec((TOP_K, rows), lambda i: (0, i)),
                   pl.BlockSpec((rows, LANES), lambda i: (i, 0)),
                   _full((N_EXPERTS, LANES))],
        out_shape=[jax.ShapeDtypeStruct((t, HALF), u32),
                   jax.ShapeDtypeStruct((TOP_K, t), i32),
                   jax.ShapeDtypeStruct((TOP_K, t), i32),
                   jax.ShapeDtypeStruct((t, LANES), f32),
                   jax.ShapeDtypeStruct((N_EXPERTS, LANES), i32)],
        scratch_shapes=[pltpu.VMEM((N_EXPERTS, rows), f32)],
        compiler_params=pltpu.CompilerParams(dimension_semantics=("arbitrary",),
                                             vmem_limit_bytes=VMEM_LIMIT),
        name="router",
    )(h1a, h1b, g, wr, br, tri)


def _sc_mesh():
    return plsc.VectorSubcoreMesh(core_axis_name="core", subcore_axis_name="subcore")


def _sc_params():
    return dataclasses.replace(pltpu.CompilerParams(), needs_layout_passes=False)


def _sc_worker():
    return lax.axis_index("subcore") * SC_CORES + lax.axis_index("core")


def _sc_positions(idx_v, rank_v, off_v, n):
    @pl.loop(0, n // SC_LANES)
    def _(i):
        sl = pl.ds(pl.multiple_of(i * SC_LANES, SC_LANES), SC_LANES)
        idx_v[sl] = plsc.load_gather(off_v, [idx_v[sl]]) + rank_v[sl]


def _sc_copy_rows(src_hbm, idx_v, dst_hbm, dst_row0, n, rows_v, gsem, wsem):
    n_ch = n // SC_CHUNK
    n_pair = n_ch // 2

    def gather(c, b):
        o = pl.multiple_of(c * SC_CHUNK, SC_CHUNK)
        return pltpu.make_async_copy(src_hbm.at[idx_v.at[pl.ds(o, SC_CHUNK)]], rows_v.at[b], gsem.at[b])

    def write(c, b):
        o = pl.multiple_of(c * SC_CHUNK, SC_CHUNK)
        return pltpu.make_async_copy(rows_v.at[b], dst_hbm.at[pl.ds(dst_row0 + o, SC_CHUNK)], wsem.at[b])

    gather(0, 0).start()

    @pl.loop(0, n_pair)
    def _(i):
        c0 = 2 * i
        gather(c0, 0).wait()

        @pl.when(i > 0)
        def _():
            write(c0 - 1, 1).wait()

        gather(c0 + 1, 1).start()
        write(c0, 0).start()
        gather(c0 + 1, 1).wait()
        write(c0, 0).wait()

        @pl.when(c0 + 2 < n_ch)
        def _():
            gather(c0 + 2, 0).start()

        write(c0 + 1, 1).start()

    if n_ch % 2:
        gather(n_ch - 1, 0).wait()
        write(n_ch - 1, 0).start()
        write(n_ch - 1, 0).wait()
    if n_pair:
        write(2 * n_pair - 1, 1).wait()


def _sc_dispatch(vpk, idx_flat, rank_flat, off, row0, n_out):
    n_tok = vpk.shape[0]
    per_w = n_out // SC_WORKERS
    assert n_out % (SC_WORKERS * SC_CHUNK) == 0 and n_tok % SC_LANES == 0 and row0 % SC_LANES == 0

    @pl.kernel(out_type=jax.ShapeDtypeStruct((n_out, HALF), u32), mesh=_sc_mesh(),
               compiler_params=_sc_params(), name="dispatch",
               scratch_types=[pltpu.VMEM((per_w,), i32), pltpu.VMEM((n_tok,), i32), pltpu.VMEM((n_tok,), i32),
                              pltpu.VMEM((N_EXPERTS,), i32), pltpu.VMEM((2, SC_CHUNK, HALF), u32),
                              pltpu.SemaphoreType.DMA((2,)), pltpu.SemaphoreType.DMA((2,))])
    def k(v_hbm, i_hbm, r_hbm, off_hbm, o_hbm, src_v, idx_v, rank_v, off_v, rows_v, gsem, wsem):
        out0 = _sc_worker() * per_w
        lo = row0 + out0
        pltpu.sync_copy(off_hbm, off_v)

        @pl.loop(0, per_w // SC_LANES)
        def _(i):
            o = pl.multiple_of(i * SC_LANES, SC_LANES)
            src_v[pl.ds(o, SC_LANES)] = lax.iota(i32, SC_LANES) + lax.rem(lo + o, n_tok)

        @pl.loop(0, TOP_K)
        def _(kk):
            pltpu.sync_copy(i_hbm.at[pl.ds(kk * n_tok, n_tok)], idx_v)
            pltpu.sync_copy(r_hbm.at[pl.ds(kk * n_tok, n_tok)], rank_v)

            @plsc.parallel_loop(0, n_tok, step=SC_LANES, unroll=8)
            def _(o):
                sl = pl.ds(pl.multiple_of(o, SC_LANES), SC_LANES)
                p = plsc.load_gather(off_v, [idx_v[sl]]) + rank_v[sl] - lo
                mine = (p >= 0) & (p < per_w)
                tok = lax.iota(i32, SC_LANES) + o
                plsc.store_scatter(src_v, [jnp.where(mine, p, 0)], tok, mask=mine)

        _sc_copy_rows(v_hbm, src_v, o_hbm, out0, per_w, rows_v, gsem, wsem)

    return k(vpk, idx_flat, rank_flat, off)


def _sc_collect(ys, idx_flat, rank_flat, off, t0, n):
    n_tok = idx_flat.shape[0] // TOP_K
    per_w = TOP_K * n // SC_WORKERS
    per_k = SC_WORKERS // TOP_K
    assert (TOP_K * n) % (SC_WORKERS * SC_CHUNK) == 0 and t0 % 8 == 0

    @pl.kernel(out_type=jax.ShapeDtypeStruct((TOP_K * n, HALF), u32), mesh=_sc_mesh(),
               compiler_params=_sc_params(), name="collect",
               scratch_types=[pltpu.VMEM((per_w,), i32), pltpu.VMEM((per_w,), i32),
                              pltpu.VMEM((N_EXPERTS,), i32), pltpu.VMEM((2, SC_CHUNK, HALF), u32),
                              pltpu.SemaphoreType.DMA((2,)), pltpu.SemaphoreType.DMA((2,))])
    def k(y_hbm, i_hbm, r_hbm, off_hbm, o_hbm, idx_v, rank_v, off_v, rows_v, gsem, wsem):
        w = _sc_worker()
        src = lax.div(w, per_k) * n_tok + t0 + lax.rem(w, per_k) * per_w
        pltpu.sync_copy(off_hbm, off_v)
        pltpu.sync_copy(i_hbm.at[pl.ds(pl.multiple_of(src, 8), per_w)], idx_v)
        pltpu.sync_copy(r_hbm.at[pl.ds(pl.multiple_of(src, 8), per_w)], rank_v)
        _sc_positions(idx_v, rank_v, off_v, per_w)
        _sc_copy_rows(y_hbm, idx_v, o_hbm, w * per_w, per_w, rows_v, gsem, wsem)

    return k(ys, idx_flat, rank_flat, off)


def _ffn_kernel(te_ref, tf_ref, slot_ref, next_ref, rows_ref, nv_ref, xs_ref, wg_hbm, bg_ref, wu_hbm, bu_ref, wd_hbm, bd_ref,
                *rest):
    ys_ref, wbuf, wg_s, wu_s, wd_s, wsem = rest[-6:]
    step = pl.program_id(0)
    w_hbm = (wg_hbm, wu_hbm, wd_hbm)
    w_bf16 = (wg_s, wu_s, wd_s)

    def fetch(e, slot):
        return [pltpu.make_async_copy(w_hbm[m].at[e], wbuf.at[slot, m], wsem.at[slot, m]) for m in range(3)]

    @pl.when(step < nv_ref[0])
    def _():
        @pl.when(tf_ref[step] == 1)
        def _():
            slot = slot_ref[step]

            @pl.when(step == 0)
            def _():
                for cp in fetch(te_ref[0], slot):
                    cp.start()

            for cp in fetch(te_ref[step], slot):
                cp.wait()
            for s in range(2):
                @pl.when(slot == s)
                def _():
                    for m in range(3):
                        w_bf16[m][...] = wbuf[s, m].astype(bf16)

            @pl.when(next_ref[step] >= 0)
            def _():
                for cp in fetch(next_ref[step], 1 - slot):
                    cp.start()

        def rows_block(rb):
            rs = slice(rb * FFN_SUB, (rb + 1) * FFN_SUB)
            x = _unpack_rows(xs_ref[rs, :]).astype(bf16)
            g = jnp.minimum(_dot(x, wg_s[...]) + bg_ref[0], SWIGLU_LIMIT)
            u = jnp.clip(_dot(x, wu_s[...]) + bu_ref[0], -SWIGLU_LIMIT, SWIGLU_LIMIT)
            hid = (u + 1.0) * (g * jax.nn.sigmoid(SWIGLU_ALPHA * g))
            y = _dot(hid.astype(bf16), wd_s[...]) + bd_ref[0]
            ys_ref[rs, :] = _pack_rows(y)

        @pl.when(rows_ref[step] > FFN_SUB)
        def _():
            rows_block(0)
            rows_block(1)

        @pl.when(rows_ref[step] <= FFN_SUB)
        def _():
            rows_block(0)
            ys_ref[FFN_SUB:, :] = jnp.zeros((FFN_ROWS - FFN_SUB, HALF), u32)

    @pl.when(step >= nv_ref[0])
    def _():
        ys_ref[...] = jnp.zeros_like(ys_ref)


def _ffn(tables, xs, n_rows_total, tile0, w_gate, b_gate, w_up, b_up, w_down, b_down, prev=None):
    n_tiles = xs.shape[0] // FFN_ROWS
    assert D_FF == D_MODEL
    assert FFN_ROWS == 2 * FFN_SUB

    def row_map(i, te, tf, sl, nx, rw, nv):
        return (jnp.minimum(i, jnp.maximum(nv[0] - 1, 0)), 0)

    def b_map(i, te, tf, sl, nx, rw, nv):
        return (te[i], 0, 0)

    w_spec = pl.BlockSpec(memory_space=pl.ANY)
    b_spec = pl.BlockSpec((1, 1, D_FF), b_map)
    in_specs = [pl.BlockSpec((FFN_ROWS, HALF), row_map), w_spec, b_spec, w_spec, b_spec, w_spec, b_spec]
    args = [xs, w_gate, b_gate.reshape(N_EXPERTS, 1, D_FF), w_up, b_up.reshape(N_EXPERTS, 1, D_FF),
            w_down, b_down.reshape(N_EXPERTS, 1, D_MODEL)]
    aliases = {}
    if prev is not None:
        in_specs.append(pl.BlockSpec(memory_space=pl.ANY))
        aliases = {len(tables) + len(args): 0}
        args.append(prev)
    return pl.pallas_call(
        _ffn_kernel,
        grid_spec=pltpu.PrefetchScalarGridSpec(
            num_scalar_prefetch=6,
            grid=(n_tiles,),
            in_specs=in_specs,
            out_specs=pl.BlockSpec((FFN_ROWS, HALF), lambda i, *_: (tile0 + i, 0)),
            scratch_shapes=[pltpu.VMEM((2, 3, D_MODEL, D_FF), f32),
                            pltpu.VMEM((D_MODEL, D_FF), bf16),
                            pltpu.VMEM((D_MODEL, D_FF), bf16),
                            pltpu.VMEM((D_FF, D_MODEL), bf16),
                            pltpu.SemaphoreType.DMA((2, 3))]),
        out_shape=jax.ShapeDtypeStruct((n_rows_total, HALF), u32),
        input_output_aliases=aliases,
        compiler_params=pltpu.CompilerParams(dimension_semantics=("arbitrary",),
                                             vmem_limit_bytes=VMEM_LIMIT),
        name="experts",
    )(*tables, *args)


def _finish_kernel(g_ref, h1a_ref, h1b_ref, pw_ref, pa_ref, pb_ref, g_post_ref, w_ple_ref, w_pg_ref, b_pg_ref,
                   g_ple_ref, *rest, first_steps):
    outa_ref, outb_ref = rest[-2:]
    step = pl.program_id(0)
    rows = h1a_ref.shape[0]
    pw = pw_ref[...]
    f = jnp.zeros((rows, D_MODEL), f32)
    for k in range(TOP_K):
        f = f + pw[:, k:k + 1] * _unpack_rows(g_ref[k])
    h2 = _two_groups(step, first_steps, h1a_ref, h1b_ref) + _rms(f, g_post_ref[...])
    gate = jax.nn.sigmoid(_dot(h2.astype(bf16), w_pg_ref[...]) + b_pg_ref[...])
    bt, tt = pa_ref.shape[0], pa_ref.shape[1]
    pin_a = pltpu.einshape("btd->tbd", pa_ref[...]).reshape(rows, PLE_DIM)
    pin = jnp.where(step < first_steps, pin_a, pb_ref[...]).astype(bf16)
    pe = _dot(pin, w_ple_ref[...]) * gate
    out = h2 + _rms(pe, g_ple_ref[...])

    @pl.when(step < first_steps)
    def _():
        outa_ref[...] = pltpu.einshape("tbd->btd", out.reshape(tt, bt, D_MODEL))

    @pl.when(step >= first_steps)
    def _():
        outb_ref[...] = out


def _finish(g, h1a, h1b, pw, pa, pb, g_post, w_ple, w_pg, b_pg, g_ple, *, a0, a_steps, b_steps, prev=None):
    na, nb = h1a.shape[0], h1b.shape[0]
    rows = FIN_ROWS
    bt = pa.shape[0]
    tt = rows // bt
    assert na % rows == 0 and nb % rows == 0 and a_steps >= 1 and rows % bt == 0

    def spec_a(width):
        return pl.BlockSpec((rows, width), lambda i: (a0 + jnp.minimum(i, a_steps - 1), 0))

    def spec_a3(width):
        return pl.BlockSpec((bt, tt, width), lambda i: (0, a0 + jnp.minimum(i, a_steps - 1), 0))

    def spec_b(width):
        return pl.BlockSpec((rows, width), lambda i: (jnp.maximum(i - a_steps, 0), 0))

    in_specs = [pl.BlockSpec((TOP_K, rows, HALF), lambda i: (0, i, 0)),
                spec_a(D_MODEL), spec_b(D_MODEL),
                pl.BlockSpec((rows, LANES), lambda i: (a0 + i, 0)),
                spec_a3(PLE_DIM), spec_b(PLE_DIM),
                _full(g_post.shape), _full(w_ple.shape), _full(w_pg.shape),
                _full(b_pg.shape), _full(g_ple.shape)]
    args = [g, h1a, h1b, pw, pa, pb, g_post, w_ple, w_pg, b_pg, g_ple]
    aliases = {}
    if prev is not None:
        in_specs.append(pl.BlockSpec(memory_space=pl.ANY))
        aliases = {len(args): 0}
        args.append(prev)
    return pl.pallas_call(
        functools.partial(_finish_kernel, first_steps=a_steps),
        grid=(a_steps + b_steps,),
        in_specs=in_specs,
        out_specs=[spec_a3(D_MODEL), spec_b(D_MODEL)],
        out_shape=[jax.ShapeDtypeStruct((bt, na // bt, D_MODEL), f32), jax.ShapeDtypeStruct((nb, D_MODEL), f32)],
        input_output_aliases=aliases,
        compiler_params=pltpu.CompilerParams(dimension_semantics=("arbitrary",),
                                             vmem_limit_bytes=VMEM_LIMIT),
        name="finish",
    )(*args)


def _block_diag_pairs(w):
    per = MXU_DIM // HEAD_A
    w4 = w.reshape(N_HEADS_A // per, per, HEAD_A, HEAD_A)
    blocks = jnp.where(jnp.eye(per, dtype=bool)[None, :, None, :, None], w4[:, :, :, None, :], 0.0)
    return blocks.reshape(N_HEADS_A // per, MXU_DIM, MXU_DIM)


def _time_major(x):
    x = jnp.swapaxes(x, 0, 1)
    return x.reshape((x.shape[0] * x.shape[1],) + x.shape[2:])


def _batch_major(x, b):
    return jnp.swapaxes(x.reshape(x.shape[0] // b, b, x.shape[1]), 0, 1)


def _routing_tables(counts, n_tiles):
    tiles = (counts + FFN_ROWS - 1) // FFN_ROWS
    tile_ends = jnp.cumsum(tiles)
    off = (tile_ends - tiles) * FFN_ROWS
    n_valid = tile_ends[-1:]
    tile_ids = jnp.minimum(jnp.arange(n_tiles, dtype=i32), n_valid - 1)
    tile_e = jnp.sum((tile_ids[:, None] >= tile_ends[None, :]).astype(i32), axis=1)
    cand = jnp.where(tiles > 0, jnp.arange(N_EXPERTS, dtype=i32), N_EXPERTS)
    suffix_min = lax.cummin(cand[::-1])[::-1]
    nxt = jnp.concatenate([suffix_min[1:], jnp.full((1,), N_EXPERTS, i32)])
    onehot = (tile_e[:, None] == jnp.arange(N_EXPERTS, dtype=i32)[None, :]).astype(i32)
    at_tile = lambda per_expert: jnp.sum(onehot * per_expert[None, :].astype(i32), axis=1)
    tile_next = at_tile(jnp.where(nxt < N_EXPERTS, nxt, -1))
    tile_rows = jnp.clip(at_tile(counts) - (tile_ids - at_tile(tile_ends - tiles)) * FFN_ROWS, 0, FFN_ROWS)
    tile_group_end = at_tile(tile_ends)
    return off.astype(i32), (tile_e.astype(i32), tile_next.astype(i32), tile_group_end.astype(i32),
                             tile_rows.astype(i32), n_valid.astype(i32))


def _tile_tables(full, tile0, n):
    tile_e, tile_next, tile_group_end, tile_rows = (a[tile0:tile0 + n] for a in full[:4])
    n_valid = full[4]
    first = jnp.concatenate([jnp.ones((1,), i32), (tile_e[1:] != tile_e[:-1]).astype(i32)])
    slot = (jnp.cumsum(first) - 1) % 2
    nxt = jnp.where(tile_group_end < tile0 + n, tile_next, -1)
    return (tile_e, first, slot.astype(i32), nxt.astype(i32), tile_rows, jnp.clip(n_valid - tile0, 0, n))


def _layer(xp, xs_tm, pp, ps_tm, state_h, state_conv, state_pool, lw):
    (norm_mix_pre, w_in, conv_w, conv_b, w_rgate, b_rgate, w_igate, b_igate, lru_lambda,
     pool_w, pool_b, pool_scale, norm_group_a, norm_group_b, w_out, norm_mix_post,
     norm_ffn_pre, w_router, b_router, w_gate, b_gate, w_up, b_up, w_down, b_down, norm_ffn_post,
     w_ple, w_ple_gate, b_ple_gate, norm_ple) = lw
    b_p, s_p = xp.shape[0], xp.shape[1]
    n_p, n_s = b_p * s_p, xs_tm.shape[0]
    b_s = state_h.shape[0]
    row = lambda a: a.reshape(1, -1)

    w_gates = jnp.concatenate([_block_diag_pairs(w_rgate), _block_diag_pairs(w_igate)], axis=-1).astype(bf16)
    mix_w = (row(norm_mix_pre), w_in.astype(bf16), conv_w, row(conv_b),
             w_gates, row(b_rgate), row(b_igate), row(lru_lambda),
             pool_w.astype(bf16), row(pool_b), row(pool_scale), row(norm_group_a), row(norm_group_b),
             w_out.astype(bf16), row(norm_mix_post))

    zeros = lambda *s: jnp.zeros(s, f32)
    h1_p, hfin_p, cfin_p, pfin_p = _mixer(
        xp, zeros((CONV_W - 1) * b_p, W_A), zeros(POOL_BUF * b_p, W_B), zeros(b_p, W_A),
        mix_w, bt=b_p, start=0)
    h1_s, hfin_s, cfin_s, pfin_s = _mixer(
        xs_tm, _time_major(state_conv), _time_major(state_pool), state_h,
        mix_w, bt=b_s, start=PAST_LEN)
    t = n_p + n_s

    w_router_pad = jnp.zeros((D_MODEL, LANES), f32).at[:, :N_EXPERTS].set(w_router)
    b_router_pad = jnp.zeros((1, LANES), f32).at[0, :N_EXPERTS].set(b_router)
    tri = jnp.triu(jnp.ones((ROUTE_ROWS, ROUTE_ROWS), bf16), k=1)
    vpk, idx_t, rank_t, pw, cnt = _router(h1_p, h1_s, row(norm_ffn_pre), w_router_pad.astype(bf16),
                                          b_router_pad, tri)

    n_tiles = (t * TOP_K) // FFN_ROWS + N_EXPERTS
    off, tables = _routing_tables(cnt[:, 0], n_tiles)
    idx_flat, rank_flat = idx_t.reshape(-1), rank_t.reshape(-1)

    step_tiles = SC_WORKERS * SC_CHUNK // FFN_ROWS
    tiles_1 = (n_tiles // 2) // step_tiles * step_tiles
    ys = None
    for tile0, n in ((0, tiles_1), (tiles_1, n_tiles - tiles_1)):
        xs = _sc_dispatch(vpk, idx_flat, rank_flat, off, tile0 * FFN_ROWS, n * FFN_ROWS)
        ys = _ffn(_tile_tables(tables, tile0, n), xs, n_tiles * FFN_ROWS, tile0,
                  w_gate, b_gate, w_up, b_up, w_down, b_down, prev=ys)
    fin_w = (row(norm_ffn_post), w_ple.astype(bf16), w_ple_gate.astype(bf16), row(b_ple_gate), row(norm_ple))
    a_blocks, b_blocks = n_p // FIN_ROWS, n_s // FIN_ROWS
    first = a_blocks // 2
    n1, n2 = first * FIN_ROWS, t - first * FIN_ROWS
    g1 = _sc_collect(ys, idx_flat, rank_flat, off, 0, n1).reshape(TOP_K, n1, HALF)
    out_p, _ = _finish(g1, h1_p, h1_s, pw, pp, ps_tm, *fin_w, a0=0, a_steps=first, b_steps=0)
    g2 = _sc_collect(ys, idx_flat, rank_flat, off, n1, n2).reshape(TOP_K, n2, HALF)
    out_p, out_s = _finish(g2, h1_p, h1_s, pw, pp, ps_tm, *fin_w, a0=first, a_steps=a_blocks - first,
                           b_steps=b_blocks, prev=out_p)

    states = (hfin_p, _batch_major(cfin_p, b_p), _batch_major(pfin_p, b_p),
              hfin_s, _batch_major(cfin_s, b_s), _batch_major(pfin_s, b_s))
    return out_p, out_s, states


def kernel(x_prompt, x_sample, state_rglru_h, state_rglru_conv, state_pool, p_prompt, p_sample, norm_mix_pre, w_in, conv_w, conv_b, w_rgate, b_rgate, w_igate, b_igate, lru_lambda, pool_w, pool_b, pool_scale, norm_group_a, norm_group_b, w_out, norm_mix_post, norm_ffn_pre, w_router, b_router, w_gate, b_gate, w_up, b_up, w_down, b_down, norm_ffn_post, w_ple, w_ple_gate, b_ple_gate, norm_ple):
    depth = w_in.shape[0]
    b_p, b_s = x_prompt.shape[0], x_sample.shape[0]
    per_layer = (norm_mix_pre, w_in, conv_w, conv_b, w_rgate, b_rgate, w_igate, b_igate, lru_lambda,
                 pool_w, pool_b, pool_scale, norm_group_a, norm_group_b, w_out, norm_mix_post,
                 norm_ffn_pre, w_router, b_router, w_gate, b_gate, w_up, b_up, w_down, b_down,
                 norm_ffn_post, w_ple, w_ple_gate, b_ple_gate, norm_ple)
    hp, hs = x_prompt, _time_major(x_sample)
    collected = []
    for i in range(depth):
        hp, hs, states = _layer(hp, hs, p_prompt[i], _time_major(p_sample[i]),
                                state_rglru_h[i], state_rglru_conv[i], state_pool[i],
                                tuple(w[i] for w in per_layer))
        collected.append(states)
    stacked = tuple(jnp.stack([c[j] for c in collected]) for j in range(6))
    return (hp, _batch_major(hs, b_s)) + stacked
```

```python
import dataclasses
import functools

import jax
import jax.numpy as jnp
from jax import lax
from jax.experimental import pallas as pl
from jax.experimental.pallas import tpu as pltpu
from jax.experimental.pallas import tpu_sc as plsc

D_MODEL = 1024
W_A = 512
W_B = 512
N_HEADS_A = 8
HEAD_A = W_A // N_HEADS_A
CONV_W = 4
LRU_C = 8.0
POOL_WINDOWS = (2, 4, 8, 16)
GROUP_B = W_B // len(POOL_WINDOWS)
POOL_BUF = max(POOL_WINDOWS) - 1
N_EXPERTS = 32
TOP_K = 4
D_FF = 1024
SWIGLU_LIMIT = 7.0
SWIGLU_ALPHA = 1.702
PLE_DIM = 256
EPS = 1e-6
PAST_LEN = 16384

LANES = 128
MXU_DIM = 256
HALF = D_MODEL // 2
SC_CORES = 2
SC_SUBCORES = 16
SC_WORKERS = SC_CORES * SC_SUBCORES
SC_LANES = 16
SC_CHUNK = 64

MIX_ROWS = 512
ROUTE_ROWS = 512
FFN_ROWS = 512
FFN_SUB = 256
FIN_ROWS = 512
VMEM_LIMIT = 56 * 1024 * 1024

f32 = jnp.float32
bf16 = jnp.bfloat16
u32 = jnp.uint32
i32 = jnp.int32


def _rms(x, g):
    return x * lax.rsqrt(jnp.mean(x * x, axis=-1, keepdims=True) + EPS) * g


def _dot(a, b):
    return jnp.dot(a, b, preferred_element_type=f32)


def _pack_rows(x):
    bits = lax.bitcast_convert_type(x.astype(bf16).astype(f32), u32)
    return (bits[:, HALF:] & jnp.uint32(0xFFFF0000)) | (bits[:, :HALF] >> 16)


def _unpack_rows(p):
    lo = lax.bitcast_convert_type(p << 16, f32)
    hi = lax.bitcast_convert_type(p & jnp.uint32(0xFFFF0000), f32)
    return jnp.concatenate([lo, hi], axis=-1)


def _mixer_kernel(x_ref, conv0_ref, pool0_ref, h0_ref, g_pre_ref, w_in_ref, conv_w_ref, conv_b_ref,
                  w_gates_ref, br_ref, bi_ref, lam_ref,
                  pool_w_ref, pool_b_ref, pool_scale_ref, ng_a_ref, ng_b_ref, w_out_ref, g_post_ref,
                  h1_ref, hfin_ref, convfin_ref, poolfin_ref,
                  h_s, conv_s, pool_s, a_s, b_s, hs_s, *, bt, tt, start):
    step = pl.program_id(0)
    rows = bt * tt

    @pl.when(step == 0)
    def _():
        h_s[...] = h0_ref[...]
        conv_s[...] = conv0_ref[...]
        pool_s[...] = pool0_ref[...]

    if x_ref.ndim == 3:
        x = pltpu.einshape("btd->tbd", x_ref[...]).reshape(rows, D_MODEL)
    else:
        x = x_ref[...]
    u = _rms(x, g_pre_ref[...]).astype(bf16)
    z = _dot(u, w_in_ref[...])
    xa, ga, xb = z[:, :W_A], z[:, W_A:2 * W_A], z[:, 2 * W_A:]

    ext_a = jnp.concatenate([conv_s[...], xa], axis=0)
    xc = conv_b_ref[...]
    for k in range(CONV_W):
        xc = xc + ext_a[k * bt:k * bt + rows] * conv_w_ref[k:k + 1, :]
    conv_s[...] = ext_a[rows:]

    xc16 = xc.astype(bf16)
    gates = [_dot(xc16[:, j * MXU_DIM:(j + 1) * MXU_DIM], w_gates_ref[j]) for j in range(W_A // MXU_DIM)]
    r = jax.nn.sigmoid(jnp.concatenate([gj[:, :MXU_DIM] for gj in gates], axis=-1) + br_ref[...])
    ig = jax.nn.sigmoid(jnp.concatenate([gj[:, MXU_DIM:] for gj in gates], axis=-1) + bi_ref[...])
    lam = lam_ref[...]
    softplus_neg = jnp.maximum(-lam, 0.0) + jnp.log1p(jnp.exp(-jnp.abs(lam)))
    log_a = (-LRU_C) * r * softplus_neg
    a_s[...] = jnp.exp(log_a)
    th = jnp.tanh(log_a)
    b_s[...] = jnp.sqrt(-2.0 * th / (1.0 - th)) * (ig * xc)

    def scan_step(t, h):
        sl = pl.ds(pl.multiple_of(t * bt, bt), bt)
        h = a_s[sl, :] * h + b_s[sl, :]
        hs_s[sl, :] = h
        return h

    h_last = lax.fori_loop(0, tt, scan_step, h_s[...], unroll=True)
    h_s[...] = h_last
    ya = hs_s[...] * jax.nn.gelu(ga)

    ext_b = jnp.concatenate([pool_s[...], xb], axis=0)
    pool_s[...] = ext_b[rows:]
    s2 = ext_b[bt:, :] + ext_b[:-bt, :]
    s4 = s2[2 * bt:, GROUP_B:] + s2[:-2 * bt, GROUP_B:]
    s8 = s4[4 * bt:, GROUP_B:] + s4[:-4 * bt, GROUP_B:]
    s16 = s8[8 * bt:, GROUP_B:] + s8[:-8 * bt, GROUP_B:]
    wins = (s2[14 * bt:, :GROUP_B], s4[12 * bt:, :GROUP_B], s8[8 * bt:, :GROUP_B], s16)
    t_idx = lax.broadcasted_iota(i32, (rows, GROUP_B), 0) // bt
    pos1 = (t_idx + (step * tt + start + 1)).astype(f32)
    yb_parts = []
    for g, w in enumerate(POOL_WINDOWS):
        cnt = jnp.minimum(jnp.float32(w), pos1)
        d = wins[g] / cnt - xb[:, g * GROUP_B:(g + 1) * GROUP_B]
        yb_parts.append(_dot(d.astype(bf16), pool_w_ref[g]))
    yb = (jnp.concatenate(yb_parts, axis=-1) + pool_b_ref[...]) * pool_scale_ref[...]

    na = _rms(ya, ng_a_ref[...])
    nb = _rms(yb, ng_b_ref[...])
    m = _dot(jnp.concatenate([na, nb], axis=-1).astype(bf16), w_out_ref[...])
    h1_ref[...] = x + _rms(m, g_post_ref[...])

    hfin_ref[...] = h_last
    convfin_ref[...] = conv_s[...]
    poolfin_ref[...] = pool_s[...]


def _full(shape):
    return pl.BlockSpec(shape, lambda i, *_: (0,) * len(shape))


def _mixer(x, conv0, pool0, h0, wts, *, bt, start):
    tt = MIX_ROWS // bt
    rows = bt * tt
    if x.ndim == 3:
        assert x.shape[0] == bt and x.shape[1] % tt == 0
        n_steps = x.shape[1] // tt
        x_spec = pl.BlockSpec((bt, tt, D_MODEL), lambda i: (0, i, 0))
    else:
        assert x.shape[0] % rows == 0
        n_steps = x.shape[0] // rows
        x_spec = pl.BlockSpec((rows, D_MODEL), lambda i: (i, 0))
    kern = functools.partial(_mixer_kernel, bt=bt, tt=tt, start=start)
    small = [conv0, pool0, h0] + list(wts)
    return pl.pallas_call(
        kern,
        grid=(n_steps,),
        in_specs=[x_spec] + [_full(a.shape) for a in small],
        out_specs=[pl.BlockSpec((rows, D_MODEL), lambda i: (i, 0)),
                   _full((bt, W_A)), _full(((CONV_W - 1) * bt, W_A)), _full((POOL_BUF * bt, W_B))],
        out_shape=[jax.ShapeDtypeStruct((n_steps * rows, D_MODEL), f32),
                   jax.ShapeDtypeStruct((bt, W_A), f32),
                   jax.ShapeDtypeStruct(((CONV_W - 1) * bt, W_A), f32),
                   jax.ShapeDtypeStruct((POOL_BUF * bt, W_B), f32)],
        scratch_shapes=[pltpu.VMEM((bt, W_A), f32),
                        pltpu.VMEM(((CONV_W - 1) * bt, W_A), f32),
                        pltpu.VMEM((POOL_BUF * bt, W_B), f32),
                        pltpu.VMEM((rows, W_A), f32),
                        pltpu.VMEM((rows, W_A), f32),
                        pltpu.VMEM((rows, W_A), f32)],
        compiler_params=pltpu.CompilerParams(dimension_semantics=("arbitrary",),
                                             vmem_limit_bytes=VMEM_LIMIT),
        name="mixer",
    )(x, *small)


def _two_groups(step, first_steps, a_ref, b_ref):
    return jnp.where(step < first_steps, a_ref[...], b_ref[...])


def _router_kernel(h1a_ref, h1b_ref, g_ref, wr_ref, br_ref, tri_ref,
                   vpk_ref, idx_ref, rank_ref, pw_ref, cnt_ref, carry_s, *, first_steps):
    step = pl.program_id(0)
    rows = h1a_ref.shape[0]

    @pl.when(step == 0)
    def _():
        carry_s[...] = jnp.zeros_like(carry_s)

    v = _rms(_two_groups(step, first_steps, h1a_ref, h1b_ref), g_ref[...])
    vpk_ref[...] = _pack_rows(v)
    logits = _dot(v.astype(bf16), wr_ref[...]) + br_ref[...]
    lt = jnp.transpose(logits)[:N_EXPERTS, :]

    eio = lax.broadcasted_iota(i32, (N_EXPERTS, rows), 0).astype(f32)
    work = lt
    vals, idxs, sels = [], [], []
    for _ in range(TOP_K):
        m = jnp.max(work, axis=0, keepdims=True)
        ik = jnp.min(jnp.where(work == m, eio, float(N_EXPERTS)), axis=0, keepdims=True)
        sel = eio == ik
        vals.append(m)
        idxs.append(ik)
        sels.append(sel)
        work = jnp.where(sel, -jnp.inf, work)
    exps = [jnp.exp(val - vals[0]) for val in vals]
    denom = exps[0] + exps[1] + exps[2] + exps[3]
    probs = [e / denom for e in exps]

    chosen = sels[0] | sels[1] | sels[2] | sels[3]
    onehot = jnp.where(chosen, 1.0, 0.0)
    before = _dot(onehot.astype(bf16), tri_ref[...])
    base = carry_s[...] + before
    ranks = [jnp.sum(jnp.where(sel, base, 0.0), axis=0, keepdims=True) for sel in sels]
    carry_s[...] = carry_s[...] + jnp.sum(onehot, axis=1, keepdims=True)

    idx_ref[...] = jnp.concatenate(idxs, axis=0).astype(i32)
    rank_ref[...] = jnp.concatenate(ranks, axis=0).astype(i32)
    pad = jnp.zeros((LANES - TOP_K, rows), f32)
    pw_ref[...] = jnp.transpose(jnp.concatenate(probs + [pad], axis=0))
    cnt_ref[...] = carry_s[:, :LANES].astype(i32)


def _group_specs(rows, width, na, nb):
    first = na // rows
    return (pl.BlockSpec((rows, width), lambda i: (jnp.minimum(i, first - 1), 0)),
            pl.BlockSpec((rows, width), lambda i: (jnp.maximum(i - first, 0), 0)))


def _router(h1a, h1b, g, wr, br, tri):
    na, nb = h1a.shape[0], h1b.shape[0]
    t = na + nb
    rows = ROUTE_ROWS
    assert na % rows == 0 and nb % rows == 0
    return pl.pallas_call(
        functools.partial(_router_kernel, first_steps=na // rows),
        grid=(t // rows,),
        in_specs=[*_group_specs(rows, D_MODEL, na, nb),
                  _full(g.shape), _full(wr.shape), _full(br.shape), _full(tri.shape)],
        out_specs=[pl.BlockSpec((rows, HALF), lambda i: (i, 0)),
                   pl.BlockSpec((TOP_K, rows), lambda i: (0, i)),
                   pl.BlockSpec((TOP_K, rows), lambda i: (0, i)),
                   pl.BlockSpec((rows, LANES), lambda i: (i, 0)),
                   _full((N_EXPERTS, LANES))],
        out_shape=[jax.ShapeDtypeStruct((t, HALF), u32),
                   jax.ShapeDtypeStruct((TOP_K, t), i32),
                   jax.ShapeDtypeStruct((TOP_K, t), i32),
                   jax.ShapeDtypeStruct((t, LANES), f32),
                   jax.ShapeDtypeStruct((N_EXPERTS, LANES), i32)],
        scratch_shapes=[pltpu.VMEM((N_EXPERTS, rows), f32)],
        compiler_params=pltpu.CompilerParams(dimension_semantics=("arbitrary",),
                                             vmem_limit_bytes=VMEM_LIMIT),
        name="router",
    )(h1a, h1b, g, wr, br, tri)


def _positions_kernel(off_ref, idx_ref, rank_ref, pos_ref):
    idx = idx_ref[...]
    pos = rank_ref[...]
    for e in range(N_EXPERTS):
        pos = pos + jnp.where(idx == e, off_ref[e], 0)
    pos_ref[...] = pos


def _positions(off, idx_t, rank_t):
    spec = pl.BlockSpec(idx_t.shape, lambda i, *_: (0, 0))
    return pl.pallas_call(
        _positions_kernel,
        grid_spec=pltpu.PrefetchScalarGridSpec(num_scalar_prefetch=1, grid=(1,),
                                               in_specs=[spec, spec], out_specs=spec),
        out_shape=jax.ShapeDtypeStruct(idx_t.shape, i32),
        name="positions",
    )(off, idx_t, rank_t)


def _sc_mesh():
    return plsc.VectorSubcoreMesh(core_axis_name="core", subcore_axis_name="subcore")


def _sc_params():
    return dataclasses.replace(pltpu.CompilerParams(), needs_layout_passes=False)


def _sc_worker():
    return lax.axis_index("subcore") * SC_CORES + lax.axis_index("core")


def _sc_copy_rows(src_hbm, idx_v, dst_hbm, dst_row0, n, rows_v, gsem, wsem):
    n_ch = n // SC_CHUNK
    n_pair = n_ch // 2

    def gather(c, b):
        o = pl.multiple_of(c * SC_CHUNK, SC_CHUNK)
        return pltpu.make_async_copy(src_hbm.at[idx_v.at[pl.ds(o, SC_CHUNK)]], rows_v.at[b], gsem.at[b])

    def write(c, b):
        o = pl.multiple_of(c * SC_CHUNK, SC_CHUNK)
        return pltpu.make_async_copy(rows_v.at[b], dst_hbm.at[pl.ds(dst_row0 + o, SC_CHUNK)], wsem.at[b])

    gather(0, 0).start()

    @pl.loop(0, n_pair)
    def _(i):
        c0 = 2 * i
        gather(c0, 0).wait()

        @pl.when(i > 0)
        def _():
            write(c0 - 1, 1).wait()

        gather(c0 + 1, 1).start()
        write(c0, 0).start()
        gather(c0 + 1, 1).wait()
        write(c0, 0).wait()

        @pl.when(c0 + 2 < n_ch)
        def _():
            gather(c0 + 2, 0).start()

        write(c0 + 1, 1).start()

    if n_ch % 2:
        gather(n_ch - 1, 0).wait()
        write(n_ch - 1, 0).start()
        write(n_ch - 1, 0).wait()
    if n_pair:
        write(2 * n_pair - 1, 1).wait()


def _sc_dispatch(vpk, pos_flat, row0, n_out):
    n_tok = vpk.shape[0]
    per_w = n_out // SC_WORKERS
    assert n_out % (SC_WORKERS * SC_CHUNK) == 0 and n_tok % SC_LANES == 0 and row0 % SC_LANES == 0

    @pl.kernel(out_type=jax.ShapeDtypeStruct((n_out, HALF), u32), mesh=_sc_mesh(),
               compiler_params=_sc_params(), name="dispatch",
               scratch_types=[pltpu.VMEM((per_w,), i32), pltpu.VMEM((n_tok,), i32),
                              pltpu.VMEM((2, SC_CHUNK, HALF), u32),
                              pltpu.SemaphoreType.DMA((2,)), pltpu.SemaphoreType.DMA((2,))])
    def k(v_hbm, p_hbm, o_hbm, src_v, pos_v, rows_v, gsem, wsem):
        out0 = _sc_worker() * per_w
        lo = row0 + out0

        @pl.loop(0, per_w // SC_LANES)
        def _(i):
            o = pl.multiple_of(i * SC_LANES, SC_LANES)
            src_v[pl.ds(o, SC_LANES)] = lax.iota(i32, SC_LANES) + lax.rem(lo + o, n_tok)

        @pl.loop(0, TOP_K)
        def _(kk):
            pltpu.sync_copy(p_hbm.at[pl.ds(kk * n_tok, n_tok)], pos_v)

            @plsc.parallel_loop(0, n_tok, step=SC_LANES, unroll=8)
            def _(o):
                p = pos_v[pl.ds(pl.multiple_of(o, SC_LANES), SC_LANES)] - lo
                mine = (p >= 0) & (p < per_w)
                tok = lax.iota(i32, SC_LANES) + o
                plsc.store_scatter(src_v, [jnp.where(mine, p, 0)], tok, mask=mine)

        _sc_copy_rows(v_hbm, src_v, o_hbm, out0, per_w, rows_v, gsem, wsem)

    return k(vpk, pos_flat)


def _sc_collect(ys, pos_flat, t0, n):
    n_tok = pos_flat.shape[0] // TOP_K
    per_w = TOP_K * n // SC_WORKERS
    per_k = SC_WORKERS // TOP_K
    assert (TOP_K * n) % (SC_WORKERS * SC_CHUNK) == 0 and t0 % 8 == 0

    @pl.kernel(out_type=jax.ShapeDtypeStruct((TOP_K * n, HALF), u32), mesh=_sc_mesh(),
               compiler_params=_sc_params(), name="collect",
               scratch_types=[pltpu.VMEM((per_w,), i32), pltpu.VMEM((2, SC_CHUNK, HALF), u32),
                              pltpu.SemaphoreType.DMA((2,)), pltpu.SemaphoreType.DMA((2,))])
    def k(y_hbm, p_hbm, o_hbm, pos_v, rows_v, gsem, wsem):
        w = _sc_worker()
        src = lax.div(w, per_k) * n_tok + t0 + lax.rem(w, per_k) * per_w
        pltpu.sync_copy(p_hbm.at[pl.ds(pl.multiple_of(src, 8), per_w)], pos_v)
        _sc_copy_rows(y_hbm, pos_v, o_hbm, w * per_w, per_w, rows_v, gsem, wsem)

    return k(ys, pos_flat)


def _ffn_kernel(te_ref, tf_ref, slot_ref, next_ref, rows_ref, nv_ref, xs_ref, wg_hbm, bg_ref, wu_hbm, bu_ref, wd_hbm, bd_ref,
                *rest):
    ys_ref, wbuf, wg_s, wu_s, wd_s, wsem = rest[-6:]
    step = pl.program_id(0)
    w_hbm = (wg_hbm, wu_hbm, wd_hbm)
    w_bf16 = (wg_s, wu_s, wd_s)

    def fetch(e, slot):
        return [pltpu.make_async_copy(w_hbm[m].at[e], wbuf.at[slot, m], wsem.at[slot, m]) for m in range(3)]

    @pl.when(step < nv_ref[0])
    def _():
        @pl.when(tf_ref[step] == 1)
        def _():
            slot = slot_ref[step]

            @pl.when(step == 0)
            def _():
                for cp in fetch(te_ref[0], slot):
                    cp.start()

            for cp in fetch(te_ref[step], slot):
                cp.wait()
            for s in range(2):
                @pl.when(slot == s)
                def _():
                    for m in range(3):
                        w_bf16[m][...] = wbuf[s, m].astype(bf16)

            @pl.when(next_ref[step] >= 0)
            def _():
                for cp in fetch(next_ref[step], 1 - slot):
                    cp.start()

        def rows_block(rb):
            rs = slice(rb * FFN_SUB, (rb + 1) * FFN_SUB)
            x = _unpack_rows(xs_ref[rs, :]).astype(bf16)
            g = jnp.minimum(_dot(x, wg_s[...]) + bg_ref[0], SWIGLU_LIMIT)
            u = jnp.clip(_dot(x, wu_s[...]) + bu_ref[0], -SWIGLU_LIMIT, SWIGLU_LIMIT)
            hid = (u + 1.0) * (g * jax.nn.sigmoid(SWIGLU_ALPHA * g))
            y = _dot(hid.astype(bf16), wd_s[...]) + bd_ref[0]
            ys_ref[rs, :] = _pack_rows(y)

        @pl.when(rows_ref[step] > FFN_SUB)
        def _():
            rows_block(0)
            rows_block(1)

        @pl.when(rows_ref[step] <= FFN_SUB)
        def _():
            rows_block(0)
            ys_ref[FFN_SUB:, :] = jnp.zeros((FFN_ROWS - FFN_SUB, HALF), u32)

    @pl.when(step >= nv_ref[0])
    def _():
        ys_ref[...] = jnp.zeros_like(ys_ref)


def _ffn(tables, xs, n_rows_total, tile0, w_gate, b_gate, w_up, b_up, w_down, b_down, prev=None):
    n_tiles = xs.shape[0] // FFN_ROWS
    assert D_FF == D_MODEL
    assert FFN_ROWS == 2 * FFN_SUB

    def row_map(i, te, tf, sl, nx, rw, nv):
        return (jnp.minimum(i, jnp.maximum(nv[0] - 1, 0)), 0)

    def b_map(i, te, tf, sl, nx, rw, nv):
        return (te[i], 0, 0)

    w_spec = pl.BlockSpec(memory_space=pl.ANY)
    b_spec = pl.BlockSpec((1, 1, D_FF), b_map)
    in_specs = [pl.BlockSpec((FFN_ROWS, HALF), row_map), w_spec, b_spec, w_spec, b_spec, w_spec, b_spec]
    args = [xs, w_gate, b_gate.reshape(N_EXPERTS, 1, D_FF), w_up, b_up.reshape(N_EXPERTS, 1, D_FF),
            w_down, b_down.reshape(N_EXPERTS, 1, D_MODEL)]
    aliases = {}
    if prev is not None:
        in_specs.append(pl.BlockSpec(memory_space=pl.ANY))
        aliases = {len(tables) + len(args): 0}
        args.append(prev)
    return pl.pallas_call(
        _ffn_kernel,
        grid_spec=pltpu.PrefetchScalarGridSpec(
            num_scalar_prefetch=6,
            grid=(n_tiles,),
            in_specs=in_specs,
            out_specs=pl.BlockSpec((FFN_ROWS, HALF), lambda i, *_: (tile0 + i, 0)),
            scratch_shapes=[pltpu.VMEM((2, 3, D_MODEL, D_FF), f32),
                            pltpu.VMEM((D_MODEL, D_FF), bf16),
                            pltpu.VMEM((D_MODEL, D_FF), bf16),
                            pltpu.VMEM((D_FF, D_MODEL), bf16),
                            pltpu.SemaphoreType.DMA((2, 3))]),
        out_shape=jax.ShapeDtypeStruct((n_rows_total, HALF), u32),
        input_output_aliases=aliases,
        compiler_params=pltpu.CompilerParams(dimension_semantics=("arbitrary",),
                                             vmem_limit_bytes=VMEM_LIMIT),
        name="experts",
    )(*tables, *args)


def _finish_kernel(g_ref, h1a_ref, h1b_ref, pw_ref, pa_ref, pb_ref, g_post_ref, w_ple_ref, w_pg_ref, b_pg_ref,
                   g_ple_ref, *rest, first_steps):
    outa_ref, outb_ref = rest[-2:]
    step = pl.program_id(0)
    rows = h1a_ref.shape[0]
    pw = pw_ref[...]
    f = jnp.zeros((rows, D_MODEL), f32)
    for k in range(TOP_K):
        f = f + pw[:, k:k + 1] * _unpack_rows(g_ref[k])
    h2 = _two_groups(step, first_steps, h1a_ref, h1b_ref) + _rms(f, g_post_ref[...])
    gate = jax.nn.sigmoid(_dot(h2.astype(bf16), w_pg_ref[...]) + b_pg_ref[...])
    bt, tt = pa_ref.shape[0], pa_ref.shape[1]
    pin_a = pltpu.einshape("btd->tbd", pa_ref[...]).reshape(rows, PLE_DIM)
    pin = jnp.where(step < first_steps, pin_a, pb_ref[...]).astype(bf16)
    pe = _dot(pin, w_ple_ref[...]) * gate
    out = h2 + _rms(pe, g_ple_ref[...])

    @pl.when(step < first_steps)
    def _():
        outa_ref[...] = pltpu.einshape("tbd->btd", out.reshape(tt, bt, D_MODEL))

    @pl.when(step >= first_steps)
    def _():
        outb_ref[...] = out


def _finish(g, h1a, h1b, pw, pa, pb, g_post, w_ple, w_pg, b_pg, g_ple, *, a0, a_steps, b_steps, prev=None):
    na, nb = h1a.shape[0], h1b.shape[0]
    rows = FIN_ROWS
    bt = pa.shape[0]
    tt = rows // bt
    assert na % rows == 0 and nb % rows == 0 and a_steps >= 1 and rows % bt == 0

    def spec_a(width):
        return pl.BlockSpec((rows, width), lambda i: (a0 + jnp.minimum(i, a_steps - 1), 0))

    def spec_a3(width):
        return pl.BlockSpec((bt, tt, width), lambda i: (0, a0 + jnp.minimum(i, a_steps - 1), 0))

    def spec_b(width):
        return pl.BlockSpec((rows, width), lambda i: (jnp.maximum(i - a_steps, 0), 0))

    in_specs = [pl.BlockSpec((TOP_K, rows, HALF), lambda i: (0, i, 0)),
                spec_a(D_MODEL), spec_b(D_MODEL),
                pl.BlockSpec((rows, LANES), lambda i: (a0 + i, 0)),
                spec_a3(PLE_DIM), spec_b(PLE_DIM),
                _full(g_post.shape), _full(w_ple.shape), _full(w_pg.shape),
                _full(b_pg.shape), _full(g_ple.shape)]
    args = [g, h1a, h1b, pw, pa, pb, g_post, w_ple, w_pg, b_pg, g_ple]
    aliases = {}
    if prev is not None:
        in_specs.append(pl.BlockSpec(memory_space=pl.ANY))
        aliases = {len(args): 0}
        args.append(prev)
    return pl.pallas_call(
        functools.partial(_finish_kernel, first_steps=a_steps),
        grid=(a_steps + b_steps,),
        in_specs=in_specs,
        out_specs=[spec_a3(D_MODEL), spec_b(D_MODEL)],
        out_shape=[jax.ShapeDtypeStruct((bt, na // bt, D_MODEL), f32), jax.ShapeDtypeStruct((nb, D_MODEL), f32)],
        input_output_aliases=aliases,
        compiler_params=pltpu.CompilerParams(dimension_semantics=("arbitrary",),
                                             vmem_limit_bytes=VMEM_LIMIT),
        name="finish",
    )(*args)


def _block_diag_pairs(w):
    per = MXU_DIM // HEAD_A
    w4 = w.reshape(N_HEADS_A // per, per, HEAD_A, HEAD_A)
    blocks = jnp.where(jnp.eye(per, dtype=bool)[None, :, None, :, None], w4[:, :, :, None, :], 0.0)
    return blocks.reshape(N_HEADS_A // per, MXU_DIM, MXU_DIM)


def _time_major(x):
    x = jnp.swapaxes(x, 0, 1)
    return x.reshape((x.shape[0] * x.shape[1],) + x.shape[2:])


def _batch_major(x, b):
    return jnp.swapaxes(x.reshape(x.shape[0] // b, b, x.shape[1]), 0, 1)


def _routing_tables(counts, n_tiles):
    tiles = (counts + FFN_ROWS - 1) // FFN_ROWS
    tile_ends = jnp.cumsum(tiles)
    off = (tile_ends - tiles) * FFN_ROWS
    n_valid = tile_ends[-1:]
    tile_ids = jnp.minimum(jnp.arange(n_tiles, dtype=i32), n_valid - 1)
    tile_e = jnp.sum((tile_ids[:, None] >= tile_ends[None, :]).astype(i32), axis=1)
    cand = jnp.where(tiles > 0, jnp.arange(N_EXPERTS, dtype=i32), N_EXPERTS)
    suffix_min = lax.cummin(cand[::-1])[::-1]
    nxt = jnp.concatenate([suffix_min[1:], jnp.full((1,), N_EXPERTS, i32)])
    onehot = (tile_e[:, None] == jnp.arange(N_EXPERTS, dtype=i32)[None, :]).astype(i32)
    at_tile = lambda per_expert: jnp.sum(onehot * per_expert[None, :].astype(i32), axis=1)
    tile_next = at_tile(jnp.where(nxt < N_EXPERTS, nxt, -1))
    tile_rows = jnp.clip(at_tile(counts) - (tile_ids - at_tile(tile_ends - tiles)) * FFN_ROWS, 0, FFN_ROWS)
    tile_group_end = at_tile(tile_ends)
    return off.astype(i32), (tile_e.astype(i32), tile_next.astype(i32), tile_group_end.astype(i32),
                             tile_rows.astype(i32), n_valid.astype(i32))


def _tile_tables(full, tile0, n):
    tile_e, tile_next, tile_group_end, tile_rows = (a[tile0:tile0 + n] for a in full[:4])
    n_valid = full[4]
    first = jnp.concatenate([jnp.ones((1,), i32), (tile_e[1:] != tile_e[:-1]).astype(i32)])
    slot = (jnp.cumsum(first) - 1) % 2
    nxt = jnp.where(tile_group_end < tile0 + n, tile_next, -1)
    return (tile_e, first, slot.astype(i32), nxt.astype(i32), tile_rows, jnp.clip(n_valid - tile0, 0, n))


def _layer(xp, xs_tm, pp, ps_tm, state_h, state_conv, state_pool, lw):
    (norm_mix_pre, w_in, conv_w, conv_b, w_rgate, b_rgate, w_igate, b_igate, lru_lambda,
     pool_w, pool_b, pool_scale, norm_group_a, norm_group_b, w_out, norm_mix_post,
     norm_ffn_pre, w_router, b_router, w_gate, b_gate, w_up, b_up, w_down, b_down, norm_ffn_post,
     w_ple, w_ple_gate, b_ple_gate, norm_ple) = lw
    b_p, s_p = xp.shape[0], xp.shape[1]
    n_p, n_s = b_p * s_p, xs_tm.shape[0]
    b_s = state_h.shape[0]
    row = lambda a: a.reshape(1, -1)

    w_gates = jnp.concatenate([_block_diag_pairs(w_rgate), _block_diag_pairs(w_igate)], axis=-1).astype(bf16)
    mix_w = (row(norm_mix_pre), w_in.astype(bf16), conv_w, row(conv_b),
             w_gates, row(b_rgate), row(b_igate), row(lru_lambda),
             pool_w.astype(bf16), row(pool_b), row(pool_scale), row(norm_group_a), row(norm_group_b),
             w_out.astype(bf16), row(norm_mix_post))

    zeros = lambda *s: jnp.zeros(s, f32)
    h1_p, hfin_p, cfin_p, pfin_p = _mixer(
        xp, zeros((CONV_W - 1) * b_p, W_A), zeros(POOL_BUF * b_p, W_B), zeros(b_p, W_A),
        mix_w, bt=b_p, start=0)
    h1_s, hfin_s, cfin_s, pfin_s = _mixer(
        xs_tm, _time_major(state_conv), _time_major(state_pool), state_h,
        mix_w, bt=b_s, start=PAST_LEN)
    t = n_p + n_s

    w_router_pad = jnp.zeros((D_MODEL, LANES), f32).at[:, :N_EXPERTS].set(w_router)
    b_router_pad = jnp.zeros((1, LANES), f32).at[0, :N_EXPERTS].set(b_router)
    tri = jnp.triu(jnp.ones((ROUTE_ROWS, ROUTE_ROWS), bf16), k=1)
    vpk, idx_t, rank_t, pw, cnt = _router(h1_p, h1_s, row(norm_ffn_pre), w_router_pad.astype(bf16),
                                          b_router_pad, tri)

    n_tiles = (t * TOP_K) // FFN_ROWS + N_EXPERTS
    off, tables = _routing_tables(cnt[:, 0], n_tiles)
    pos_flat = _positions(off, idx_t, rank_t).reshape(-1)

    step_tiles = SC_WORKERS * SC_CHUNK // FFN_ROWS
    tiles_1 = (n_tiles // 2) // step_tiles * step_tiles
    ys = None
    for tile0, n in ((0, tiles_1), (tiles_1, n_tiles - tiles_1)):
        xs = _sc_dispatch(vpk, pos_flat, tile0 * FFN_ROWS, n * FFN_ROWS)
        ys = _ffn(_tile_tables(tables, tile0, n), xs, n_tiles * FFN_ROWS, tile0,
                  w_gate, b_gate, w_up, b_up, w_down, b_down, prev=ys)
    fin_w = (row(norm_ffn_post), w_ple.astype(bf16), w_ple_gate.astype(bf16), row(b_ple_gate), row(norm_ple))
    a_blocks, b_blocks = n_p // FIN_ROWS, n_s // FIN_ROWS
    first = a_blocks // 2
    n1, n2 = first * FIN_ROWS, t - first * FIN_ROWS
    g1 = _sc_collect(ys, pos_flat, 0, n1).reshape(TOP_K, n1, HALF)
    out_p, _ = _finish(g1, h1_p, h1_s, pw, pp, ps_tm, *fin_w, a0=0, a_steps=first, b_steps=0)
    g2 = _sc_collect(ys, pos_flat, n1, n2).reshape(TOP_K, n2, HALF)
    out_p, out_s = _finish(g2, h1_p, h1_s, pw, pp, ps_tm, *fin_w, a0=first, a_steps=a_blocks - first,
                           b_steps=b_blocks, prev=out_p)

    states = (hfin_p, _batch_major(cfin_p, b_p), _batch_major(pfin_p, b_p),
              hfin_s, _batch_major(cfin_s, b_s), _batch_major(pfin_s, b_s))
    return out_p, out_s, states


def kernel(x_prompt, x_sample, state_rglru_h, state_rglru_conv, state_pool, p_prompt, p_sample, norm_mix_pre, w_in, conv_w, conv_b, w_rgate, b_rgate, w_igate, b_igate, lru_lambda, pool_w, pool_b, pool_scale, norm_group_a, norm_group_b, w_out, norm_mix_post, norm_ffn_pre, w_router, b_router, w_gate, b_gate, w_up, b_up, w_down, b_down, norm_ffn_post, w_ple, w_ple_gate, b_ple_gate, norm_ple):
    depth = w_in.shape[0]
    b_p, b_s = x_prompt.shape[0], x_sample.shape[0]
    per_layer = (norm_mix_pre, w_in, conv_w, conv_b, w_rgate, b_rgate, w_igate, b_igate, lru_lambda,
                 pool_w, pool_b, pool_scale, norm_group_a, norm_group_b, w_out, norm_mix_post,
                 norm_ffn_pre, w_router, b_router, w_gate, b_gate, w_up, b_up, w_down, b_down,
                 norm_ffn_post, w_ple, w_ple_gate, b_ple_gate, norm_ple)
    hp, hs = x_prompt, _time_major(x_sample)
    collected = []
    for i in range(depth):
        hp, hs, states = _layer(hp, hs, p_prompt[i], _time_major(p_sample[i]),
                                state_rglru_h[i], state_rglru_conv[i], state_pool[i],
                                tuple(w[i] for w in per_layer))
        collected.append(states)
    stacked = tuple(jnp.stack([c[j] for c in collected]) for j in range(6))
    return (hp, _batch_major(hs, b_s)) + stacked
```

```python
import dataclasses
import functools

import jax
import jax.numpy as jnp
from jax import lax
from jax.experimental import pallas as pl
from jax.experimental.pallas import tpu as pltpu
from jax.experimental.pallas import tpu_sc as plsc

D_MODEL = 1024
W_A = 512
W_B = 512
N_HEADS_A = 8
HEAD_A = W_A // N_HEADS_A
CONV_W = 4
LRU_C = 8.0
POOL_WINDOWS = (2, 4, 8, 16)
GROUP_B = W_B // len(POOL_WINDOWS)
POOL_BUF = max(POOL_WINDOWS) - 1
N_EXPERTS = 32
TOP_K = 4
D_FF = 1024
SWIGLU_LIMIT = 7.0
SWIGLU_ALPHA = 1.702
PLE_DIM = 256
EPS = 1e-6
PAST_LEN = 16384

LANES = 128
MXU_DIM = 256
HALF = D_MODEL // 2
SC_CORES = 2
SC_SUBCORES = 16
SC_WORKERS = SC_CORES * SC_SUBCORES
SC_LANES = 16
SC_CHUNK = 64

MIX_ROWS = 512
ROUTE_ROWS = 512
FFN_ROWS = 1024
FFN_SUB = 256
FIN_ROWS = 512
VMEM_LIMIT = 56 * 1024 * 1024

f32 = jnp.float32
bf16 = jnp.bfloat16
u32 = jnp.uint32
i32 = jnp.int32


def _rms(x, g):
    return x * lax.rsqrt(jnp.mean(x * x, axis=-1, keepdims=True) + EPS) * g


def _dot(a, b):
    return jnp.dot(a, b, preferred_element_type=f32)


def _pack_rows(x):
    bits = lax.bitcast_convert_type(x.astype(bf16).astype(f32), u32)
    return (bits[:, HALF:] & jnp.uint32(0xFFFF0000)) | (bits[:, :HALF] >> 16)


def _unpack_rows(p):
    lo = lax.bitcast_convert_type(p << 16, f32)
    hi = lax.bitcast_convert_type(p & jnp.uint32(0xFFFF0000), f32)
    return jnp.concatenate([lo, hi], axis=-1)


def _mixer_kernel(x_ref, conv0_ref, pool0_ref, h0_ref, g_pre_ref, w_in_ref, conv_w_ref, conv_b_ref,
                  w_gates_ref, br_ref, bi_ref, lam_ref,
                  pool_w_ref, pool_b_ref, pool_scale_ref, ng_a_ref, ng_b_ref, w_out_ref, g_post_ref,
                  h1_ref, hfin_ref, convfin_ref, poolfin_ref,
                  h_s, conv_s, pool_s, a_s, b_s, hs_s, *, bt, tt, start):
    step = pl.program_id(0)
    rows = bt * tt

    @pl.when(step == 0)
    def _():
        h_s[...] = h0_ref[...]
        conv_s[...] = conv0_ref[...]
        pool_s[...] = pool0_ref[...]

    if x_ref.ndim == 3:
        x = pltpu.einshape("btd->tbd", x_ref[...]).reshape(rows, D_MODEL)
    else:
        x = x_ref[...]
    u = _rms(x, g_pre_ref[...]).astype(bf16)
    z = _dot(u, w_in_ref[...])
    xa, ga, xb = z[:, :W_A], z[:, W_A:2 * W_A], z[:, 2 * W_A:]

    ext_a = jnp.concatenate([conv_s[...], xa], axis=0)
    xc = conv_b_ref[...]
    for k in range(CONV_W):
        xc = xc + ext_a[k * bt:k * bt + rows] * conv_w_ref[k:k + 1, :]
    conv_s[...] = ext_a[rows:]

    xc16 = xc.astype(bf16)
    gates = [_dot(xc16[:, j * MXU_DIM:(j + 1) * MXU_DIM], w_gates_ref[j]) for j in range(W_A // MXU_DIM)]
    r = jax.nn.sigmoid(jnp.concatenate([gj[:, :MXU_DIM] for gj in gates], axis=-1) + br_ref[...])
    ig = jax.nn.sigmoid(jnp.concatenate([gj[:, MXU_DIM:] for gj in gates], axis=-1) + bi_ref[...])
    lam = lam_ref[...]
    softplus_neg = jnp.maximum(-lam, 0.0) + jnp.log1p(jnp.exp(-jnp.abs(lam)))
    log_a = (-LRU_C) * r * softplus_neg
    a_s[...] = jnp.exp(log_a)
    th = jnp.tanh(log_a)
    b_s[...] = jnp.sqrt(-2.0 * th / (1.0 - th)) * (ig * xc)

    def scan_step(t, h):
        sl = pl.ds(pl.multiple_of(t * bt, bt), bt)
        h = a_s[sl, :] * h + b_s[sl, :]
        hs_s[sl, :] = h
        return h

    h_last = lax.fori_loop(0, tt, scan_step, h_s[...], unroll=True)
    h_s[...] = h_last
    ya = hs_s[...] * jax.nn.gelu(ga)

    ext_b = jnp.concatenate([pool_s[...], xb], axis=0)
    pool_s[...] = ext_b[rows:]
    s2 = ext_b[bt:, :] + ext_b[:-bt, :]
    s4 = s2[2 * bt:, GROUP_B:] + s2[:-2 * bt, GROUP_B:]
    s8 = s4[4 * bt:, GROUP_B:] + s4[:-4 * bt, GROUP_B:]
    s16 = s8[8 * bt:, GROUP_B:] + s8[:-8 * bt, GROUP_B:]
    wins = (s2[14 * bt:, :GROUP_B], s4[12 * bt:, :GROUP_B], s8[8 * bt:, :GROUP_B], s16)
    t_idx = lax.broadcasted_iota(i32, (rows, GROUP_B), 0) // bt
    pos1 = (t_idx + (step * tt + start + 1)).astype(f32)
    yb_parts = []
    for g, w in enumerate(POOL_WINDOWS):
        cnt = jnp.minimum(jnp.float32(w), pos1)
        d = wins[g] / cnt - xb[:, g * GROUP_B:(g + 1) * GROUP_B]
        yb_parts.append(_dot(d.astype(bf16), pool_w_ref[g]))
    yb = (jnp.concatenate(yb_parts, axis=-1) + pool_b_ref[...]) * pool_scale_ref[...]

    na = _rms(ya, ng_a_ref[...])
    nb = _rms(yb, ng_b_ref[...])
    m = _dot(jnp.concatenate([na, nb], axis=-1).astype(bf16), w_out_ref[...])
    h1_ref[...] = x + _rms(m, g_post_ref[...])

    hfin_ref[...] = h_last
    convfin_ref[...] = conv_s[...]
    poolfin_ref[...] = pool_s[...]


def _full(shape):
    return pl.BlockSpec(shape, lambda i, *_: (0,) * len(shape))


def _mixer(x, conv0, pool0, h0, wts, *, bt, start):
    tt = MIX_ROWS // bt
    rows = bt * tt
    if x.ndim == 3:
        assert x.shape[0] == bt and x.shape[1] % tt == 0
        n_steps = x.shape[1] // tt
        x_spec = pl.BlockSpec((bt, tt, D_MODEL), lambda i: (0, i, 0))
    else:
        assert x.shape[0] % rows == 0
        n_steps = x.shape[0] // rows
        x_spec = pl.BlockSpec((rows, D_MODEL), lambda i: (i, 0))
    kern = functools.partial(_mixer_kernel, bt=bt, tt=tt, start=start)
    small = [conv0, pool0, h0] + list(wts)
    return pl.pallas_call(
        kern,
        grid=(n_steps,),
        in_specs=[x_spec] + [_full(a.shape) for a in small],
        out_specs=[pl.BlockSpec((rows, D_MODEL), lambda i: (i, 0)),
                   _full((bt, W_A)), _full(((CONV_W - 1) * bt, W_A)), _full((POOL_BUF * bt, W_B))],
        out_shape=[jax.ShapeDtypeStruct((n_steps * rows, D_MODEL), f32),
                   jax.ShapeDtypeStruct((bt, W_A), f32),
                   jax.ShapeDtypeStruct(((CONV_W - 1) * bt, W_A), f32),
                   jax.ShapeDtypeStruct((POOL_BUF * bt, W_B), f32)],
        scratch_shapes=[pltpu.VMEM((bt, W_A), f32),
                        pltpu.VMEM(((CONV_W - 1) * bt, W_A), f32),
                        pltpu.VMEM((POOL_BUF * bt, W_B), f32),
                        pltpu.VMEM((rows, W_A), f32),
                        pltpu.VMEM((rows, W_A), f32),
                        pltpu.VMEM((rows, W_A), f32)],
        compiler_params=pltpu.CompilerParams(dimension_semantics=("arbitrary",),
                                             vmem_limit_bytes=VMEM_LIMIT),
        name="mixer",
    )(x, *small)


def _two_groups(step, first_steps, a_ref, b_ref):
    return jnp.where(step < first_steps, a_ref[...], b_ref[...])


def _router_kernel(h1a_ref, h1b_ref, g_ref, wr_ref, br_ref, tri_ref,
                   vpk_ref, idx_ref, rank_ref, pw_ref, cnt_ref, carry_s, *, first_steps):
    step = pl.program_id(0)
    rows = h1a_ref.shape[0]

    @pl.when(step == 0)
    def _():
        carry_s[...] = jnp.zeros_like(carry_s)

    v = _rms(_two_groups(step, first_steps, h1a_ref, h1b_ref), g_ref[...])
    vpk_ref[...] = _pack_rows(v)
    logits = _dot(v.astype(bf16), wr_ref[...]) + br_ref[...]
    lt = jnp.transpose(logits)[:N_EXPERTS, :]

    eio = lax.broadcasted_iota(i32, (N_EXPERTS, rows), 0).astype(f32)
    work = lt
    vals, idxs, sels = [], [], []
    for _ in range(TOP_K):
        m = jnp.max(work, axis=0, keepdims=True)
        ik = jnp.min(jnp.where(work == m, eio, float(N_EXPERTS)), axis=0, keepdims=True)
        sel = eio == ik
        vals.append(m)
        idxs.append(ik)
        sels.append(sel)
        work = jnp.where(sel, -jnp.inf, work)
    exps = [jnp.exp(val - vals[0]) for val in vals]
    denom = exps[0] + exps[1] + exps[2] + exps[3]
    probs = [e / denom for e in exps]

    chosen = sels[0] | sels[1] | sels[2] | sels[3]
    onehot = jnp.where(chosen, 1.0, 0.0)
    before = _dot(onehot.astype(bf16), tri_ref[...])
    base = carry_s[...] + before
    ranks = [jnp.sum(jnp.where(sel, base, 0.0), axis=0, keepdims=True) for sel in sels]
    carry_s[...] = carry_s[...] + jnp.sum(onehot, axis=1, keepdims=True)

    idx_ref[...] = jnp.concatenate(idxs, axis=0).astype(i32)
    rank_ref[...] = jnp.concatenate(ranks, axis=0).astype(i32)
    pad = jnp.zeros((LANES - TOP_K, rows), f32)
    pw_ref[...] = jnp.transpose(jnp.concatenate(probs + [pad], axis=0))
    cnt_ref[...] = carry_s[:, :LANES].astype(i32)


def _group_specs(rows, width, na, nb):
    first = na // rows
    return (pl.BlockSpec((rows, width), lambda i: (jnp.minimum(i, first - 1), 0)),
            pl.BlockSpec((rows, width), lambda i: (jnp.maximum(i - first, 0), 0)))


def _router(h1a, h1b, g, wr, br, tri):
    na, nb = h1a.shape[0], h1b.shape[0]
    t = na + nb
    rows = ROUTE_ROWS
    assert na % rows == 0 and nb % rows == 0
    return pl.pallas_call(
        functools.partial(_router_kernel, first_steps=na // rows),
        grid=(t // rows,),
        in_specs=[*_group_specs(rows, D_MODEL, na, nb),
                  _full(g.shape), _full(wr.shape), _full(br.shape), _full(tri.shape)],
        out_specs=[pl.BlockSpec((rows, HALF), lambda i: (i, 0)),
                   pl.BlockSpec((TOP_K, rows), lambda i: (0, i)),
                   pl.BlockSpec((TOP_K, rows), lambda i: (0, i)),
                   pl.BlockSpec((rows, LANES), lambda i: (i, 0)),
                   _full((N_EXPERTS, LANES))],
        out_shape=[jax.ShapeDtypeStruct((t, HALF), u32),
                   jax.ShapeDtypeStruct((TOP_K, t), i32),
                   jax.ShapeDtypeStruct((TOP_K, t), i32),
                   jax.ShapeDtypeStruct((t, LANES), f32),
                   jax.ShapeDtypeStruct((N_EXPERTS, LANES), i32)],
        scratch_shapes=[pltpu.VMEM((N_EXPERTS, rows), f32)],
        compiler_params=pltpu.CompilerParams(dimension_semantics=("arbitrary",),
                                             vmem_limit_bytes=VMEM_LIMIT),
        name="router",
    )(h1a, h1b, g, wr, br, tri)


def _positions_kernel(off_ref, idx_ref, rank_ref, pos_ref):
    idx = idx_ref[...]
    pos = rank_ref[...]
    for e in range(N_EXPERTS):
        pos = pos + jnp.where(idx == e, off_ref[e], 0)
    pos_ref[...] = pos


def _positions(off, idx_t, rank_t):
    spec = pl.BlockSpec(idx_t.shape, lambda i, *_: (0, 0))
    return pl.pallas_call(
        _positions_kernel,
        grid_spec=pltpu.PrefetchScalarGridSpec(num_scalar_prefetch=1, grid=(1,),
                                               in_specs=[spec, spec], out_specs=spec),
        out_shape=jax.ShapeDtypeStruct(idx_t.shape, i32),
        name="positions",
    )(off, idx_t, rank_t)


def _sc_mesh():
    return plsc.VectorSubcoreMesh(core_axis_name="core", subcore_axis_name="subcore")


def _sc_params():
    return dataclasses.replace(pltpu.CompilerParams(), needs_layout_passes=False)


def _sc_worker():
    return lax.axis_index("subcore") * SC_CORES + lax.axis_index("core")


def _sc_copy_rows(src_hbm, idx_v, dst_hbm, dst_row0, n, rows_v, gsem, wsem):
    n_ch = n // SC_CHUNK
    n_pair = n_ch // 2

    def gather(c, b):
        o = pl.multiple_of(c * SC_CHUNK, SC_CHUNK)
        return pltpu.make_async_copy(src_hbm.at[idx_v.at[pl.ds(o, SC_CHUNK)]], rows_v.at[b], gsem.at[b])

    def write(c, b):
        o = pl.multiple_of(c * SC_CHUNK, SC_CHUNK)
        return pltpu.make_async_copy(rows_v.at[b], dst_hbm.at[pl.ds(dst_row0 + o, SC_CHUNK)], wsem.at[b])

    gather(0, 0).start()

    @pl.loop(0, n_pair)
    def _(i):
        c0 = 2 * i
        gather(c0, 0).wait()

        @pl.when(i > 0)
        def _():
            write(c0 - 1, 1).wait()

        gather(c0 + 1, 1).start()
        write(c0, 0).start()
        gather(c0 + 1, 1).wait()
        write(c0, 0).wait()

        @pl.when(c0 + 2 < n_ch)
        def _():
            gather(c0 + 2, 0).start()

        write(c0 + 1, 1).start()

    if n_ch % 2:
        gather(n_ch - 1, 0).wait()
        write(n_ch - 1, 0).start()
        write(n_ch - 1, 0).wait()
    if n_pair:
        write(2 * n_pair - 1, 1).wait()


def _sc_dispatch(vpk, pos_flat, row0, n_out):
    n_tok = vpk.shape[0]
    per_w = n_out // SC_WORKERS
    assert n_out % (SC_WORKERS * SC_CHUNK) == 0 and n_tok % SC_LANES == 0 and row0 % SC_LANES == 0

    @pl.kernel(out_type=jax.ShapeDtypeStruct((n_out, HALF), u32), mesh=_sc_mesh(),
               compiler_params=_sc_params(), name="dispatch",
               scratch_types=[pltpu.VMEM((per_w,), i32), pltpu.VMEM((n_tok,), i32),
                              pltpu.VMEM((2, SC_CHUNK, HALF), u32),
                              pltpu.SemaphoreType.DMA((2,)), pltpu.SemaphoreType.DMA((2,))])
    def k(v_hbm, p_hbm, o_hbm, src_v, pos_v, rows_v, gsem, wsem):
        out0 = _sc_worker() * per_w
        lo = row0 + out0

        @pl.loop(0, per_w // SC_LANES)
        def _(i):
            o = pl.multiple_of(i * SC_LANES, SC_LANES)
            src_v[pl.ds(o, SC_LANES)] = lax.iota(i32, SC_LANES) + lax.rem(lo + o, n_tok)

        @pl.loop(0, TOP_K)
        def _(kk):
            pltpu.sync_copy(p_hbm.at[pl.ds(kk * n_tok, n_tok)], pos_v)

            @plsc.parallel_loop(0, n_tok, step=SC_LANES, unroll=8)
            def _(o):
                p = pos_v[pl.ds(pl.multiple_of(o, SC_LANES), SC_LANES)] - lo
                mine = (p >= 0) & (p < per_w)
                tok = lax.iota(i32, SC_LANES) + o
                plsc.store_scatter(src_v, [jnp.where(mine, p, 0)], tok, mask=mine)

        _sc_copy_rows(v_hbm, src_v, o_hbm, out0, per_w, rows_v, gsem, wsem)

    return k(vpk, pos_flat)


def _sc_collect(ys, pos_flat, t0, n):
    n_tok = pos_flat.shape[0] // TOP_K
    per_w = TOP_K * n // SC_WORKERS
    per_k = SC_WORKERS // TOP_K
    assert (TOP_K * n) % (SC_WORKERS * SC_CHUNK) == 0 and t0 % 8 == 0

    @pl.kernel(out_type=jax.ShapeDtypeStruct((TOP_K * n, HALF), u32), mesh=_sc_mesh(),
               compiler_params=_sc_params(), name="collect",
               scratch_types=[pltpu.VMEM((per_w,), i32), pltpu.VMEM((2, SC_CHUNK, HALF), u32),
                              pltpu.SemaphoreType.DMA((2,)), pltpu.SemaphoreType.DMA((2,))])
    def k(y_hbm, p_hbm, o_hbm, pos_v, rows_v, gsem, wsem):
        w = _sc_worker()
        src = lax.div(w, per_k) * n_tok + t0 + lax.rem(w, per_k) * per_w
        pltpu.sync_copy(p_hbm.at[pl.ds(pl.multiple_of(src, 8), per_w)], pos_v)
        _sc_copy_rows(y_hbm, pos_v, o_hbm, w * per_w, per_w, rows_v, gsem, wsem)

    return k(ys, pos_flat)


def _ffn_kernel(te_ref, tf_ref, slot_ref, next_ref, rows_ref, nv_ref, xs_ref, wg_hbm, bg_ref, wu_hbm, bu_ref, wd_hbm, bd_ref,
                *rest):
    ys_ref, wbuf, wg_s, wu_s, wd_s, wsem = rest[-6:]
    step = pl.program_id(0)
    w_hbm = (wg_hbm, wu_hbm, wd_hbm)
    w_bf16 = (wg_s, wu_s, wd_s)

    def fetch(e, slot):
        return [pltpu.make_async_copy(w_hbm[m].at[e], wbuf.at[slot, m], wsem.at[slot, m]) for m in range(3)]

    @pl.when(step < nv_ref[0])
    def _():
        @pl.when(tf_ref[step] == 1)
        def _():
            slot = slot_ref[step]

            @pl.when(step == 0)
            def _():
                for cp in fetch(te_ref[0], slot):
                    cp.start()

            for cp in fetch(te_ref[step], slot):
                cp.wait()
            for s in range(2):
                @pl.when(slot == s)
                def _():
                    for m in range(3):
                        w_bf16[m][...] = wbuf[s, m].astype(bf16)

            @pl.when(next_ref[step] >= 0)
            def _():
                for cp in fetch(next_ref[step], 1 - slot):
                    cp.start()

        def rows_block(rb, carry):
            rs = pl.ds(pl.multiple_of(rb * FFN_SUB, FFN_SUB), FFN_SUB)
            x = _unpack_rows(xs_ref[rs, :]).astype(bf16)
            g = jnp.minimum(_dot(x, wg_s[...]) + bg_ref[0], SWIGLU_LIMIT)
            u = jnp.clip(_dot(x, wu_s[...]) + bu_ref[0], -SWIGLU_LIMIT, SWIGLU_LIMIT)
            hid = (u + 1.0) * (g * jax.nn.sigmoid(SWIGLU_ALPHA * g))
            y = _dot(hid.astype(bf16), wd_s[...]) + bd_ref[0]
            ys_ref[rs, :] = _pack_rows(y)
            return carry

        def zero_block(rb, carry):
            rs = pl.ds(pl.multiple_of(rb * FFN_SUB, FFN_SUB), FFN_SUB)
            ys_ref[rs, :] = jnp.zeros((FFN_SUB, HALF), u32)
            return carry

        used = lax.div(rows_ref[step] + (FFN_SUB - 1), FFN_SUB)
        lax.fori_loop(0, used, rows_block, 0)
        lax.fori_loop(used, FFN_ROWS // FFN_SUB, zero_block, 0)

    @pl.when(step >= nv_ref[0])
    def _():
        ys_ref[...] = jnp.zeros_like(ys_ref)


def _ffn(tables, xs, n_rows_total, tile0, w_gate, b_gate, w_up, b_up, w_down, b_down, prev=None):
    n_tiles = xs.shape[0] // FFN_ROWS
    assert D_FF == D_MODEL
    assert FFN_ROWS % FFN_SUB == 0

    def row_map(i, te, tf, sl, nx, rw, nv):
        return (jnp.minimum(i, jnp.maximum(nv[0] - 1, 0)), 0)

    def b_map(i, te, tf, sl, nx, rw, nv):
        return (te[i], 0, 0)

    w_spec = pl.BlockSpec(memory_space=pl.ANY)
    b_spec = pl.BlockSpec((1, 1, D_FF), b_map)
    in_specs = [pl.BlockSpec((FFN_ROWS, HALF), row_map), w_spec, b_spec, w_spec, b_spec, w_spec, b_spec]
    args = [xs, w_gate, b_gate.reshape(N_EXPERTS, 1, D_FF), w_up, b_up.reshape(N_EXPERTS, 1, D_FF),
            w_down, b_down.reshape(N_EXPERTS, 1, D_MODEL)]
    aliases = {}
    if prev is not None:
        in_specs.append(pl.BlockSpec(memory_space=pl.ANY))
        aliases = {len(tables) + len(args): 0}
        args.append(prev)
    return pl.pallas_call(
        _ffn_kernel,
        grid_spec=pltpu.PrefetchScalarGridSpec(
            num_scalar_prefetch=6,
            grid=(n_tiles,),
            in_specs=in_specs,
            out_specs=pl.BlockSpec((FFN_ROWS, HALF), lambda i, *_: (tile0 + i, 0)),
            scratch_shapes=[pltpu.VMEM((2, 3, D_MODEL, D_FF), f32),
                            pltpu.VMEM((D_MODEL, D_FF), bf16),
                            pltpu.VMEM((D_MODEL, D_FF), bf16),
                            pltpu.VMEM((D_FF, D_MODEL), bf16),
                            pltpu.SemaphoreType.DMA((2, 3))]),
        out_shape=jax.ShapeDtypeStruct((n_rows_total, HALF), u32),
        input_output_aliases=aliases,
        compiler_params=pltpu.CompilerParams(dimension_semantics=("arbitrary",),
                                             vmem_limit_bytes=VMEM_LIMIT),
        name="experts",
    )(*tables, *args)


def _finish_kernel(g_ref, h1a_ref, h1b_ref, pw_ref, pa_ref, pb_ref, g_post_ref, w_ple_ref, w_pg_ref, b_pg_ref,
                   g_ple_ref, *rest, first_steps):
    outa_ref, outb_ref = rest[-2:]
    step = pl.program_id(0)
    rows = h1a_ref.shape[0]
    pw = pw_ref[...]
    f = jnp.zeros((rows, D_MODEL), f32)
    for k in range(TOP_K):
        f = f + pw[:, k:k + 1] * _unpack_rows(g_ref[k])
    h2 = _two_groups(step, first_steps, h1a_ref, h1b_ref) + _rms(f, g_post_ref[...])
    gate = jax.nn.sigmoid(_dot(h2.astype(bf16), w_pg_ref[...]) + b_pg_ref[...])
    bt, tt = pa_ref.shape[0], pa_ref.shape[1]
    pin_a = pltpu.einshape("btd->tbd", pa_ref[...]).reshape(rows, PLE_DIM)
    pin = jnp.where(step < first_steps, pin_a, pb_ref[...]).astype(bf16)
    pe = _dot(pin, w_ple_ref[...]) * gate
    out = h2 + _rms(pe, g_ple_ref[...])

    @pl.when(step < first_steps)
    def _():
        outa_ref[...] = pltpu.einshape("tbd->btd", out.reshape(tt, bt, D_MODEL))

    @pl.when(step >= first_steps)
    def _():
        outb_ref[...] = out


def _finish(g, h1a, h1b, pw, pa, pb, g_post, w_ple, w_pg, b_pg, g_ple, *, a0, a_steps, b_steps, prev=None):
    na, nb = h1a.shape[0], h1b.shape[0]
    rows = FIN_ROWS
    bt = pa.shape[0]
    tt = rows // bt
    assert na % rows == 0 and nb % rows == 0 and a_steps >= 1 and rows % bt == 0

    def spec_a(width):
        return pl.BlockSpec((rows, width), lambda i: (a0 + jnp.minimum(i, a_steps - 1), 0))

    def spec_a3(width):
        return pl.BlockSpec((bt, tt, width), lambda i: (0, a0 + jnp.minimum(i, a_steps - 1), 0))

    def spec_b(width):
        return pl.BlockSpec((rows, width), lambda i: (jnp.maximum(i - a_steps, 0), 0))

    in_specs = [pl.BlockSpec((TOP_K, rows, HALF), lambda i: (0, i, 0)),
                spec_a(D_MODEL), spec_b(D_MODEL),
                pl.BlockSpec((rows, LANES), lambda i: (a0 + i, 0)),
                spec_a3(PLE_DIM), spec_b(PLE_DIM),
                _full(g_post.shape), _full(w_ple.shape), _full(w_pg.shape),
                _full(b_pg.shape), _full(g_ple.shape)]
    args = [g, h1a, h1b, pw, pa, pb, g_post, w_ple, w_pg, b_pg, g_ple]
    aliases = {}
    if prev is not None:
        in_specs.append(pl.BlockSpec(memory_space=pl.ANY))
        aliases = {len(args): 0}
        args.append(prev)
    return pl.pallas_call(
        functools.partial(_finish_kernel, first_steps=a_steps),
        grid=(a_steps + b_steps,),
        in_specs=in_specs,
        out_specs=[spec_a3(D_MODEL), spec_b(D_MODEL)],
        out_shape=[jax.ShapeDtypeStruct((bt, na // bt, D_MODEL), f32), jax.ShapeDtypeStruct((nb, D_MODEL), f32)],
        input_output_aliases=aliases,
        compiler_params=pltpu.CompilerParams(dimension_semantics=("arbitrary",),
                                             vmem_limit_bytes=VMEM_LIMIT),
        name="finish",
    )(*args)


def _block_diag_pairs(w):
    per = MXU_DIM // HEAD_A
    w4 = w.reshape(N_HEADS_A // per, per, HEAD_A, HEAD_A)
    blocks = jnp.where(jnp.eye(per, dtype=bool)[None, :, None, :, None], w4[:, :, :, None, :], 0.0)
    return blocks.reshape(N_HEADS_A // per, MXU_DIM, MXU_DIM)


def _time_major(x):
    x = jnp.swapaxes(x, 0, 1)
    return x.reshape((x.shape[0] * x.shape[1],) + x.shape[2:])


def _batch_major(x, b):
    return jnp.swapaxes(x.reshape(x.shape[0] // b, b, x.shape[1]), 0, 1)


def _routing_tables(counts, n_tiles):
    tiles = (counts + FFN_ROWS - 1) // FFN_ROWS
    tile_ends = jnp.cumsum(tiles)
    off = (tile_ends - tiles) * FFN_ROWS
    n_valid = tile_ends[-1:]
    tile_ids = jnp.minimum(jnp.arange(n_tiles, dtype=i32), n_valid - 1)
    tile_e = jnp.sum((tile_ids[:, None] >= tile_ends[None, :]).astype(i32), axis=1)
    cand = jnp.where(tiles > 0, jnp.arange(N_EXPERTS, dtype=i32), N_EXPERTS)
    suffix_min = lax.cummin(cand[::-1])[::-1]
    nxt = jnp.concatenate([suffix_min[1:], jnp.full((1,), N_EXPERTS, i32)])
    onehot = (tile_e[:, None] == jnp.arange(N_EXPERTS, dtype=i32)[None, :]).astype(i32)
    at_tile = lambda per_expert: jnp.sum(onehot * per_expert[None, :].astype(i32), axis=1)
    tile_next = at_tile(jnp.where(nxt < N_EXPERTS, nxt, -1))
    tile_rows = jnp.clip(at_tile(counts) - (tile_ids - at_tile(tile_ends - tiles)) * FFN_ROWS, 0, FFN_ROWS)
    tile_group_end = at_tile(tile_ends)
    return off.astype(i32), (tile_e.astype(i32), tile_next.astype(i32), tile_group_end.astype(i32),
                             tile_rows.astype(i32), n_valid.astype(i32))


def _tile_tables(full, tile0, n):
    tile_e, tile_next, tile_group_end, tile_rows = (a[tile0:tile0 + n] for a in full[:4])
    n_valid = full[4]
    first = jnp.concatenate([jnp.ones((1,), i32), (tile_e[1:] != tile_e[:-1]).astype(i32)])
    slot = (jnp.cumsum(first) - 1) % 2
    nxt = jnp.where(tile_group_end < tile0 + n, tile_next, -1)
    return (tile_e, first, slot.astype(i32), nxt.astype(i32), tile_rows, jnp.clip(n_valid - tile0, 0, n))


def _layer(xp, xs_tm, pp, ps_tm, state_h, state_conv, state_pool, lw):
    (norm_mix_pre, w_in, conv_w, conv_b, w_rgate, b_rgate, w_igate, b_igate, lru_lambda,
     pool_w, pool_b, pool_scale, norm_group_a, norm_group_b, w_out, norm_mix_post,
     norm_ffn_pre, w_router, b_router, w_gate, b_gate, w_up, b_up, w_down, b_down, norm_ffn_post,
     w_ple, w_ple_gate, b_ple_gate, norm_ple) = lw
    b_p, s_p = xp.shape[0], xp.shape[1]
    n_p, n_s = b_p * s_p, xs_tm.shape[0]
    b_s = state_h.shape[0]
    row = lambda a: a.reshape(1, -1)

    w_gates = jnp.concatenate([_block_diag_pairs(w_rgate), _block_diag_pairs(w_igate)], axis=-1).astype(bf16)
    mix_w = (row(norm_mix_pre), w_in.astype(bf16), conv_w, row(conv_b),
             w_gates, row(b_rgate), row(b_igate), row(lru_lambda),
             pool_w.astype(bf16), row(pool_b), row(pool_scale), row(norm_group_a), row(norm_group_b),
             w_out.astype(bf16), row(norm_mix_post))

    zeros = lambda *s: jnp.zeros(s, f32)
    h1_p, hfin_p, cfin_p, pfin_p = _mixer(
        xp, zeros((CONV_W - 1) * b_p, W_A), zeros(POOL_BUF * b_p, W_B), zeros(b_p, W_A),
        mix_w, bt=b_p, start=0)
    h1_s, hfin_s, cfin_s, pfin_s = _mixer(
        xs_tm, _time_major(state_conv), _time_major(state_pool), state_h,
        mix_w, bt=b_s, start=PAST_LEN)
    t = n_p + n_s

    w_router_pad = jnp.zeros((D_MODEL, LANES), f32).at[:, :N_EXPERTS].set(w_router)
    b_router_pad = jnp.zeros((1, LANES), f32).at[0, :N_EXPERTS].set(b_router)
    tri = jnp.triu(jnp.ones((ROUTE_ROWS, ROUTE_ROWS), bf16), k=1)
    vpk, idx_t, rank_t, pw, cnt = _router(h1_p, h1_s, row(norm_ffn_pre), w_router_pad.astype(bf16),
                                          b_router_pad, tri)

    n_tiles = (t * TOP_K) // FFN_ROWS + N_EXPERTS
    off, tables = _routing_tables(cnt[:, 0], n_tiles)
    pos_flat = _positions(off, idx_t, rank_t).reshape(-1)

    step_tiles = SC_WORKERS * SC_CHUNK // FFN_ROWS
    tiles_1 = (n_tiles // 2) // step_tiles * step_tiles
    ys = None
    for tile0, n in ((0, tiles_1), (tiles_1, n_tiles - tiles_1)):
        xs = _sc_dispatch(vpk, pos_flat, tile0 * FFN_ROWS, n * FFN_ROWS)
        ys = _ffn(_tile_tables(tables, tile0, n), xs, n_tiles * FFN_ROWS, tile0,
                  w_gate, b_gate, w_up, b_up, w_down, b_down, prev=ys)
    fin_w = (row(norm_ffn_post), w_ple.astype(bf16), w_ple_gate.astype(bf16), row(b_ple_gate), row(norm_ple))
    a_blocks, b_blocks = n_p // FIN_ROWS, n_s // FIN_ROWS
    first = a_blocks // 2
    n1, n2 = first * FIN_ROWS, t - first * FIN_ROWS
    g1 = _sc_collect(ys, pos_flat, 0, n1).reshape(TOP_K, n1, HALF)
    out_p, _ = _finish(g1, h1_p, h1_s, pw, pp, ps_tm, *fin_w, a0=0, a_steps=first, b_steps=0)
    g2 = _sc_collect(ys, pos_flat, n1, n2).reshape(TOP_K, n2, HALF)
    out_p, out_s = _finish(g2, h1_p, h1_s, pw, pp, ps_tm, *fin_w, a0=first, a_steps=a_blocks - first,
                           b_steps=b_blocks, prev=out_p)

    states = (hfin_p, _batch_major(cfin_p, b_p), _batch_major(pfin_p, b_p),
              hfin_s, _batch_major(cfin_s, b_s), _batch_major(pfin_s, b_s))
    return out_p, out_s, states


def kernel(x_prompt, x_sample, state_rglru_h, state_rglru_conv, state_pool, p_prompt, p_sample, norm_mix_pre, w_in, conv_w, conv_b, w_rgate, b_rgate, w_igate, b_igate, lru_lambda, pool_w, pool_b, pool_scale, norm_group_a, norm_group_b, w_out, norm_mix_post, norm_ffn_pre, w_router, b_router, w_gate, b_gate, w_up, b_up, w_down, b_down, norm_ffn_post, w_ple, w_ple_gate, b_ple_gate, norm_ple):
    depth = w_in.shape[0]
    b_p, b_s = x_prompt.shape[0], x_sample.shape[0]
    per_layer = (norm_mix_pre, w_in, conv_w, conv_b, w_rgate, b_rgate, w_igate, b_igate, lru_lambda,
                 pool_w, pool_b, pool_scale, norm_group_a, norm_group_b, w_out, norm_mix_post,
                 norm_ffn_pre, w_router, b_router, w_gate, b_gate, w_up, b_up, w_down, b_down,
                 norm_ffn_post, w_ple, w_ple_gate, b_ple_gate, norm_ple)
    hp, hs = x_prompt, _time_major(x_sample)
    collected = []
    for i in range(depth):
        hp, hs, states = _layer(hp, hs, p_prompt[i], _time_major(p_sample[i]),
                                state_rglru_h[i], state_rglru_conv[i], state_pool[i],
                                tuple(w[i] for w in per_layer))
        collected.append(states)
    stacked = tuple(jnp.stack([c[j] for c in collected]) for j in range(6))
    return (hp, _batch_major(hs, b_s)) + stacked
```

```python
import dataclasses
import functools

import jax
import jax.numpy as jnp
from jax import lax
from jax.experimental import pallas as pl
from jax.experimental.pallas import tpu as pltpu
from jax.experimental.pallas import tpu_sc as plsc

D_MODEL = 1024
W_A = 512
W_B = 512
N_HEADS_A = 8
HEAD_A = W_A // N_HEADS_A
CONV_W = 4
LRU_C = 8.0
POOL_WINDOWS = (2, 4, 8, 16)
GROUP_B = W_B // len(POOL_WINDOWS)
POOL_BUF = max(POOL_WINDOWS) - 1
N_EXPERTS = 32
TOP_K = 4
D_FF = 1024
SWIGLU_LIMIT = 7.0
SWIGLU_ALPHA = 1.702
PLE_DIM = 256
EPS = 1e-6
PAST_LEN = 16384

LANES = 128
MXU_DIM = 256
HALF = D_MODEL // 2
SC_CORES = 2
SC_SUBCORES = 16
SC_WORKERS = SC_CORES * SC_SUBCORES
SC_LANES = 16
SC_CHUNK = 64

MIX_ROWS = 512
ROUTE_ROWS = 512
FFN_ROWS = 512
FFN_SUB = 256
FIN_ROWS = 512
FIRST_SHARE = 4
W_SLOTS = 3
VMEM_LIMIT = 56 * 1024 * 1024

f32 = jnp.float32
bf16 = jnp.bfloat16
u32 = jnp.uint32
i32 = jnp.int32


def _rms(x, g):
    return x * lax.rsqrt(jnp.mean(x * x, axis=-1, keepdims=True) + EPS) * g


def _dot(a, b):
    return jnp.dot(a, b, preferred_element_type=f32)


def _pack_rows(x):
    bits = lax.bitcast_convert_type(x.astype(bf16).astype(f32), u32)
    return (bits[:, HALF:] & jnp.uint32(0xFFFF0000)) | (bits[:, :HALF] >> 16)


def _unpack_rows(p):
    lo = lax.bitcast_convert_type(p << 16, f32)
    hi = lax.bitcast_convert_type(p & jnp.uint32(0xFFFF0000), f32)
    return jnp.concatenate([lo, hi], axis=-1)


def _mixer_kernel(x_ref, conv0_ref, pool0_ref, h0_ref, g_pre_ref, w_in_ref, conv_w_ref, conv_b_ref,
                  w_gates_ref, br_ref, bi_ref, lam_ref,
                  pool_w_ref, pool_b_ref, pool_scale_ref, ng_a_ref, ng_b_ref, w_out_ref, g_post_ref,
                  h1_ref, hfin_ref, convfin_ref, poolfin_ref,
                  h_s, conv_s, pool_s, a_s, b_s, hs_s, *, bt, tt, start):
    step = pl.program_id(0)
    rows = bt * tt

    @pl.when(step == 0)
    def _():
        h_s[...] = h0_ref[...]
        conv_s[...] = conv0_ref[...]
        pool_s[...] = pool0_ref[...]

    if x_ref.ndim == 3:
        x = pltpu.einshape("btd->tbd", x_ref[...]).reshape(rows, D_MODEL)
    else:
        x = x_ref[...]
    u = _rms(x, g_pre_ref[...]).astype(bf16)
    z = _dot(u, w_in_ref[...])
    xa, ga, xb = z[:, :W_A], z[:, W_A:2 * W_A], z[:, 2 * W_A:]

    ext_a = jnp.concatenate([conv_s[...], xa], axis=0)
    xc = conv_b_ref[...]
    for k in range(CONV_W):
        xc = xc + ext_a[k * bt:k * bt + rows] * conv_w_ref[k:k + 1, :]
    conv_s[...] = ext_a[rows:]

    xc16 = xc.astype(bf16)
    gates = [_dot(xc16[:, j * MXU_DIM:(j + 1) * MXU_DIM], w_gates_ref[j]) for j in range(W_A // MXU_DIM)]
    r = jax.nn.sigmoid(jnp.concatenate([gj[:, :MXU_DIM] for gj in gates], axis=-1) + br_ref[...])
    ig = jax.nn.sigmoid(jnp.concatenate([gj[:, MXU_DIM:] for gj in gates], axis=-1) + bi_ref[...])
    lam = lam_ref[...]
    softplus_neg = jnp.maximum(-lam, 0.0) + jnp.log1p(jnp.exp(-jnp.abs(lam)))
    log_a = (-LRU_C) * r * softplus_neg
    a_s[...] = jnp.exp(log_a)
    th = jnp.tanh(log_a)
    v = (-2.0 * th) / (1.0 - th)
    b_s[...] = (v * lax.rsqrt(jnp.maximum(v, 1.17549435e-38))) * (ig * xc)

    def scan_step(t, h):
        sl = pl.ds(pl.multiple_of(t * bt, bt), bt)
        h = a_s[sl, :] * h + b_s[sl, :]
        hs_s[sl, :] = h
        return h

    h_last = lax.fori_loop(0, tt, scan_step, h_s[...], unroll=True)
    h_s[...] = h_last
    ya = hs_s[...] * jax.nn.gelu(ga)

    ext_b = jnp.concatenate([pool_s[...], xb], axis=0)
    pool_s[...] = ext_b[rows:]
    s2 = ext_b[bt:, :] + ext_b[:-bt, :]
    s4 = s2[2 * bt:, GROUP_B:] + s2[:-2 * bt, GROUP_B:]
    s8 = s4[4 * bt:, GROUP_B:] + s4[:-4 * bt, GROUP_B:]
    s16 = s8[8 * bt:, GROUP_B:] + s8[:-8 * bt, GROUP_B:]
    wins = (s2[14 * bt:, :GROUP_B], s4[12 * bt:, :GROUP_B], s8[8 * bt:, :GROUP_B], s16)
    t_idx = lax.broadcasted_iota(i32, (rows, GROUP_B), 0) // bt
    pos1 = (t_idx + (step * tt + start + 1)).astype(f32)
    yb_parts = []
    for g, w in enumerate(POOL_WINDOWS):
        cnt = jnp.minimum(jnp.float32(w), pos1)
        d = wins[g] / cnt - xb[:, g * GROUP_B:(g + 1) * GROUP_B]
        yb_parts.append(_dot(d.astype(bf16), pool_w_ref[g]))
    yb = (jnp.concatenate(yb_parts, axis=-1) + pool_b_ref[...]) * pool_scale_ref[...]

    na = _rms(ya, ng_a_ref[...])
    nb = _rms(yb, ng_b_ref[...])
    m = _dot(jnp.concatenate([na, nb], axis=-1).astype(bf16), w_out_ref[...])
    h1_ref[...] = x + _rms(m, g_post_ref[...])

    hfin_ref[...] = h_last
    convfin_ref[...] = conv_s[...]
    poolfin_ref[...] = pool_s[...]


def _full(shape):
    return pl.BlockSpec(shape, lambda i, *_: (0,) * len(shape))


def _mixer(x, conv0, pool0, h0, wts, *, bt, start):
    tt = MIX_ROWS // bt
    rows = bt * tt
    if x.ndim == 3:
        assert x.shape[0] == bt and x.shape[1] % tt == 0
        n_steps = x.shape[1] // tt
        x_spec = pl.BlockSpec((bt, tt, D_MODEL), lambda i: (0, i, 0))
    else:
        assert x.shape[0] % rows == 0
        n_steps = x.shape[0] // rows
        x_spec = pl.BlockSpec((rows, D_MODEL), lambda i: (i, 0))
    kern = functools.partial(_mixer_kernel, bt=bt, tt=tt, start=start)
    small = [conv0, pool0, h0] + list(wts)
    return pl.pallas_call(
        kern,
        grid=(n_steps,),
        in_specs=[x_spec] + [_full(a.shape) for a in small],
        out_specs=[pl.BlockSpec((rows, D_MODEL), lambda i: (i, 0)),
                   _full((bt, W_A)), _full(((CONV_W - 1) * bt, W_A)), _full((POOL_BUF * bt, W_B))],
        out_shape=[jax.ShapeDtypeStruct((n_steps * rows, D_MODEL), f32),
                   jax.ShapeDtypeStruct((bt, W_A), f32),
                   jax.ShapeDtypeStruct(((CONV_W - 1) * bt, W_A), f32),
                   jax.ShapeDtypeStruct((POOL_BUF * bt, W_B), f32)],
        scratch_shapes=[pltpu.VMEM((bt, W_A), f32),
                        pltpu.VMEM(((CONV_W - 1) * bt, W_A), f32),
                        pltpu.VMEM((POOL_BUF * bt, W_B), f32),
                        pltpu.VMEM((rows, W_A), f32),
                        pltpu.VMEM((rows, W_A), f32),
                        pltpu.VMEM((rows, W_A), f32)],
        compiler_params=pltpu.CompilerParams(dimension_semantics=("arbitrary",),
                                             vmem_limit_bytes=VMEM_LIMIT),
        name="mixer",
    )(x, *small)


def _two_groups(step, first_steps, a_ref, b_ref):
    return jnp.where(step < first_steps, a_ref[...], b_ref[...])


def _router_kernel(h1a_ref, h1b_ref, g_ref, wr_ref, br_ref, tri_ref,
                   vpk_ref, idx_ref, rank_ref, pw_ref, cnt_ref, carry_s, *, first_steps):
    step = pl.program_id(0)
    rows = h1a_ref.shape[0]

    @pl.when(step == 0)
    def _():
        carry_s[...] = jnp.zeros_like(carry_s)

    v = _rms(_two_groups(step, first_steps, h1a_ref, h1b_ref), g_ref[...])
    vpk_ref[...] = _pack_rows(v)
    logits = _dot(v.astype(bf16), wr_ref[...]) + br_ref[...]
    lt = jnp.transpose(logits)[:N_EXPERTS, :]

    eio = lax.broadcasted_iota(i32, (N_EXPERTS, rows), 0).astype(f32)
    work = lt
    vals, idxs, sels = [], [], []
    for _ in range(TOP_K):
        m = jnp.max(work, axis=0, keepdims=True)
        ik = jnp.min(jnp.where(work == m, eio, float(N_EXPERTS)), axis=0, keepdims=True)
        sel = eio == ik
        vals.append(m)
        idxs.append(ik)
        sels.append(sel)
        work = jnp.where(sel, -jnp.inf, work)
    exps = [jnp.exp(val - vals[0]) for val in vals]
    denom = exps[0] + exps[1] + exps[2] + exps[3]
    probs = [e / denom for e in exps]

    chosen = sels[0] | sels[1] | sels[2] | sels[3]
    onehot = jnp.where(chosen, 1.0, 0.0)
    before = _dot(onehot.astype(bf16), tri_ref[...])
    base = carry_s[...] + before
    ranks = [jnp.sum(jnp.where(sel, base, 0.0), axis=0, keepdims=True) for sel in sels]
    carry_s[...] = carry_s[...] + jnp.sum(onehot, axis=1, keepdims=True)

    idx_ref[...] = jnp.concatenate(idxs, axis=0).astype(i32)
    rank_ref[...] = jnp.concatenate(ranks, axis=0).astype(i32)
    pad = jnp.zeros((LANES - TOP_K, rows), f32)
    pw_ref[...] = jnp.transpose(jnp.concatenate(probs + [pad], axis=0))
    cnt_ref[...] = carry_s[:, :LANES].astype(i32)


def _group_specs(rows, width, na, nb):
    first = na // rows
    return (pl.BlockSpec((rows, width), lambda i: (jnp.minimum(i, first - 1), 0)),
            pl.BlockSpec((rows, width), lambda i: (jnp.maximum(i - first, 0), 0)))


def _router(h1a, h1b, g, wr, br, tri):
    na, nb = h1a.shape[0], h1b.shape[0]
    t = na + nb
    rows = ROUTE_ROWS
    assert na % rows == 0 and nb % rows == 0
    return pl.pallas_call(
        functools.partial(_router_kernel, first_steps=na // rows),
        grid=(t // rows,),
        in_specs=[*_group_specs(rows, D_MODEL, na, nb),
                  _full(g.shape), _full(wr.shape), _full(br.shape), _full(tri.shape)],
        out_specs=[pl.BlockSpec((rows, HALF), lambda i: (i, 0)),
                   pl.BlockSpec((TOP_K, rows), lambda i: (0, i)),
                   pl.BlockSpec((TOP_K, rows), lambda i: (0, i)),
                   pl.BlockSpec((rows, LANES), lambda i: (i, 0)),
                   _full((N_EXPERTS, LANES))],
        out_shape=[jax.ShapeDtypeStruct((t, HALF), u32),
                   jax.ShapeDtypeStruct((TOP_K, t), i32),
                   jax.ShapeDtypeStruct((TOP_K, t), i32),
                   jax.ShapeDtypeStruct((t, LANES), f32),
                   jax.ShapeDtypeStruct((N_EXPERTS, LANES), i32)],
        scratch_shapes=[pltpu.VMEM((N_EXPERTS, rows), f32)],
        compiler_params=pltpu.CompilerParams(dimension_semantics=("arbitrary",),
                                             vmem_limit_bytes=VMEM_LIMIT),
        name="router",
    )(h1a, h1b, g, wr, br, tri)


def _positions_kernel(off_ref, idx_ref, rank_ref, pos_ref):
    idx = idx_ref[...]
    pos = rank_ref[...]
    for e in range(N_EXPERTS):
        pos = pos + jnp.where(idx == e, off_ref[e], 0)
    pos_ref[...] = pos


def _positions(off, idx_t, rank_t):
    spec = pl.BlockSpec(idx_t.shape, lambda i, *_: (0, 0))
    return pl.pallas_call(
        _positions_kernel,
        grid_spec=pltpu.PrefetchScalarGridSpec(num_scalar_prefetch=1, grid=(1,),
                                               in_specs=[spec, spec], out_specs=spec),
        out_shape=jax.ShapeDtypeStruct(idx_t.shape, i32),
        name="positions",
    )(off, idx_t, rank_t)


def _sc_mesh():
    return plsc.VectorSubcoreMesh(core_axis_name="core", subcore_axis_name="subcore")


def _sc_params():
    return dataclasses.replace(pltpu.CompilerParams(), needs_layout_passes=False)


def _sc_worker():
    return lax.axis_index("subcore") * SC_CORES + lax.axis_index("core")


def _sc_copy_rows(src_hbm, idx_v, dst_hbm, dst_row0, n, rows_v, gsem, wsem):
    n_ch = n // SC_CHUNK
    n_pair = n_ch // 2

    def gather(c, b):
        o = pl.multiple_of(c * SC_CHUNK, SC_CHUNK)
        return pltpu.make_async_copy(src_hbm.at[idx_v.at[pl.ds(o, SC_CHUNK)]], rows_v.at[b], gsem.at[b])

    def write(c, b):
        o = pl.multiple_of(c * SC_CHUNK, SC_CHUNK)
        return pltpu.make_async_copy(rows_v.at[b], dst_hbm.at[pl.ds(dst_row0 + o, SC_CHUNK)], wsem.at[b])

    gather(0, 0).start()

    @pl.loop(0, n_pair)
    def _(i):
        c0 = 2 * i
        gather(c0, 0).wait()

        @pl.when(i > 0)
        def _():
            write(c0 - 1, 1).wait()

        gather(c0 + 1, 1).start()
        write(c0, 0).start()
        gather(c0 + 1, 1).wait()
        write(c0, 0).wait()

        @pl.when(c0 + 2 < n_ch)
        def _():
            gather(c0 + 2, 0).start()

        write(c0 + 1, 1).start()

    if n_ch % 2:
        gather(n_ch - 1, 0).wait()
        write(n_ch - 1, 0).start()
        write(n_ch - 1, 0).wait()
    if n_pair:
        write(2 * n_pair - 1, 1).wait()


def _sc_dispatch(vpk, pos_flat, row0, n_out):
    n_tok = vpk.shape[0]
    per_w = n_out // SC_WORKERS
    assert n_out % (SC_WORKERS * SC_CHUNK) == 0 and n_tok % SC_LANES == 0 and row0 % SC_LANES == 0

    @pl.kernel(out_type=jax.ShapeDtypeStruct((n_out, HALF), u32), mesh=_sc_mesh(),
               compiler_params=_sc_params(), name="dispatch",
               scratch_types=[pltpu.VMEM((per_w,), i32), pltpu.VMEM((n_tok,), i32),
                              pltpu.VMEM((2, SC_CHUNK, HALF), u32),
                              pltpu.SemaphoreType.DMA((2,)), pltpu.SemaphoreType.DMA((2,))])
    def k(v_hbm, p_hbm, o_hbm, src_v, pos_v, rows_v, gsem, wsem):
        out0 = _sc_worker() * per_w
        lo = row0 + out0

        @pl.loop(0, per_w // SC_LANES)
        def _(i):
            o = pl.multiple_of(i * SC_LANES, SC_LANES)
            src_v[pl.ds(o, SC_LANES)] = lax.iota(i32, SC_LANES) + lax.rem(lo + o, n_tok)

        @pl.loop(0, TOP_K)
        def _(kk):
            pltpu.sync_copy(p_hbm.at[pl.ds(kk * n_tok, n_tok)], pos_v)

            @plsc.parallel_loop(0, n_tok, step=SC_LANES, unroll=8)
            def _(o):
                p = pos_v[pl.ds(pl.multiple_of(o, SC_LANES), SC_LANES)] - lo
                mine = (p >= 0) & (p < per_w)
                tok = lax.iota(i32, SC_LANES) + o
                plsc.store_scatter(src_v, [jnp.where(mine, p, 0)], tok, mask=mine)

        _sc_copy_rows(v_hbm, src_v, o_hbm, out0, per_w, rows_v, gsem, wsem)

    return k(vpk, pos_flat)


def _sc_collect(ys, pos_flat, t0, n):
    n_tok = pos_flat.shape[0] // TOP_K
    per_w = TOP_K * n // SC_WORKERS
    per_k = SC_WORKERS // TOP_K
    assert (TOP_K * n) % (SC_WORKERS * SC_CHUNK) == 0 and t0 % 8 == 0

    @pl.kernel(out_type=jax.ShapeDtypeStruct((TOP_K * n, HALF), u32), mesh=_sc_mesh(),
               compiler_params=_sc_params(), name="collect",
               scratch_types=[pltpu.VMEM((per_w,), i32), pltpu.VMEM((2, SC_CHUNK, HALF), u32),
                              pltpu.SemaphoreType.DMA((2,)), pltpu.SemaphoreType.DMA((2,))])
    def k(y_hbm, p_hbm, o_hbm, pos_v, rows_v, gsem, wsem):
        w = _sc_worker()
        src = lax.div(w, per_k) * n_tok + t0 + lax.rem(w, per_k) * per_w
        pltpu.sync_copy(p_hbm.at[pl.ds(pl.multiple_of(src, 8), per_w)], pos_v)
        _sc_copy_rows(y_hbm, pos_v, o_hbm, w * per_w, per_w, rows_v, gsem, wsem)

    return k(ys, pos_flat)


def _ffn_kernel(te_ref, tf_ref, slot_ref, next1_ref, next2_ref, rows_ref, nv_ref,
                xs_ref, wg_hbm, bg_ref, wu_hbm, bu_ref, wd_hbm, bd_ref, *rest):
    ys_ref, wbuf, wg_s, wu_s, wd_s, wsem = rest[-6:]
    step = pl.program_id(0)
    w_hbm = (wg_hbm, wu_hbm, wd_hbm)
    w_bf16 = (wg_s, wu_s, wd_s)

    def fetch(e, slot):
        return [pltpu.make_async_copy(w_hbm[m].at[e], wbuf.at[slot, m], wsem.at[slot, m]) for m in range(3)]

    @pl.when(step < nv_ref[0])
    def _():
        @pl.when(tf_ref[step] == 1)
        def _():
            slot = slot_ref[step]

            @pl.when(step == 0)
            def _():
                for cp in fetch(te_ref[0], 0):
                    cp.start()

                @pl.when(next1_ref[0] >= 0)
                def _():
                    for cp in fetch(next1_ref[0], 1):
                        cp.start()

            for cp in fetch(te_ref[step], slot):
                cp.wait()
            for s in range(W_SLOTS):
                @pl.when(slot == s)
                def _():
                    for m in range(3):
                        w_bf16[m][...] = wbuf[s, m].astype(bf16)

            @pl.when(next2_ref[step] >= 0)
            def _():
                for cp in fetch(next2_ref[step], lax.rem(slot + W_SLOTS - 1, W_SLOTS)):
                    cp.start()

        def rows_block(rb):
            rs = slice(rb * FFN_SUB, (rb + 1) * FFN_SUB)
            x = _unpack_rows(xs_ref[rs, :]).astype(bf16)
            g = jnp.minimum(_dot(x, wg_s[...]) + bg_ref[0], SWIGLU_LIMIT)
            u = jnp.clip(_dot(x, wu_s[...]) + bu_ref[0], -SWIGLU_LIMIT, SWIGLU_LIMIT)
            hid = (u + 1.0) * (g * jax.nn.sigmoid(SWIGLU_ALPHA * g))
            y = _dot(hid.astype(bf16), wd_s[...]) + bd_ref[0]
            ys_ref[rs, :] = _pack_rows(y)

        @pl.when(rows_ref[step] > FFN_SUB)
        def _():
            rows_block(0)
            rows_block(1)

        @pl.when(rows_ref[step] <= FFN_SUB)
        def _():
            rows_block(0)
            ys_ref[FFN_SUB:, :] = jnp.zeros((FFN_ROWS - FFN_SUB, HALF), u32)

    @pl.when(step >= nv_ref[0])
    def _():
        ys_ref[...] = jnp.zeros_like(ys_ref)


def _ffn(tables, xs, n_rows_total, tile0, w_gate, b_gate, w_up, b_up, w_down, b_down, prev=None):
    n_tiles = xs.shape[0] // FFN_ROWS
    assert D_FF == D_MODEL
    assert FFN_ROWS == 2 * FFN_SUB
    assert W_SLOTS == 3

    def row_map(i, te, tf, sl, n1, n2, rw, nv):
        return (jnp.minimum(i, jnp.maximum(nv[0] - 1, 0)), 0)

    def b_map(i, te, *_):
        return (te[i], 0, 0)

    w_spec = pl.BlockSpec(memory_space=pl.ANY)
    b_spec = pl.BlockSpec((1, 1, D_FF), b_map)
    in_specs = [pl.BlockSpec((FFN_ROWS, HALF), row_map), w_spec, b_spec, w_spec, b_spec, w_spec, b_spec]
    args = [xs, w_gate, b_gate.reshape(N_EXPERTS, 1, D_FF), w_up, b_up.reshape(N_EXPERTS, 1, D_FF),
            w_down, b_down.reshape(N_EXPERTS, 1, D_MODEL)]
    aliases = {}
    if prev is not None:
        in_specs.append(pl.BlockSpec(memory_space=pl.ANY))
        aliases = {len(tables) + len(args): 0}
        args.append(prev)
    return pl.pallas_call(
        _ffn_kernel,
        grid_spec=pltpu.PrefetchScalarGridSpec(
            num_scalar_prefetch=len(tables),
            grid=(n_tiles,),
            in_specs=in_specs,
            out_specs=pl.BlockSpec((FFN_ROWS, HALF), lambda i, *_: (tile0 + i, 0)),
            scratch_shapes=[pltpu.VMEM((W_SLOTS, 3, D_MODEL, D_FF), f32),
                            pltpu.VMEM((D_MODEL, D_FF), bf16),
                            pltpu.VMEM((D_MODEL, D_FF), bf16),
                            pltpu.VMEM((D_FF, D_MODEL), bf16),
                            pltpu.SemaphoreType.DMA((W_SLOTS, 3))]),
        out_shape=jax.ShapeDtypeStruct((n_rows_total, HALF), u32),
        input_output_aliases=aliases,
        compiler_params=pltpu.CompilerParams(dimension_semantics=("arbitrary",),
                                             vmem_limit_bytes=VMEM_LIMIT),
        name="experts",
    )(*tables, *args)


def _finish_kernel(g_ref, h1a_ref, h1b_ref, pw_ref, pa_ref, pb_ref, g_post_ref, w_ple_ref, w_pg_ref, b_pg_ref,
                   g_ple_ref, *rest, first_steps):
    outa_ref, outb_ref = rest[-2:]
    step = pl.program_id(0)
    rows = h1a_ref.shape[0]
    pw = pw_ref[...]
    f = jnp.zeros((rows, D_MODEL), f32)
    for k in range(TOP_K):
        f = f + pw[:, k:k + 1] * _unpack_rows(g_ref[k])
    h2 = _two_groups(step, first_steps, h1a_ref, h1b_ref) + _rms(f, g_post_ref[...])
    gate = jax.nn.sigmoid(_dot(h2.astype(bf16), w_pg_ref[...]) + b_pg_ref[...])
    bt, tt = pa_ref.shape[0], pa_ref.shape[1]
    pin_a = pltpu.einshape("btd->tbd", pa_ref[...]).reshape(rows, PLE_DIM)
    pin = jnp.where(step < first_steps, pin_a, pb_ref[...]).astype(bf16)
    pe = _dot(pin, w_ple_ref[...]) * gate
    out = h2 + _rms(pe, g_ple_ref[...])

    @pl.when(step < first_steps)
    def _():
        outa_ref[...] = pltpu.einshape("tbd->btd", out.reshape(tt, bt, D_MODEL))

    @pl.when(step >= first_steps)
    def _():
        outb_ref[...] = out


def _finish(g, h1a, h1b, pw, pa, pb, g_post, w_ple, w_pg, b_pg, g_ple, *, a0, a_steps, b_steps, prev=None):
    na, nb = h1a.shape[0], h1b.shape[0]
    rows = FIN_ROWS
    bt = pa.shape[0]
    tt = rows // bt
    assert na % rows == 0 and nb % rows == 0 and a_steps >= 1 and rows % bt == 0

    def spec_a(width):
        return pl.BlockSpec((rows, width), lambda i: (a0 + jnp.minimum(i, a_steps - 1), 0))

    def spec_a3(width):
        return pl.BlockSpec((bt, tt, width), lambda i: (0, a0 + jnp.minimum(i, a_steps - 1), 0))

    def spec_b(width):
        return pl.BlockSpec((rows, width), lambda i: (jnp.maximum(i - a_steps, 0), 0))

    in_specs = [pl.BlockSpec((TOP_K, rows, HALF), lambda i: (0, i, 0)),
                spec_a(D_MODEL), spec_b(D_MODEL),
                pl.BlockSpec((rows, LANES), lambda i: (a0 + i, 0)),
                spec_a3(PLE_DIM), spec_b(PLE_DIM),
                _full(g_post.shape), _full(w_ple.shape), _full(w_pg.shape),
                _full(b_pg.shape), _full(g_ple.shape)]
    args = [g, h1a, h1b, pw, pa, pb, g_post, w_ple, w_pg, b_pg, g_ple]
    aliases = {}
    if prev is not None:
        in_specs.append(pl.BlockSpec(memory_space=pl.ANY))
        aliases = {len(args): 0}
        args.append(prev)
    return pl.pallas_call(
        functools.partial(_finish_kernel, first_steps=a_steps),
        grid=(a_steps + b_steps,),
        in_specs=in_specs,
        out_specs=[spec_a3(D_MODEL), spec_b(D_MODEL)],
        out_shape=[jax.ShapeDtypeStruct((bt, na // bt, D_MODEL), f32), jax.ShapeDtypeStruct((nb, D_MODEL), f32)],
        input_output_aliases=aliases,
        compiler_params=pltpu.CompilerParams(dimension_semantics=("arbitrary",),
                                             vmem_limit_bytes=VMEM_LIMIT),
        name="finish",
    )(*args)


def _block_diag_pairs(w):
    per = MXU_DIM // HEAD_A
    w4 = w.reshape(N_HEADS_A // per, per, HEAD_A, HEAD_A)
    blocks = jnp.where(jnp.eye(per, dtype=bool)[None, :, None, :, None], w4[:, :, :, None, :], 0.0)
    return blocks.reshape(N_HEADS_A // per, MXU_DIM, MXU_DIM)


def _time_major(x):
    x = jnp.swapaxes(x, 0, 1)
    return x.reshape((x.shape[0] * x.shape[1],) + x.shape[2:])


def _batch_major(x, b):
    return jnp.swapaxes(x.reshape(x.shape[0] // b, b, x.shape[1]), 0, 1)


def _routing_tables(counts, n_tiles):
    tiles = (counts + FFN_ROWS - 1) // FFN_ROWS
    tile_ends = jnp.cumsum(tiles)
    off = (tile_ends - tiles) * FFN_ROWS
    n_valid = tile_ends[-1:]
    tile_ids = jnp.minimum(jnp.arange(n_tiles, dtype=i32), n_valid - 1)
    tile_e = jnp.sum((tile_ids[:, None] >= tile_ends[None, :]).astype(i32), axis=1)
    cand = jnp.where(tiles > 0, jnp.arange(N_EXPERTS, dtype=i32), N_EXPERTS)
    suffix_min = lax.cummin(cand[::-1])[::-1]
    nxt = jnp.concatenate([suffix_min[1:], jnp.full((1,), N_EXPERTS, i32)])
    onehot = (tile_e[:, None] == jnp.arange(N_EXPERTS, dtype=i32)[None, :]).astype(i32)
    at_tile = lambda per_expert: jnp.sum(onehot * per_expert[None, :].astype(i32), axis=1)
    hop = (jnp.minimum(nxt, N_EXPERTS - 1)[:, None] == jnp.arange(N_EXPERTS, dtype=i32)[None, :]).astype(i32)
    nxt2 = jnp.where(nxt < N_EXPERTS, jnp.sum(hop * nxt[None, :], axis=1), N_EXPERTS)
    end_nxt = jnp.sum(hop * tile_ends[None, :].astype(i32), axis=1)
    as_expert = lambda e: jnp.where(e < N_EXPERTS, e, -1)
    tile_rows = jnp.clip(at_tile(counts) - (tile_ids - at_tile(tile_ends - tiles)) * FFN_ROWS, 0, FFN_ROWS)
    per_tile = (tile_e, at_tile(as_expert(nxt)), at_tile(as_expert(nxt2)),
                at_tile(tile_ends),
                at_tile(end_nxt),
                tile_rows)
    return off.astype(i32), tuple(a.astype(i32) for a in per_tile) + (n_valid.astype(i32),)


def _tile_tables(full, tile0, n):
    tile_e, next1, next2, start1, start2, tile_rows = (a[tile0:tile0 + n] for a in full[:6])
    n_valid = full[6]
    first = jnp.concatenate([jnp.ones((1,), i32), (tile_e[1:] != tile_e[:-1]).astype(i32)])
    slot = (jnp.cumsum(first) - 1) % W_SLOTS
    next1 = jnp.where(start1 < tile0 + n, next1, -1)
    next2 = jnp.where((next1 >= 0) & (start2 < tile0 + n), next2, -1)
    return (tile_e, first, slot.astype(i32), next1.astype(i32), next2.astype(i32), tile_rows,
            jnp.clip(n_valid - tile0, 0, n))


def _layer(xp, xs_tm, pp, ps_tm, state_h, state_conv, state_pool, lw):
    (norm_mix_pre, w_in, conv_w, conv_b, w_rgate, b_rgate, w_igate, b_igate, lru_lambda,
     pool_w, pool_b, pool_scale, norm_group_a, norm_group_b, w_out, norm_mix_post,
     norm_ffn_pre, w_router, b_router, w_gate, b_gate, w_up, b_up, w_down, b_down, norm_ffn_post,
     w_ple, w_ple_gate, b_ple_gate, norm_ple) = lw
    b_p, s_p = xp.shape[0], xp.shape[1]
    n_p, n_s = b_p * s_p, xs_tm.shape[0]
    b_s = state_h.shape[0]
    row = lambda a: a.reshape(1, -1)

    w_gates = jnp.concatenate([_block_diag_pairs(w_rgate), _block_diag_pairs(w_igate)], axis=-1).astype(bf16)
    mix_w = (row(norm_mix_pre), w_in.astype(bf16), conv_w, row(conv_b),
             w_gates, row(b_rgate), row(b_igate), row(lru_lambda),
             pool_w.astype(bf16), row(pool_b), row(pool_scale), row(norm_group_a), row(norm_group_b),
             w_out.astype(bf16), row(norm_mix_post))

    zeros = lambda *s: jnp.zeros(s, f32)
    h1_p, hfin_p, cfin_p, pfin_p = _mixer(
        xp, zeros((CONV_W - 1) * b_p, W_A), zeros(POOL_BUF * b_p, W_B), zeros(b_p, W_A),
        mix_w, bt=b_p, start=0)
    h1_s, hfin_s, cfin_s, pfin_s = _mixer(
        xs_tm, _time_major(state_conv), _time_major(state_pool), state_h,
        mix_w, bt=b_s, start=PAST_LEN)
    t = n_p + n_s

    w_router_pad = jnp.zeros((D_MODEL, LANES), f32).at[:, :N_EXPERTS].set(w_router)
    b_router_pad = jnp.zeros((1, LANES), f32).at[0, :N_EXPERTS].set(b_router)
    tri = jnp.triu(jnp.ones((ROUTE_ROWS, ROUTE_ROWS), bf16), k=1)
    vpk, idx_t, rank_t, pw, cnt = _router(h1_p, h1_s, row(norm_ffn_pre), w_router_pad.astype(bf16),
                                          b_router_pad, tri)

    n_tiles = (t * TOP_K) // FFN_ROWS + N_EXPERTS
    off, tables = _routing_tables(cnt[:, 0], n_tiles)
    pos_flat = _positions(off, idx_t, rank_t).reshape(-1)

    step_tiles = SC_WORKERS * SC_CHUNK // FFN_ROWS
    tiles_1 = (n_tiles // FIRST_SHARE) // step_tiles * step_tiles
    ys = None
    for tile0, n in ((0, tiles_1), (tiles_1, n_tiles - tiles_1)):
        xs = _sc_dispatch(vpk, pos_flat, tile0 * FFN_ROWS, n * FFN_ROWS)
        ys = _ffn(_tile_tables(tables, tile0, n), xs, n_tiles * FFN_ROWS, tile0,
                  w_gate, b_gate, w_up, b_up, w_down, b_down, prev=ys)
    fin_w = (row(norm_ffn_post), w_ple.astype(bf16), w_ple_gate.astype(bf16), row(b_ple_gate), row(norm_ple))
    a_blocks, b_blocks = n_p // FIN_ROWS, n_s // FIN_ROWS
    first = a_blocks // 2
    n1, n2 = first * FIN_ROWS, t - first * FIN_ROWS
    g1 = _sc_collect(ys, pos_flat, 0, n1).reshape(TOP_K, n1, HALF)
    out_p, _ = _finish(g1, h1_p, h1_s, pw, pp, ps_tm, *fin_w, a0=0, a_steps=first, b_steps=0)
    g2 = _sc_collect(ys, pos_flat, n1, n2).reshape(TOP_K, n2, HALF)
    out_p, out_s = _finish(g2, h1_p, h1_s, pw, pp, ps_tm, *fin_w, a0=first, a_steps=a_blocks - first,
                           b_steps=b_blocks, prev=out_p)

    states = (hfin_p, _batch_major(cfin_p, b_p), _batch_major(pfin_p, b_p),
              hfin_s, _batch_major(cfin_s, b_s), _batch_major(pfin_s, b_s))
    return out_p, out_s, states


def kernel(x_prompt, x_sample, state_rglru_h, state_rglru_conv, state_pool, p_prompt, p_sample, norm_mix_pre, w_in, conv_w, conv_b, w_rgate, b_rgate, w_igate, b_igate, lru_lambda, pool_w, pool_b, pool_scale, norm_group_a, norm_group_b, w_out, norm_mix_post, norm_ffn_pre, w_router, b_router, w_gate, b_gate, w_up, b_up, w_down, b_down, norm_ffn_post, w_ple, w_ple_gate, b_ple_gate, norm_ple):
    depth = w_in.shape[0]
    b_p, b_s = x_prompt.shape[0], x_sample.shape[0]
    per_layer = (norm_mix_pre, w_in, conv_w, conv_b, w_rgate, b_rgate, w_igate, b_igate, lru_lambda,
                 pool_w, pool_b, pool_scale, norm_group_a, norm_group_b, w_out, norm_mix_post,
                 norm_ffn_pre, w_router, b_router, w_gate, b_gate, w_up, b_up, w_down, b_down,
                 norm_ffn_post, w_ple, w_ple_gate, b_ple_gate, norm_ple)
    hp, hs = x_prompt, _time_major(x_sample)
    collected = []
    for i in range(depth):
        hp, hs, states = _layer(hp, hs, p_prompt[i], _time_major(p_sample[i]),
                                state_rglru_h[i], state_rglru_conv[i], state_pool[i],
                                tuple(w[i] for w in per_layer))
        collected.append(states)
    stacked = tuple(jnp.stack([c[j] for c in collected]) for j in range(6))
    return (hp, _batch_major(hs, b_s)) + stacked
```

```python
import dataclasses
import functools

import jax
import jax.numpy as jnp
from jax import lax
from jax.experimental import pallas as pl
from jax.experimental.pallas import tpu as pltpu
from jax.experimental.pallas import tpu_sc as plsc

D_MODEL = 1024
W_A = 512
W_B = 512
N_HEADS_A = 8
HEAD_A = W_A // N_HEADS_A
CONV_W = 4
LRU_C = 8.0
POOL_WINDOWS = (2, 4, 8, 16)
GROUP_B = W_B // len(POOL_WINDOWS)
POOL_BUF = max(POOL_WINDOWS) - 1
N_EXPERTS = 32
TOP_K = 4
D_FF = 1024
SWIGLU_LIMIT = 7.0
SWIGLU_ALPHA = 1.702
PLE_DIM = 256
EPS = 1e-6
PAST_LEN = 16384

LANES = 128
MXU_DIM = 256
HALF = D_MODEL // 2
SC_CORES = 2
SC_SUBCORES = 16
SC_WORKERS = SC_CORES * SC_SUBCORES
SC_LANES = 16
SC_CHUNK = 64

MIX_ROWS = 512
ROUTE_ROWS = 512
FFN_ROWS = 512
FFN_SUB = 256
FIN_ROWS = 512
FIRST_SHARE = 4
VMEM_LIMIT = 56 * 1024 * 1024

f32 = jnp.float32
bf16 = jnp.bfloat16
u32 = jnp.uint32
i32 = jnp.int32


def _rms(x, g):
    return x * lax.rsqrt(jnp.mean(x * x, axis=-1, keepdims=True) + EPS) * g


def _dot(a, b):
    return jnp.dot(a, b, preferred_element_type=f32)


def _pack_rows(x):
    bits = lax.bitcast_convert_type(x.astype(bf16).astype(f32), u32)
    return (bits[:, HALF:] & jnp.uint32(0xFFFF0000)) | (bits[:, :HALF] >> 16)


def _unpack_rows(p):
    lo = lax.bitcast_convert_type(p << 16, f32)
    hi = lax.bitcast_convert_type(p & jnp.uint32(0xFFFF0000), f32)
    return jnp.concatenate([lo, hi], axis=-1)


def _mixer_kernel(x_ref, conv0_ref, pool0_ref, h0_ref, g_pre_ref, w_in_ref, conv_w_ref, conv_b_ref,
                  w_gates_ref, br_ref, bi_ref, lam_ref,
                  pool_w_ref, pool_b_ref, pool_scale_ref, ng_a_ref, ng_b_ref, w_out_ref, g_post_ref,
                  h1_ref, hfin_ref, convfin_ref, poolfin_ref,
                  h_s, conv_s, pool_s, a_s, b_s, hs_s, *, bt, tt, start):
    step = pl.program_id(0)
    rows = bt * tt

    @pl.when(step == 0)
    def _():
        h_s[...] = h0_ref[...]
        conv_s[...] = conv0_ref[...]
        pool_s[...] = pool0_ref[...]

    if x_ref.ndim == 3:
        x = pltpu.einshape("btd->tbd", x_ref[...]).reshape(rows, D_MODEL)
    else:
        x = x_ref[...]
    u = _rms(x, g_pre_ref[...]).astype(bf16)
    z = _dot(u, w_in_ref[...])
    xa, ga, xb = z[:, :W_A], z[:, W_A:2 * W_A], z[:, 2 * W_A:]

    ext_a = jnp.concatenate([conv_s[...], xa], axis=0)
    xc = conv_b_ref[...]
    for k in range(CONV_W):
        xc = xc + ext_a[k * bt:k * bt + rows] * conv_w_ref[k:k + 1, :]
    conv_s[...] = ext_a[rows:]

    xc16 = xc.astype(bf16)
    gates = [_dot(xc16[:, j * MXU_DIM:(j + 1) * MXU_DIM], w_gates_ref[j]) for j in range(W_A // MXU_DIM)]
    r = jax.nn.sigmoid(jnp.concatenate([gj[:, :MXU_DIM] for gj in gates], axis=-1) + br_ref[...])
    ig = jax.nn.sigmoid(jnp.concatenate([gj[:, MXU_DIM:] for gj in gates], axis=-1) + bi_ref[...])
    lam = lam_ref[...]
    softplus_neg = jnp.maximum(-lam, 0.0) + jnp.log1p(jnp.exp(-jnp.abs(lam)))
    log_a = (-LRU_C) * r * softplus_neg
    a_s[...] = jnp.exp(log_a)
    th = jnp.tanh(log_a)
    v = (-2.0 * th) / (1.0 - th)
    b_s[...] = (v * lax.rsqrt(jnp.maximum(v, 1.17549435e-38))) * (ig * xc)

    def scan_step(t, h):
        sl = pl.ds(pl.multiple_of(t * bt, bt), bt)
        h = a_s[sl, :] * h + b_s[sl, :]
        hs_s[sl, :] = h
        return h

    h_last = lax.fori_loop(0, tt, scan_step, h_s[...], unroll=True)
    h_s[...] = h_last
    ya = hs_s[...] * jax.nn.gelu(ga)

    ext_b = jnp.concatenate([pool_s[...], xb], axis=0)
    pool_s[...] = ext_b[rows:]
    s2 = ext_b[bt:, :] + ext_b[:-bt, :]
    s4 = s2[2 * bt:, GROUP_B:] + s2[:-2 * bt, GROUP_B:]
    s8 = s4[4 * bt:, GROUP_B:] + s4[:-4 * bt, GROUP_B:]
    s16 = s8[8 * bt:, GROUP_B:] + s8[:-8 * bt, GROUP_B:]
    wins = (s2[14 * bt:, :GROUP_B], s4[12 * bt:, :GROUP_B], s8[8 * bt:, :GROUP_B], s16)
    t_idx = lax.broadcasted_iota(i32, (rows, GROUP_B), 0) // bt
    pos1 = (t_idx + (step * tt + start + 1)).astype(f32)
    d_parts = []
    for g, w in enumerate(POOL_WINDOWS):
        cnt = jnp.minimum(jnp.float32(w), pos1)
        d_parts.append(wins[g] / cnt - xb[:, g * GROUP_B:(g + 1) * GROUP_B])
    d = jnp.concatenate(d_parts, axis=-1).astype(bf16)
    yb_parts = [_dot(d[:, j * MXU_DIM:(j + 1) * MXU_DIM], pool_w_ref[j]) for j in range(W_B // MXU_DIM)]
    yb = (jnp.concatenate(yb_parts, axis=-1) + pool_b_ref[...]) * pool_scale_ref[...]

    na = _rms(ya, ng_a_ref[...])
    nb = _rms(yb, ng_b_ref[...])
    m = _dot(jnp.concatenate([na, nb], axis=-1).astype(bf16), w_out_ref[...])
    h1_ref[...] = x + _rms(m, g_post_ref[...])

    hfin_ref[...] = h_last
    convfin_ref[...] = conv_s[...]
    poolfin_ref[...] = pool_s[...]


def _full(shape):
    return pl.BlockSpec(shape, lambda i, *_: (0,) * len(shape))


def _mixer(x, conv0, pool0, h0, wts, *, bt, start):
    tt = MIX_ROWS // bt
    rows = bt * tt
    if x.ndim == 3:
        assert x.shape[0] == bt and x.shape[1] % tt == 0
        n_steps = x.shape[1] // tt
        x_spec = pl.BlockSpec((bt, tt, D_MODEL), lambda i: (0, i, 0))
    else:
        assert x.shape[0] % rows == 0
        n_steps = x.shape[0] // rows
        x_spec = pl.BlockSpec((rows, D_MODEL), lambda i: (i, 0))
    kern = functools.partial(_mixer_kernel, bt=bt, tt=tt, start=start)
    small = [conv0, pool0, h0] + list(wts)
    return pl.pallas_call(
        kern,
        grid=(n_steps,),
        in_specs=[x_spec] + [_full(a.shape) for a in small],
        out_specs=[pl.BlockSpec((rows, D_MODEL), lambda i: (i, 0)),
                   _full((bt, W_A)), _full(((CONV_W - 1) * bt, W_A)), _full((POOL_BUF * bt, W_B))],
        out_shape=[jax.ShapeDtypeStruct((n_steps * rows, D_MODEL), f32),
                   jax.ShapeDtypeStruct((bt, W_A), f32),
                   jax.ShapeDtypeStruct(((CONV_W - 1) * bt, W_A), f32),
                   jax.ShapeDtypeStruct((POOL_BUF * bt, W_B), f32)],
        scratch_shapes=[pltpu.VMEM((bt, W_A), f32),
                        pltpu.VMEM(((CONV_W - 1) * bt, W_A), f32),
                        pltpu.VMEM((POOL_BUF * bt, W_B), f32),
                        pltpu.VMEM((rows, W_A), f32),
                        pltpu.VMEM((rows, W_A), f32),
                        pltpu.VMEM((rows, W_A), f32)],
        compiler_params=pltpu.CompilerParams(dimension_semantics=("arbitrary",),
                                             vmem_limit_bytes=VMEM_LIMIT),
        name="mixer",
    )(x, *small)


def _two_groups(step, first_steps, a_ref, b_ref):
    return jnp.where(step < first_steps, a_ref[...], b_ref[...])


def _router_kernel(h1a_ref, h1b_ref, g_ref, wr_ref, br_ref, tri_ref,
                   vpk_ref, idx_ref, rank_ref, pw_ref, cnt_ref, carry_s, *, first_steps):
    step = pl.program_id(0)
    rows = h1a_ref.shape[0]

    @pl.when(step == 0)
    def _():
        carry_s[...] = jnp.zeros_like(carry_s)

    v = _rms(_two_groups(step, first_steps, h1a_ref, h1b_ref), g_ref[...])
    vpk_ref[...] = _pack_rows(v)
    logits = _dot(v.astype(bf16), wr_ref[...]) + br_ref[...]
    lt = jnp.transpose(logits)[:N_EXPERTS, :]

    eio = lax.broadcasted_iota(i32, (N_EXPERTS, rows), 0).astype(f32)
    work = lt
    vals, idxs, sels = [], [], []
    for _ in range(TOP_K):
        m = jnp.max(work, axis=0, keepdims=True)
        ik = jnp.min(jnp.where(work == m, eio, float(N_EXPERTS)), axis=0, keepdims=True)
        sel = eio == ik
        vals.append(m)
        idxs.append(ik)
        sels.append(sel)
        work = jnp.where(sel, -jnp.inf, work)
    exps = [jnp.exp(val - vals[0]) for val in vals]
    denom = exps[0] + exps[1] + exps[2] + exps[3]
    probs = [e / denom for e in exps]

    chosen = sels[0] | sels[1] | sels[2] | sels[3]
    onehot = jnp.where(chosen, 1.0, 0.0)
    before = _dot(onehot.astype(bf16), tri_ref[...])
    base = carry_s[...] + before
    ranks = [jnp.sum(jnp.where(sel, base, 0.0), axis=0, keepdims=True) for sel in sels]
    carry_s[...] = carry_s[...] + jnp.sum(onehot, axis=1, keepdims=True)

    idx_ref[...] = jnp.concatenate(idxs, axis=0).astype(i32)
    rank_ref[...] = jnp.concatenate(ranks, axis=0).astype(i32)
    pad = jnp.zeros((LANES - TOP_K, rows), f32)
    pw_ref[...] = jnp.transpose(jnp.concatenate(probs + [pad], axis=0))
    cnt_ref[...] = carry_s[:, :LANES].astype(i32)


def _group_specs(rows, width, na, nb):
    first = na // rows
    return (pl.BlockSpec((rows, width), lambda i: (jnp.minimum(i, first - 1), 0)),
            pl.BlockSpec((rows, width), lambda i: (jnp.maximum(i - first, 0), 0)))


def _router(h1a, h1b, g, wr, br, tri):
    na, nb = h1a.shape[0], h1b.shape[0]
    t = na + nb
    rows = ROUTE_ROWS
    assert na % rows == 0 and nb % rows == 0
    return pl.pallas_call(
        functools.partial(_router_kernel, first_steps=na // rows),
        grid=(t // rows,),
        in_specs=[*_group_specs(rows, D_MODEL, na, nb),
                  _full(g.shape), _full(wr.shape), _full(br.shape), _full(tri.shape)],
        out_specs=[pl.BlockSpec((rows, HALF), lambda i: (i, 0)),
                   pl.BlockSpec((TOP_K, rows), lambda i: (0, i)),
                   pl.BlockSpec((TOP_K, rows), lambda i: (0, i)),
                   pl.BlockSpec((rows, LANES), lambda i: (i, 0)),
                   _full((N_EXPERTS, LANES))],
        out_shape=[jax.ShapeDtypeStruct((t, HALF), u32),
                   jax.ShapeDtypeStruct((TOP_K, t), i32),
                   jax.ShapeDtypeStruct((TOP_K, t), i32),
                   jax.ShapeDtypeStruct((t, LANES), f32),
                   jax.ShapeDtypeStruct((N_EXPERTS, LANES), i32)],
        scratch_shapes=[pltpu.VMEM((N_EXPERTS, rows), f32)],
        compiler_params=pltpu.CompilerParams(dimension_semantics=("arbitrary",),
                                             vmem_limit_bytes=VMEM_LIMIT),
        name="router",
    )(h1a, h1b, g, wr, br, tri)


def _positions_kernel(off_ref, idx_ref, rank_ref, pos_ref):
    idx = idx_ref[...]
    pos = rank_ref[...]
    for e in range(N_EXPERTS):
        pos = pos + jnp.where(idx == e, off_ref[e], 0)
    pos_ref[...] = pos


def _positions(off, idx_t, rank_t):
    spec = pl.BlockSpec(idx_t.shape, lambda i, *_: (0, 0))
    return pl.pallas_call(
        _positions_kernel,
        grid_spec=pltpu.PrefetchScalarGridSpec(num_scalar_prefetch=1, grid=(1,),
                                               in_specs=[spec, spec], out_specs=spec),
        out_shape=jax.ShapeDtypeStruct(idx_t.shape, i32),
        name="positions",
    )(off, idx_t, rank_t)


def _sc_mesh():
    return plsc.VectorSubcoreMesh(core_axis_name="core", subcore_axis_name="subcore")


def _sc_params():
    return dataclasses.replace(pltpu.CompilerParams(), needs_layout_passes=False)


def _sc_worker():
    return lax.axis_index("subcore") * SC_CORES + lax.axis_index("core")


def _sc_copy_rows(src_hbm, idx_v, dst_hbm, dst_row0, n, rows_v, gsem, wsem):
    n_ch = n // SC_CHUNK
    n_pair = n_ch // 2

    def gather(c, b):
        o = pl.multiple_of(c * SC_CHUNK, SC_CHUNK)
        return pltpu.make_async_copy(src_hbm.at[idx_v.at[pl.ds(o, SC_CHUNK)]], rows_v.at[b], gsem.at[b])

    def write(c, b):
        o = pl.multiple_of(c * SC_CHUNK, SC_CHUNK)
        return pltpu.make_async_copy(rows_v.at[b], dst_hbm.at[pl.ds(dst_row0 + o, SC_CHUNK)], wsem.at[b])

    gather(0, 0).start()

    @pl.loop(0, n_pair)
    def _(i):
        c0 = 2 * i
        gather(c0, 0).wait()

        @pl.when(i > 0)
        def _():
            write(c0 - 1, 1).wait()

        gather(c0 + 1, 1).start()
        write(c0, 0).start()
        gather(c0 + 1, 1).wait()
        write(c0, 0).wait()

        @pl.when(c0 + 2 < n_ch)
        def _():
            gather(c0 + 2, 0).start()

        write(c0 + 1, 1).start()

    if n_ch % 2:
        gather(n_ch - 1, 0).wait()
        write(n_ch - 1, 0).start()
        write(n_ch - 1, 0).wait()
    if n_pair:
        write(2 * n_pair - 1, 1).wait()


def _sc_dispatch(vpk, pos_flat, row0, n_out):
    n_tok = vpk.shape[0]
    per_w = n_out // SC_WORKERS
    assert n_out % (SC_WORKERS * SC_CHUNK) == 0 and n_tok % SC_LANES == 0 and row0 % SC_LANES == 0

    @pl.kernel(out_type=jax.ShapeDtypeStruct((n_out, HALF), u32), mesh=_sc_mesh(),
               compiler_params=_sc_params(), name="dispatch",
               scratch_types=[pltpu.VMEM((per_w,), i32), pltpu.VMEM((n_tok,), i32),
                              pltpu.VMEM((2, SC_CHUNK, HALF), u32),
                              pltpu.SemaphoreType.DMA((2,)), pltpu.SemaphoreType.DMA((2,))])
    def k(v_hbm, p_hbm, o_hbm, src_v, pos_v, rows_v, gsem, wsem):
        out0 = _sc_worker() * per_w
        lo = row0 + out0

        @pl.loop(0, per_w // SC_LANES)
        def _(i):
            o = pl.multiple_of(i * SC_LANES, SC_LANES)
            src_v[pl.ds(o, SC_LANES)] = lax.iota(i32, SC_LANES) + lax.rem(lo + o, n_tok)

        @pl.loop(0, TOP_K)
        def _(kk):
            pltpu.sync_copy(p_hbm.at[pl.ds(kk * n_tok, n_tok)], pos_v)

            @plsc.parallel_loop(0, n_tok, step=SC_LANES, unroll=8)
            def _(o):
                p = pos_v[pl.ds(pl.multiple_of(o, SC_LANES), SC_LANES)] - lo
                mine = (p >= 0) & (p < per_w)
                tok = lax.iota(i32, SC_LANES) + o
                plsc.store_scatter(src_v, [jnp.where(mine, p, 0)], tok, mask=mine)

        _sc_copy_rows(v_hbm, src_v, o_hbm, out0, per_w, rows_v, gsem, wsem)

    return k(vpk, pos_flat)


def _sc_collect(ys, pos_flat, t0, n):
    n_tok = pos_flat.shape[0] // TOP_K
    per_w = TOP_K * n // SC_WORKERS
    per_k = SC_WORKERS // TOP_K
    assert (TOP_K * n) % (SC_WORKERS * SC_CHUNK) == 0 and t0 % 8 == 0

    @pl.kernel(out_type=jax.ShapeDtypeStruct((TOP_K * n, HALF), u32), mesh=_sc_mesh(),
               compiler_params=_sc_params(), name="collect",
               scratch_types=[pltpu.VMEM((per_w,), i32), pltpu.VMEM((2, SC_CHUNK, HALF), u32),
                              pltpu.SemaphoreType.DMA((2,)), pltpu.SemaphoreType.DMA((2,))])
    def k(y_hbm, p_hbm, o_hbm, pos_v, rows_v, gsem, wsem):
        w = _sc_worker()
        src = lax.div(w, per_k) * n_tok + t0 + lax.rem(w, per_k) * per_w
        pltpu.sync_copy(p_hbm.at[pl.ds(pl.multiple_of(src, 8), per_w)], pos_v)
        _sc_copy_rows(y_hbm, pos_v, o_hbm, w * per_w, per_w, rows_v, gsem, wsem)

    return k(ys, pos_flat)


def _ffn_kernel(te_ref, tf_ref, slot_ref, next_ref, rows_ref, nv_ref, xs_ref, wg_hbm, bg_ref, wu_hbm, bu_ref, wd_hbm, bd_ref,
                *rest):
    ys_ref, wbuf, wg_s, wu_s, wd_s, wsem = rest[-6:]
    step = pl.program_id(0)
    w_hbm = (wg_hbm, wu_hbm, wd_hbm)
    w_bf16 = (wg_s, wu_s, wd_s)

    def fetch(e, slot):
        return [pltpu.make_async_copy(w_hbm[m].at[e], wbuf.at[slot, m], wsem.at[slot, m]) for m in range(3)]

    @pl.when(step < nv_ref[0])
    def _():
        @pl.when(tf_ref[step] == 1)
        def _():
            slot = slot_ref[step]

            @pl.when(step == 0)
            def _():
                for cp in fetch(te_ref[0], slot):
                    cp.start()

            for cp in fetch(te_ref[step], slot):
                cp.wait()
            for s in range(2):
                @pl.when(slot == s)
                def _():
                    for m in range(3):
                        w_bf16[m][...] = wbuf[s, m].astype(bf16)

            @pl.when(next_ref[step] >= 0)
            def _():
                for cp in fetch(next_ref[step], 1 - slot):
                    cp.start()

        def rows_block(rb):
            rs = slice(rb * FFN_SUB, (rb + 1) * FFN_SUB)
            x = _unpack_rows(xs_ref[rs, :]).astype(bf16)
            g = jnp.minimum(_dot(x, wg_s[...]) + bg_ref[0], SWIGLU_LIMIT)
            u = jnp.clip(_dot(x, wu_s[...]) + bu_ref[0], -SWIGLU_LIMIT, SWIGLU_LIMIT)
            hid = (u + 1.0) * (g * jax.nn.sigmoid(SWIGLU_ALPHA * g))
            y = _dot(hid.astype(bf16), wd_s[...]) + bd_ref[0]
            ys_ref[rs, :] = _pack_rows(y)

        @pl.when(rows_ref[step] > FFN_SUB)
        def _():
            rows_block(0)
            rows_block(1)

        @pl.when(rows_ref[step] <= FFN_SUB)
        def _():
            rows_block(0)
            ys_ref[FFN_SUB:, :] = jnp.zeros((FFN_ROWS - FFN_SUB, HALF), u32)

    @pl.when(step >= nv_ref[0])
    def _():
        ys_ref[...] = jnp.zeros_like(ys_ref)


def _ffn(tables, xs, n_rows_total, tile0, w_gate, b_gate, w_up, b_up, w_down, b_down, prev=None):
    n_tiles = xs.shape[0] // FFN_ROWS
    assert D_FF == D_MODEL
    assert FFN_ROWS == 2 * FFN_SUB

    def row_map(i, te, tf, sl, nx, rw, nv):
        return (jnp.minimum(i, jnp.maximum(nv[0] - 1, 0)), 0)

    def b_map(i, te, tf, sl, nx, rw, nv):
        return (te[i], 0, 0)

    w_spec = pl.BlockSpec(memory_space=pl.ANY)
    b_spec = pl.BlockSpec((1, 1, D_FF), b_map)
    in_specs = [pl.BlockSpec((FFN_ROWS, HALF), row_map), w_spec, b_spec, w_spec, b_spec, w_spec, b_spec]
    args = [xs, w_gate, b_gate.reshape(N_EXPERTS, 1, D_FF), w_up, b_up.reshape(N_EXPERTS, 1, D_FF),
            w_down, b_down.reshape(N_EXPERTS, 1, D_MODEL)]
    aliases = {}
    if prev is not None:
        in_specs.append(pl.BlockSpec(memory_space=pl.ANY))
        aliases = {len(tables) + len(args): 0}
        args.append(prev)
    return pl.pallas_call(
        _ffn_kernel,
        grid_spec=pltpu.PrefetchScalarGridSpec(
            num_scalar_prefetch=6,
            grid=(n_tiles,),
            in_specs=in_specs,
            out_specs=pl.BlockSpec((FFN_ROWS, HALF), lambda i, *_: (tile0 + i, 0)),
            scratch_shapes=[pltpu.VMEM((2, 3, D_MODEL, D_FF), f32),
                            pltpu.VMEM((D_MODEL, D_FF), bf16),
                            pltpu.VMEM((D_MODEL, D_FF), bf16),
                            pltpu.VMEM((D_FF, D_MODEL), bf16),
                            pltpu.SemaphoreType.DMA((2, 3))]),
        out_shape=jax.ShapeDtypeStruct((n_rows_total, HALF), u32),
        input_output_aliases=aliases,
        compiler_params=pltpu.CompilerParams(dimension_semantics=("arbitrary",),
                                             vmem_limit_bytes=VMEM_LIMIT),
        name="experts",
    )(*tables, *args)


def _finish_kernel(g_ref, h1a_ref, h1b_ref, pw_ref, pa_ref, pb_ref, g_post_ref, w_ple_ref, w_pg_ref, b_pg_ref,
                   g_ple_ref, *rest, first_steps):
    outa_ref, outb_ref = rest[-2:]
    step = pl.program_id(0)
    rows = h1a_ref.shape[0]
    pw = pw_ref[...]
    f = jnp.zeros((rows, D_MODEL), f32)
    for k in range(TOP_K):
        f = f + pw[:, k:k + 1] * _unpack_rows(g_ref[k])
    h2 = _two_groups(step, first_steps, h1a_ref, h1b_ref) + _rms(f, g_post_ref[...])
    gate = jax.nn.sigmoid(_dot(h2.astype(bf16), w_pg_ref[...]) + b_pg_ref[...])
    bt, tt = pa_ref.shape[0], pa_ref.shape[1]
    pin_a = pltpu.einshape("btd->tbd", pa_ref[...]).reshape(rows, PLE_DIM)
    pin = jnp.where(step < first_steps, pin_a, pb_ref[...]).astype(bf16)
    pe = _dot(pin, w_ple_ref[...]) * gate
    out = h2 + _rms(pe, g_ple_ref[...])

    @pl.when(step < first_steps)
    def _():
        outa_ref[...] = pltpu.einshape("tbd->btd", out.reshape(tt, bt, D_MODEL))

    @pl.when(step >= first_steps)
    def _():
        outb_ref[...] = out


def _finish(g, h1a, h1b, pw, pa, pb, g_post, w_ple, w_pg, b_pg, g_ple, *, a0, a_steps, b_steps, prev=None):
    na, nb = h1a.shape[0], h1b.shape[0]
    rows = FIN_ROWS
    bt = pa.shape[0]
    tt = rows // bt
    assert na % rows == 0 and nb % rows == 0 and a_steps >= 1 and rows % bt == 0

    def spec_a(width):
        return pl.BlockSpec((rows, width), lambda i: (a0 + jnp.minimum(i, a_steps - 1), 0))

    def spec_a3(width):
        return pl.BlockSpec((bt, tt, width), lambda i: (0, a0 + jnp.minimum(i, a_steps - 1), 0))

    def spec_b(width):
        return pl.BlockSpec((rows, width), lambda i: (jnp.maximum(i - a_steps, 0), 0))

    in_specs = [pl.BlockSpec((TOP_K, rows, HALF), lambda i: (0, i, 0)),
                spec_a(D_MODEL), spec_b(D_MODEL),
                pl.BlockSpec((rows, LANES), lambda i: (a0 + i, 0)),
                spec_a3(PLE_DIM), spec_b(PLE_DIM),
                _full(g_post.shape), _full(w_ple.shape), _full(w_pg.shape),
                _full(b_pg.shape), _full(g_ple.shape)]
    args = [g, h1a, h1b, pw, pa, pb, g_post, w_ple, w_pg, b_pg, g_ple]
    aliases = {}
    if prev is not None:
        in_specs.append(pl.BlockSpec(memory_space=pl.ANY))
        aliases = {len(args): 0}
        args.append(prev)
    return pl.pallas_call(
        functools.partial(_finish_kernel, first_steps=a_steps),
        grid=(a_steps + b_steps,),
        in_specs=in_specs,
        out_specs=[spec_a3(D_MODEL), spec_b(D_MODEL)],
        out_shape=[jax.ShapeDtypeStruct((bt, na // bt, D_MODEL), f32), jax.ShapeDtypeStruct((nb, D_MODEL), f32)],
        input_output_aliases=aliases,
        compiler_params=pltpu.CompilerParams(dimension_semantics=("arbitrary",),
                                             vmem_limit_bytes=VMEM_LIMIT),
        name="finish",
    )(*args)


def _block_diag_pairs(w):
    n_blocks, n = w.shape[0], w.shape[1]
    per = MXU_DIM // n
    w4 = w.reshape(n_blocks // per, per, n, n)
    blocks = jnp.where(jnp.eye(per, dtype=bool)[None, :, None, :, None], w4[:, :, :, None, :], 0.0)
    return blocks.reshape(n_blocks // per, MXU_DIM, MXU_DIM)


def _time_major(x):
    x = jnp.swapaxes(x, 0, 1)
    return x.reshape((x.shape[0] * x.shape[1],) + x.shape[2:])


def _batch_major(x, b):
    return jnp.swapaxes(x.reshape(x.shape[0] // b, b, x.shape[1]), 0, 1)


def _routing_tables(counts, n_tiles):
    tiles = (counts + FFN_ROWS - 1) // FFN_ROWS
    tile_ends = jnp.cumsum(tiles)
    off = (tile_ends - tiles) * FFN_ROWS
    n_valid = tile_ends[-1:]
    tile_ids = jnp.minimum(jnp.arange(n_tiles, dtype=i32), n_valid - 1)
    tile_e = jnp.sum((tile_ids[:, None] >= tile_ends[None, :]).astype(i32), axis=1)
    cand = jnp.where(tiles > 0, jnp.arange(N_EXPERTS, dtype=i32), N_EXPERTS)
    suffix_min = lax.cummin(cand[::-1])[::-1]
    nxt = jnp.concatenate([suffix_min[1:], jnp.full((1,), N_EXPERTS, i32)])
    onehot = (tile_e[:, None] == jnp.arange(N_EXPERTS, dtype=i32)[None, :]).astype(i32)
    at_tile = lambda per_expert: jnp.sum(onehot * per_expert[None, :].astype(i32), axis=1)
    tile_next = at_tile(jnp.where(nxt < N_EXPERTS, nxt, -1))
    tile_rows = jnp.clip(at_tile(counts) - (tile_ids - at_tile(tile_ends - tiles)) * FFN_ROWS, 0, FFN_ROWS)
    tile_group_end = at_tile(tile_ends)
    return off.astype(i32), (tile_e.astype(i32), tile_next.astype(i32), tile_group_end.astype(i32),
                             tile_rows.astype(i32), n_valid.astype(i32))


def _tile_tables(full, tile0, n):
    tile_e, tile_next, tile_group_end, tile_rows = (a[tile0:tile0 + n] for a in full[:4])
    n_valid = full[4]
    first = jnp.concatenate([jnp.ones((1,), i32), (tile_e[1:] != tile_e[:-1]).astype(i32)])
    slot = (jnp.cumsum(first) - 1) % 2
    nxt = jnp.where(tile_group_end < tile0 + n, tile_next, -1)
    return (tile_e, first, slot.astype(i32), nxt.astype(i32), tile_rows, jnp.clip(n_valid - tile0, 0, n))


def _layer(xp, xs_tm, pp, ps_tm, state_h, state_conv, state_pool, lw):
    (norm_mix_pre, w_in, conv_w, conv_b, w_rgate, b_rgate, w_igate, b_igate, lru_lambda,
     pool_w, pool_b, pool_scale, norm_group_a, norm_group_b, w_out, norm_mix_post,
     norm_ffn_pre, w_router, b_router, w_gate, b_gate, w_up, b_up, w_down, b_down, norm_ffn_post,
     w_ple, w_ple_gate, b_ple_gate, norm_ple) = lw
    b_p, s_p = xp.shape[0], xp.shape[1]
    n_p, n_s = b_p * s_p, xs_tm.shape[0]
    b_s = state_h.shape[0]
    row = lambda a: a.reshape(1, -1)

    w_gates = jnp.concatenate([_block_diag_pairs(w_rgate), _block_diag_pairs(w_igate)], axis=-1).astype(bf16)
    mix_w = (row(norm_mix_pre), w_in.astype(bf16), conv_w, row(conv_b),
             w_gates, row(b_rgate), row(b_igate), row(lru_lambda),
             _block_diag_pairs(pool_w).astype(bf16), row(pool_b), row(pool_scale), row(norm_group_a), row(norm_group_b),
             w_out.astype(bf16), row(norm_mix_post))

    zeros = lambda *s: jnp.zeros(s, f32)
    h1_p, hfin_p, cfin_p, pfin_p = _mixer(
        xp, zeros((CONV_W - 1) * b_p, W_A), zeros(POOL_BUF * b_p, W_B), zeros(b_p, W_A),
        mix_w, bt=b_p, start=0)
    h1_s, hfin_s, cfin_s, pfin_s = _mixer(
        xs_tm, _time_major(state_conv), _time_major(state_pool), state_h,
        mix_w, bt=b_s, start=PAST_LEN)
    t = n_p + n_s

    w_router_pad = jnp.zeros((D_MODEL, LANES), f32).at[:, :N_EXPERTS].set(w_router)
    b_router_pad = jnp.zeros((1, LANES), f32).at[0, :N_EXPERTS].set(b_router)
    tri = jnp.triu(jnp.ones((ROUTE_ROWS, ROUTE_ROWS), bf16), k=1)
    vpk, idx_t, rank_t, pw, cnt = _router(h1_p, h1_s, row(norm_ffn_pre), w_router_pad.astype(bf16),
                                          b_router_pad, tri)

    n_tiles = (t * TOP_K) // FFN_ROWS + N_EXPERTS
    off, tables = _routing_tables(cnt[:, 0], n_tiles)
    pos_flat = _positions(off, idx_t, rank_t).reshape(-1)

    step_tiles = SC_WORKERS * SC_CHUNK // FFN_ROWS
    tiles_1 = (n_tiles // FIRST_SHARE) // step_tiles * step_tiles
    ys = None
    for tile0, n in ((0, tiles_1), (tiles_1, n_tiles - tiles_1)):
        xs = _sc_dispatch(vpk, pos_flat, tile0 * FFN_ROWS, n * FFN_ROWS)
        ys = _ffn(_tile_tables(tables, tile0, n), xs, n_tiles * FFN_ROWS, tile0,
                  w_gate, b_gate, w_up, b_up, w_down, b_down, prev=ys)
    fin_w = (row(norm_ffn_post), w_ple.astype(bf16), w_ple_gate.astype(bf16), row(b_ple_gate), row(norm_ple))
    a_blocks, b_blocks = n_p // FIN_ROWS, n_s // FIN_ROWS
    first = a_blocks // 2
    n1, n2 = first * FIN_ROWS, t - first * FIN_ROWS
    g1 = _sc_collect(ys, pos_flat, 0, n1).reshape(TOP_K, n1, HALF)
    out_p, _ = _finish(g1, h1_p, h1_s, pw, pp, ps_tm, *fin_w, a0=0, a_steps=first, b_steps=0)
    g2 = _sc_collect(ys, pos_flat, n1, n2).reshape(TOP_K, n2, HALF)
    out_p, out_s = _finish(g2, h1_p, h1_s, pw, pp, ps_tm, *fin_w, a0=first, a_steps=a_blocks - first,
                           b_steps=b_blocks, prev=out_p)

    states = (hfin_p, _batch_major(cfin_p, b_p), _batch_major(pfin_p, b_p),
              hfin_s, _batch_major(cfin_s, b_s), _batch_major(pfin_s, b_s))
    return out_p, out_s, states


def kernel(x_prompt, x_sample, state_rglru_h, state_rglru_conv, state_pool, p_prompt, p_sample, norm_mix_pre, w_in, conv_w, conv_b, w_rgate, b_rgate, w_igate, b_igate, lru_lambda, pool_w, pool_b, pool_scale, norm_group_a, norm_group_b, w_out, norm_mix_post, norm_ffn_pre, w_router, b_router, w_gate, b_gate, w_up, b_up, w_down, b_down, norm_ffn_post, w_ple, w_ple_gate, b_ple_gate, norm_ple):
    depth = w_in.shape[0]
    b_s = x_sample.shape[0]
    per_layer = (norm_mix_pre, w_in, conv_w, conv_b, w_rgate, b_rgate, w_igate, b_igate, lru_lambda,
                 pool_w, pool_b, pool_scale, norm_group_a, norm_group_b, w_out, norm_mix_post,
                 norm_ffn_pre, w_router, b_router, w_gate, b_gate, w_up, b_up, w_down, b_down,
                 norm_ffn_post, w_ple, w_ple_gate, b_ple_gate, norm_ple)
    hp, hs = x_prompt, _time_major(x_sample)
    collected = []
    for i in range(depth):
        hp, hs, states = _layer(hp, hs, p_prompt[i], _time_major(p_sample[i]),
                                state_rglru_h[i], state_rglru_conv[i], state_pool[i],
                                tuple(w[i] for w in per_layer))
        collected.append(states)
    stacked = tuple(jnp.stack([c[j] for c in collected]) for j in range(6))
    return (hp, _batch_major(hs, b_s)) + stacked
```

```python
import dataclasses
import functools

import jax
import jax.numpy as jnp
from jax import lax
from jax.experimental import pallas as pl
from jax.experimental.pallas import tpu as pltpu
from jax.experimental.pallas import tpu_sc as plsc

D_MODEL = 1024
W_A = 512
W_B = 512
N_HEADS_A = 8
HEAD_A = W_A // N_HEADS_A
CONV_W = 4
LRU_C = 8.0
POOL_WINDOWS = (2, 4, 8, 16)
GROUP_B = W_B // len(POOL_WINDOWS)
POOL_BUF = max(POOL_WINDOWS) - 1
N_EXPERTS = 32
TOP_K = 4
D_FF = 1024
SWIGLU_LIMIT = 7.0
SWIGLU_ALPHA = 1.702
PLE_DIM = 256
EPS = 1e-6
PAST_LEN = 16384

LANES = 128
MXU_DIM = 256
HALF = D_MODEL // 2
SC_CORES = 2
SC_SUBCORES = 16
SC_WORKERS = SC_CORES * SC_SUBCORES
SC_LANES = 16
SC_CHUNK = 64

MIX_ROWS = 512
ROUTE_ROWS = 512
FFN_ROWS = 512
FFN_SUB = 256
FIN_ROWS = 512
FIRST_SHARE = 4
FIN_PARTS = 4
VMEM_LIMIT = 56 * 1024 * 1024

f32 = jnp.float32
bf16 = jnp.bfloat16
u32 = jnp.uint32
i32 = jnp.int32


def _rms(x, g):
    return x * lax.rsqrt(jnp.mean(x * x, axis=-1, keepdims=True) + EPS) * g


def _dot(a, b):
    return jnp.dot(a, b, preferred_element_type=f32)


def _pack_rows(x):
    bits = lax.bitcast_convert_type(x.astype(bf16).astype(f32), u32)
    return (bits[:, HALF:] & jnp.uint32(0xFFFF0000)) | (bits[:, :HALF] >> 16)


def _unpack_rows(p):
    lo = lax.bitcast_convert_type(p << 16, f32)
    hi = lax.bitcast_convert_type(p & jnp.uint32(0xFFFF0000), f32)
    return jnp.concatenate([lo, hi], axis=-1)


def _mixer_kernel(x_ref, conv0_ref, pool0_ref, h0_ref, g_pre_ref, w_in_ref, conv_w_ref, conv_b_ref,
                  w_gates_ref, br_ref, bi_ref, lam_ref,
                  pool_w_ref, pool_b_ref, pool_scale_ref, ng_a_ref, ng_b_ref, w_out_ref, g_post_ref,
                  h1_ref, hfin_ref, convfin_ref, poolfin_ref,
                  h_s, conv_s, pool_s, a_s, b_s, hs_s, *, bt, tt, start):
    step = pl.program_id(0)
    rows = bt * tt

    @pl.when(step == 0)
    def _():
        h_s[...] = h0_ref[...]
        conv_s[...] = conv0_ref[...]
        pool_s[...] = pool0_ref[...]

    if x_ref.ndim == 3:
        x = pltpu.einshape("btd->tbd", x_ref[...]).reshape(rows, D_MODEL)
    else:
        x = x_ref[...]
    u = _rms(x, g_pre_ref[...]).astype(bf16)
    z = _dot(u, w_in_ref[...])
    xa, ga, xb = z[:, :W_A], z[:, W_A:2 * W_A], z[:, 2 * W_A:]

    ext_a = jnp.concatenate([conv_s[...], xa], axis=0)
    xc = conv_b_ref[...]
    for k in range(CONV_W):
        xc = xc + ext_a[k * bt:k * bt + rows] * conv_w_ref[k:k + 1, :]
    conv_s[...] = ext_a[rows:]

    xc16 = xc.astype(bf16)
    gates = [_dot(xc16[:, j * MXU_DIM:(j + 1) * MXU_DIM], w_gates_ref[j]) for j in range(W_A // MXU_DIM)]
    r = jax.nn.sigmoid(jnp.concatenate([gj[:, :MXU_DIM] for gj in gates], axis=-1) + br_ref[...])
    ig = jax.nn.sigmoid(jnp.concatenate([gj[:, MXU_DIM:] for gj in gates], axis=-1) + bi_ref[...])
    lam = lam_ref[...]
    softplus_neg = jnp.maximum(-lam, 0.0) + jnp.log1p(jnp.exp(-jnp.abs(lam)))
    log_a = (-LRU_C) * r * softplus_neg
    a_s[...] = jnp.exp(log_a)
    th = jnp.tanh(log_a)
    v = (-2.0 * th) / (1.0 - th)
    b_s[...] = (v * lax.rsqrt(jnp.maximum(v, 1.17549435e-38))) * (ig * xc)

    def scan_step(t, h):
        sl = pl.ds(pl.multiple_of(t * bt, bt), bt)
        h = a_s[sl, :] * h + b_s[sl, :]
        hs_s[sl, :] = h
        return h

    h_last = lax.fori_loop(0, tt, scan_step, h_s[...], unroll=True)
    h_s[...] = h_last
    ya = hs_s[...] * jax.nn.gelu(ga)

    ext_b = jnp.concatenate([pool_s[...], xb], axis=0)
    pool_s[...] = ext_b[rows:]
    s2 = ext_b[bt:, :] + ext_b[:-bt, :]
    s4 = s2[2 * bt:, GROUP_B:] + s2[:-2 * bt, GROUP_B:]
    s8 = s4[4 * bt:, GROUP_B:] + s4[:-4 * bt, GROUP_B:]
    s16 = s8[8 * bt:, GROUP_B:] + s8[:-8 * bt, GROUP_B:]
    wins = (s2[14 * bt:, :GROUP_B], s4[12 * bt:, :GROUP_B], s8[8 * bt:, :GROUP_B], s16)
    t_idx = lax.broadcasted_iota(i32, (rows, GROUP_B), 0) // bt
    pos1 = (t_idx + (step * tt + start + 1)).astype(f32)
    d_parts = []
    for g, w in enumerate(POOL_WINDOWS):
        cnt = jnp.minimum(jnp.float32(w), pos1)
        d_parts.append(wins[g] / cnt - xb[:, g * GROUP_B:(g + 1) * GROUP_B])
    d = jnp.concatenate(d_parts, axis=-1).astype(bf16)
    yb_parts = [_dot(d[:, j * MXU_DIM:(j + 1) * MXU_DIM], pool_w_ref[j]) for j in range(W_B // MXU_DIM)]
    yb = (jnp.concatenate(yb_parts, axis=-1) + pool_b_ref[...]) * pool_scale_ref[...]

    na = _rms(ya, ng_a_ref[...])
    nb = _rms(yb, ng_b_ref[...])
    m = _dot(jnp.concatenate([na, nb], axis=-1).astype(bf16), w_out_ref[...])
    h1_ref[...] = x + _rms(m, g_post_ref[...])

    hfin_ref[...] = h_last
    convfin_ref[...] = conv_s[...]
    poolfin_ref[...] = pool_s[...]


def _full(shape):
    return pl.BlockSpec(shape, lambda i, *_: (0,) * len(shape))


def _mixer(x, conv0, pool0, h0, wts, *, bt, start):
    tt = MIX_ROWS // bt
    rows = bt * tt
    if x.ndim == 3:
        assert x.shape[0] == bt and x.shape[1] % tt == 0
        n_steps = x.shape[1] // tt
        x_spec = pl.BlockSpec((bt, tt, D_MODEL), lambda i: (0, i, 0))
    else:
        assert x.shape[0] % rows == 0
        n_steps = x.shape[0] // rows
        x_spec = pl.BlockSpec((rows, D_MODEL), lambda i: (i, 0))
    kern = functools.partial(_mixer_kernel, bt=bt, tt=tt, start=start)
    small = [conv0, pool0, h0] + list(wts)
    return pl.pallas_call(
        kern,
        grid=(n_steps,),
        in_specs=[x_spec] + [_full(a.shape) for a in small],
        out_specs=[pl.BlockSpec((rows, D_MODEL), lambda i: (i, 0)),
                   _full((bt, W_A)), _full(((CONV_W - 1) * bt, W_A)), _full((POOL_BUF * bt, W_B))],
        out_shape=[jax.ShapeDtypeStruct((n_steps * rows, D_MODEL), f32),
                   jax.ShapeDtypeStruct((bt, W_A), f32),
                   jax.ShapeDtypeStruct(((CONV_W - 1) * bt, W_A), f32),
                   jax.ShapeDtypeStruct((POOL_BUF * bt, W_B), f32)],
        scratch_shapes=[pltpu.VMEM((bt, W_A), f32),
                        pltpu.VMEM(((CONV_W - 1) * bt, W_A), f32),
                        pltpu.VMEM((POOL_BUF * bt, W_B), f32),
                        pltpu.VMEM((rows, W_A), f32),
                        pltpu.VMEM((rows, W_A), f32),
                        pltpu.VMEM((rows, W_A), f32)],
        compiler_params=pltpu.CompilerParams(dimension_semantics=("arbitrary",),
                                             vmem_limit_bytes=VMEM_LIMIT),
        name="mixer",
    )(x, *small)


def _two_groups(step, first_steps, a_ref, b_ref):
    return jnp.where(step < first_steps, a_ref[...], b_ref[...])


def _router_kernel(h1a_ref, h1b_ref, g_ref, wr_ref, br_ref, tri_ref,
                   vpk_ref, idx_ref, rank_ref, pw_ref, cnt_ref, carry_s, *, first_steps):
    step = pl.program_id(0)
    rows = h1a_ref.shape[0]

    @pl.when(step == 0)
    def _():
        carry_s[...] = jnp.zeros_like(carry_s)

    v = _rms(_two_groups(step, first_steps, h1a_ref, h1b_ref), g_ref[...])
    vpk_ref[...] = _pack_rows(v)
    logits = _dot(v.astype(bf16), wr_ref[...]) + br_ref[...]
    lt = jnp.transpose(logits)[:N_EXPERTS, :]

    eio = lax.broadcasted_iota(i32, (N_EXPERTS, rows), 0).astype(f32)
    work = lt
    vals, idxs, sels = [], [], []
    for _ in range(TOP_K):
        m = jnp.max(work, axis=0, keepdims=True)
        ik = jnp.min(jnp.where(work == m, eio, float(N_EXPERTS)), axis=0, keepdims=True)
        sel = eio == ik
        vals.append(m)
        idxs.append(ik)
        sels.append(sel)
        work = jnp.where(sel, -jnp.inf, work)
    exps = [jnp.exp(val - vals[0]) for val in vals]
    denom = exps[0] + exps[1] + exps[2] + exps[3]
    probs = [e / denom for e in exps]

    chosen = sels[0] | sels[1] | sels[2] | sels[3]
    onehot = jnp.where(chosen, 1.0, 0.0)
    before = _dot(onehot.astype(bf16), tri_ref[...])
    base = carry_s[...] + before
    ranks = [jnp.sum(jnp.where(sel, base, 0.0), axis=0, keepdims=True) for sel in sels]
    carry_s[...] = carry_s[...] + jnp.sum(onehot, axis=1, keepdims=True)

    idx_ref[...] = jnp.concatenate(idxs, axis=0).astype(i32)
    rank_ref[...] = jnp.concatenate(ranks, axis=0).astype(i32)
    pad = jnp.zeros((LANES - TOP_K, rows), f32)
    pw_ref[...] = jnp.transpose(jnp.concatenate(probs + [pad], axis=0))
    cnt_ref[...] = carry_s[:, :LANES].astype(i32)


def _group_specs(rows, width, na, nb):
    first = na // rows
    return (pl.BlockSpec((rows, width), lambda i: (jnp.minimum(i, first - 1), 0)),
            pl.BlockSpec((rows, width), lambda i: (jnp.maximum(i - first, 0), 0)))


def _router(h1a, h1b, g, wr, br, tri):
    na, nb = h1a.shape[0], h1b.shape[0]
    t = na + nb
    rows = ROUTE_ROWS
    assert na % rows == 0 and nb % rows == 0
    return pl.pallas_call(
        functools.partial(_router_kernel, first_steps=na // rows),
        grid=(t // rows,),
        in_specs=[*_group_specs(rows, D_MODEL, na, nb),
                  _full(g.shape), _full(wr.shape), _full(br.shape), _full(tri.shape)],
        out_specs=[pl.BlockSpec((rows, HALF), lambda i: (i, 0)),
                   pl.BlockSpec((TOP_K, rows), lambda i: (0, i)),
                   pl.BlockSpec((TOP_K, rows), lambda i: (0, i)),
                   pl.BlockSpec((rows, LANES), lambda i: (i, 0)),
                   _full((N_EXPERTS, LANES))],
        out_shape=[jax.ShapeDtypeStruct((t, HALF), u32),
                   jax.ShapeDtypeStruct((TOP_K, t), i32),
                   jax.ShapeDtypeStruct((TOP_K, t), i32),
                   jax.ShapeDtypeStruct((t, LANES), f32),
                   jax.ShapeDtypeStruct((N_EXPERTS, LANES), i32)],
        scratch_shapes=[pltpu.VMEM((N_EXPERTS, rows), f32)],
        compiler_params=pltpu.CompilerParams(dimension_semantics=("arbitrary",),
                                             vmem_limit_bytes=VMEM_LIMIT),
        name="router",
    )(h1a, h1b, g, wr, br, tri)


def _positions_kernel(off_ref, idx_ref, rank_ref, pos_ref):
    idx = idx_ref[...]
    pos = rank_ref[...]
    for e in range(N_EXPERTS):
        pos = pos + jnp.where(idx == e, off_ref[e], 0)
    pos_ref[...] = pos


def _positions(off, idx_t, rank_t):
    spec = pl.BlockSpec(idx_t.shape, lambda i, *_: (0, 0))
    return pl.pallas_call(
        _positions_kernel,
        grid_spec=pltpu.PrefetchScalarGridSpec(num_scalar_prefetch=1, grid=(1,),
                                               in_specs=[spec, spec], out_specs=spec),
        out_shape=jax.ShapeDtypeStruct(idx_t.shape, i32),
        name="positions",
    )(off, idx_t, rank_t)


def _sc_mesh():
    return plsc.VectorSubcoreMesh(core_axis_name="core", subcore_axis_name="subcore")


def _sc_params():
    return dataclasses.replace(pltpu.CompilerParams(), needs_layout_passes=False)


def _sc_worker():
    return lax.axis_index("subcore") * SC_CORES + lax.axis_index("core")


def _sc_copy_rows(src_hbm, idx_v, dst_hbm, dst_row0, n, rows_v, gsem, wsem):
    n_ch = n // SC_CHUNK
    n_pair = n_ch // 2

    def gather(c, b):
        o = pl.multiple_of(c * SC_CHUNK, SC_CHUNK)
        return pltpu.make_async_copy(src_hbm.at[idx_v.at[pl.ds(o, SC_CHUNK)]], rows_v.at[b], gsem.at[b])

    def write(c, b):
        o = pl.multiple_of(c * SC_CHUNK, SC_CHUNK)
        return pltpu.make_async_copy(rows_v.at[b], dst_hbm.at[pl.ds(dst_row0 + o, SC_CHUNK)], wsem.at[b])

    gather(0, 0).start()

    @pl.loop(0, n_pair)
    def _(i):
        c0 = 2 * i
        gather(c0, 0).wait()

        @pl.when(i > 0)
        def _():
            write(c0 - 1, 1).wait()

        gather(c0 + 1, 1).start()
        write(c0, 0).start()
        gather(c0 + 1, 1).wait()
        write(c0, 0).wait()

        @pl.when(c0 + 2 < n_ch)
        def _():
            gather(c0 + 2, 0).start()

        write(c0 + 1, 1).start()

    if n_ch % 2:
        gather(n_ch - 1, 0).wait()
        write(n_ch - 1, 0).start()
        write(n_ch - 1, 0).wait()
    if n_pair:
        write(2 * n_pair - 1, 1).wait()


def _sc_dispatch(vpk, pos_flat, row0, n_out):
    n_tok = vpk.shape[0]
    per_w = n_out // SC_WORKERS
    assert n_out % (SC_WORKERS * SC_CHUNK) == 0 and n_tok % SC_LANES == 0 and row0 % SC_LANES == 0

    @pl.kernel(out_type=jax.ShapeDtypeStruct((n_out, HALF), u32), mesh=_sc_mesh(),
               compiler_params=_sc_params(), name="dispatch",
               scratch_types=[pltpu.VMEM((per_w,), i32), pltpu.VMEM((n_tok,), i32),
                              pltpu.VMEM((2, SC_CHUNK, HALF), u32),
                              pltpu.SemaphoreType.DMA((2,)), pltpu.SemaphoreType.DMA((2,))])
    def k(v_hbm, p_hbm, o_hbm, src_v, pos_v, rows_v, gsem, wsem):
        out0 = _sc_worker() * per_w
        lo = row0 + out0

        @pl.loop(0, per_w // SC_LANES)
        def _(i):
            o = pl.multiple_of(i * SC_LANES, SC_LANES)
            src_v[pl.ds(o, SC_LANES)] = lax.iota(i32, SC_LANES) + lax.rem(lo + o, n_tok)

        @pl.loop(0, TOP_K)
        def _(kk):
            pltpu.sync_copy(p_hbm.at[pl.ds(kk * n_tok, n_tok)], pos_v)

            @plsc.parallel_loop(0, n_tok, step=SC_LANES, unroll=8)
            def _(o):
                p = pos_v[pl.ds(pl.multiple_of(o, SC_LANES), SC_LANES)] - lo
                mine = (p >= 0) & (p < per_w)
                tok = lax.iota(i32, SC_LANES) + o
                plsc.store_scatter(src_v, [jnp.where(mine, p, 0)], tok, mask=mine)

        _sc_copy_rows(v_hbm, src_v, o_hbm, out0, per_w, rows_v, gsem, wsem)

    return k(vpk, pos_flat)


def _sc_collect(ys, pos_flat, t0, n):
    n_tok = pos_flat.shape[0] // TOP_K
    per_w = TOP_K * n // SC_WORKERS
    per_k = SC_WORKERS // TOP_K
    assert (TOP_K * n) % (SC_WORKERS * SC_CHUNK) == 0 and t0 % 8 == 0

    @pl.kernel(out_type=jax.ShapeDtypeStruct((TOP_K * n, HALF), u32), mesh=_sc_mesh(),
               compiler_params=_sc_params(), name="collect",
               scratch_types=[pltpu.VMEM((per_w,), i32), pltpu.VMEM((2, SC_CHUNK, HALF), u32),
                              pltpu.SemaphoreType.DMA((2,)), pltpu.SemaphoreType.DMA((2,))])
    def k(y_hbm, p_hbm, o_hbm, pos_v, rows_v, gsem, wsem):
        w = _sc_worker()
        src = lax.div(w, per_k) * n_tok + t0 + lax.rem(w, per_k) * per_w
        pltpu.sync_copy(p_hbm.at[pl.ds(pl.multiple_of(src, 8), per_w)], pos_v)
        _sc_copy_rows(y_hbm, pos_v, o_hbm, w * per_w, per_w, rows_v, gsem, wsem)

    return k(ys, pos_flat)


def _ffn_kernel(te_ref, tf_ref, slot_ref, next_ref, rows_ref, nv_ref, xs_ref, wg_hbm, bg_ref, wu_hbm, bu_ref, wd_hbm, bd_ref,
                *rest):
    ys_ref, wbuf, wg_s, wu_s, wd_s, wsem = rest[-6:]
    step = pl.program_id(0)
    w_hbm = (wg_hbm, wu_hbm, wd_hbm)
    w_bf16 = (wg_s, wu_s, wd_s)

    def fetch(e, slot):
        return [pltpu.make_async_copy(w_hbm[m].at[e], wbuf.at[slot, m], wsem.at[slot, m]) for m in range(3)]

    @pl.when(step < nv_ref[0])
    def _():
        @pl.when(tf_ref[step] == 1)
        def _():
            slot = slot_ref[step]

            @pl.when(step == 0)
            def _():
                for cp in fetch(te_ref[0], slot):
                    cp.start()

            for cp in fetch(te_ref[step], slot):
                cp.wait()
            for s in range(2):
                @pl.when(slot == s)
                def _():
                    for m in range(3):
                        w_bf16[m][...] = wbuf[s, m].astype(bf16)

            @pl.when(next_ref[step] >= 0)
            def _():
                for cp in fetch(next_ref[step], 1 - slot):
                    cp.start()

        def rows_block(rb):
            rs = slice(rb * FFN_SUB, (rb + 1) * FFN_SUB)
            x = _unpack_rows(xs_ref[rs, :]).astype(bf16)
            g = jnp.minimum(_dot(x, wg_s[...]) + bg_ref[0], SWIGLU_LIMIT)
            u = jnp.clip(_dot(x, wu_s[...]) + bu_ref[0], -SWIGLU_LIMIT, SWIGLU_LIMIT)
            hid = (u + 1.0) * (g * jax.nn.sigmoid(SWIGLU_ALPHA * g))
            y = _dot(hid.astype(bf16), wd_s[...]) + bd_ref[0]
            ys_ref[rs, :] = _pack_rows(y)

        @pl.when(rows_ref[step] > FFN_SUB)
        def _():
            rows_block(0)
            rows_block(1)

        @pl.when(rows_ref[step] <= FFN_SUB)
        def _():
            rows_block(0)
            ys_ref[FFN_SUB:, :] = jnp.zeros((FFN_ROWS - FFN_SUB, HALF), u32)

    @pl.when(step >= nv_ref[0])
    def _():
        ys_ref[...] = jnp.zeros_like(ys_ref)


def _ffn(tables, xs, n_rows_total, tile0, w_gate, b_gate, w_up, b_up, w_down, b_down, prev=None):
    n_tiles = xs.shape[0] // FFN_ROWS
    assert D_FF == D_MODEL
    assert FFN_ROWS == 2 * FFN_SUB

    def row_map(i, te, tf, sl, nx, rw, nv):
        return (jnp.minimum(i, jnp.maximum(nv[0] - 1, 0)), 0)

    def b_map(i, te, tf, sl, nx, rw, nv):
        return (te[i], 0, 0)

    w_spec = pl.BlockSpec(memory_space=pl.ANY)
    b_spec = pl.BlockSpec((1, 1, D_FF), b_map)
    in_specs = [pl.BlockSpec((FFN_ROWS, HALF), row_map), w_spec, b_spec, w_spec, b_spec, w_spec, b_spec]
    args = [xs, w_gate, b_gate.reshape(N_EXPERTS, 1, D_FF), w_up, b_up.reshape(N_EXPERTS, 1, D_FF),
            w_down, b_down.reshape(N_EXPERTS, 1, D_MODEL)]
    aliases = {}
    if prev is not None:
        in_specs.append(pl.BlockSpec(memory_space=pl.ANY))
        aliases = {len(tables) + len(args): 0}
        args.append(prev)
    return pl.pallas_call(
        _ffn_kernel,
        grid_spec=pltpu.PrefetchScalarGridSpec(
            num_scalar_prefetch=6,
            grid=(n_tiles,),
            in_specs=in_specs,
            out_specs=pl.BlockSpec((FFN_ROWS, HALF), lambda i, *_: (tile0 + i, 0)),
            scratch_shapes=[pltpu.VMEM((2, 3, D_MODEL, D_FF), f32),
                            pltpu.VMEM((D_MODEL, D_FF), bf16),
                            pltpu.VMEM((D_MODEL, D_FF), bf16),
                            pltpu.VMEM((D_FF, D_MODEL), bf16),
                            pltpu.SemaphoreType.DMA((2, 3))]),
        out_shape=jax.ShapeDtypeStruct((n_rows_total, HALF), u32),
        input_output_aliases=aliases,
        compiler_params=pltpu.CompilerParams(dimension_semantics=("arbitrary",),
                                             vmem_limit_bytes=VMEM_LIMIT),
        name="experts",
    )(*tables, *args)


def _finish_kernel(g_ref, h1a_ref, h1b_ref, pw_ref, pa_ref, pb_ref, g_post_ref, w_ple_ref, w_pg_ref, b_pg_ref,
                   g_ple_ref, *rest, first_steps):
    outa_ref, outb_ref = rest[-2:]
    step = pl.program_id(0)
    rows = h1a_ref.shape[0]
    pw = pw_ref[...]
    f = jnp.zeros((rows, D_MODEL), f32)
    for k in range(TOP_K):
        f = f + pw[:, k:k + 1] * _unpack_rows(g_ref[k])
    h2 = _two_groups(step, first_steps, h1a_ref, h1b_ref) + _rms(f, g_post_ref[...])
    gate = jax.nn.sigmoid(_dot(h2.astype(bf16), w_pg_ref[...]) + b_pg_ref[...])
    bt, tt = pa_ref.shape[0], pa_ref.shape[1]
    pin_a = pltpu.einshape("btd->tbd", pa_ref[...]).reshape(rows, PLE_DIM)
    pin = jnp.where(step < first_steps, pin_a, pb_ref[...]).astype(bf16)
    pe = _dot(pin, w_ple_ref[...]) * gate
    out = h2 + _rms(pe, g_ple_ref[...])

    @pl.when(step < first_steps)
    def _():
        outa_ref[...] = pltpu.einshape("tbd->btd", out.reshape(tt, bt, D_MODEL))

    @pl.when(step >= first_steps)
    def _():
        outb_ref[...] = out


def _finish(g, h1a, h1b, pw, pa, pb, g_post, w_ple, w_pg, b_pg, g_ple, *, a0, a_steps, b_steps, prev=None):
    na, nb = h1a.shape[0], h1b.shape[0]
    rows = FIN_ROWS
    bt = pa.shape[0]
    tt = rows // bt
    assert na % rows == 0 and nb % rows == 0 and a_steps >= 1 and rows % bt == 0

    def spec_a(width):
        return pl.BlockSpec((rows, width), lambda i: (a0 + jnp.minimum(i, a_steps - 1), 0))

    def spec_a3(width):
        return pl.BlockSpec((bt, tt, width), lambda i: (0, a0 + jnp.minimum(i, a_steps - 1), 0))

    def spec_b(width):
        return pl.BlockSpec((rows, width), lambda i: (jnp.maximum(i - a_steps, 0), 0))

    in_specs = [pl.BlockSpec((TOP_K, rows, HALF), lambda i: (0, i, 0)),
                spec_a(D_MODEL), spec_b(D_MODEL),
                pl.BlockSpec((rows, LANES), lambda i: (a0 + i, 0)),
                spec_a3(PLE_DIM), spec_b(PLE_DIM),
                _full(g_post.shape), _full(w_ple.shape), _full(w_pg.shape),
                _full(b_pg.shape), _full(g_ple.shape)]
    args = [g, h1a, h1b, pw, pa, pb, g_post, w_ple, w_pg, b_pg, g_ple]
    aliases = {}
    if prev is not None:
        in_specs.append(pl.BlockSpec(memory_space=pl.ANY))
        aliases = {len(args): 0}
        args.append(prev)
    return pl.pallas_call(
        functools.partial(_finish_kernel, first_steps=a_steps),
        grid=(a_steps + b_steps,),
        in_specs=in_specs,
        out_specs=[spec_a3(D_MODEL), spec_b(D_MODEL)],
        out_shape=[jax.ShapeDtypeStruct((bt, na // bt, D_MODEL), f32), jax.ShapeDtypeStruct((nb, D_MODEL), f32)],
        input_output_aliases=aliases,
        compiler_params=pltpu.CompilerParams(dimension_semantics=("arbitrary",),
                                             vmem_limit_bytes=VMEM_LIMIT),
        name="finish",
    )(*args)


def _block_diag_pairs(w):
    n_blocks, n = w.shape[0], w.shape[1]
    per = MXU_DIM // n
    w4 = w.reshape(n_blocks // per, per, n, n)
    blocks = jnp.where(jnp.eye(per, dtype=bool)[None, :, None, :, None], w4[:, :, :, None, :], 0.0)
    return blocks.reshape(n_blocks // per, MXU_DIM, MXU_DIM)


def _time_major(x):
    x = jnp.swapaxes(x, 0, 1)
    return x.reshape((x.shape[0] * x.shape[1],) + x.shape[2:])


def _batch_major(x, b):
    return jnp.swapaxes(x.reshape(x.shape[0] // b, b, x.shape[1]), 0, 1)


def _routing_tables(counts, n_tiles):
    tiles = (counts + FFN_ROWS - 1) // FFN_ROWS
    tile_ends = jnp.cumsum(tiles)
    off = (tile_ends - tiles) * FFN_ROWS
    n_valid = tile_ends[-1:]
    tile_ids = jnp.minimum(jnp.arange(n_tiles, dtype=i32), n_valid - 1)
    tile_e = jnp.sum((tile_ids[:, None] >= tile_ends[None, :]).astype(i32), axis=1)
    cand = jnp.where(tiles > 0, jnp.arange(N_EXPERTS, dtype=i32), N_EXPERTS)
    suffix_min = lax.cummin(cand[::-1])[::-1]
    nxt = jnp.concatenate([suffix_min[1:], jnp.full((1,), N_EXPERTS, i32)])
    onehot = (tile_e[:, None] == jnp.arange(N_EXPERTS, dtype=i32)[None, :]).astype(i32)
    at_tile = lambda per_expert: jnp.sum(onehot * per_expert[None, :].astype(i32), axis=1)
    tile_next = at_tile(jnp.where(nxt < N_EXPERTS, nxt, -1))
    tile_rows = jnp.clip(at_tile(counts) - (tile_ids - at_tile(tile_ends - tiles)) * FFN_ROWS, 0, FFN_ROWS)
    tile_group_end = at_tile(tile_ends)
    return off.astype(i32), (tile_e.astype(i32), tile_next.astype(i32), tile_group_end.astype(i32),
                             tile_rows.astype(i32), n_valid.astype(i32))


def _tile_tables(full, tile0, n):
    tile_e, tile_next, tile_group_end, tile_rows = (a[tile0:tile0 + n] for a in full[:4])
    n_valid = full[4]
    first = jnp.concatenate([jnp.ones((1,), i32), (tile_e[1:] != tile_e[:-1]).astype(i32)])
    slot = (jnp.cumsum(first) - 1) % 2
    nxt = jnp.where(tile_group_end < tile0 + n, tile_next, -1)
    return (tile_e, first, slot.astype(i32), nxt.astype(i32), tile_rows, jnp.clip(n_valid - tile0, 0, n))


def _layer(xp, xs_tm, pp, ps_tm, state_h, state_conv, state_pool, lw):
    (norm_mix_pre, w_in, conv_w, conv_b, w_rgate, b_rgate, w_igate, b_igate, lru_lambda,
     pool_w, pool_b, pool_scale, norm_group_a, norm_group_b, w_out, norm_mix_post,
     norm_ffn_pre, w_router, b_router, w_gate, b_gate, w_up, b_up, w_down, b_down, norm_ffn_post,
     w_ple, w_ple_gate, b_ple_gate, norm_ple) = lw
    b_p, s_p = xp.shape[0], xp.shape[1]
    n_p, n_s = b_p * s_p, xs_tm.shape[0]
    b_s = state_h.shape[0]
    row = lambda a: a.reshape(1, -1)

    w_gates = jnp.concatenate([_block_diag_pairs(w_rgate), _block_diag_pairs(w_igate)], axis=-1).astype(bf16)
    mix_w = (row(norm_mix_pre), w_in.astype(bf16), conv_w, row(conv_b),
             w_gates, row(b_rgate), row(b_igate), row(lru_lambda),
             _block_diag_pairs(pool_w).astype(bf16), row(pool_b), row(pool_scale), row(norm_group_a), row(norm_group_b),
             w_out.astype(bf16), row(norm_mix_post))

    zeros = lambda *s: jnp.zeros(s, f32)
    h1_p, hfin_p, cfin_p, pfin_p = _mixer(
        xp, zeros((CONV_W - 1) * b_p, W_A), zeros(POOL_BUF * b_p, W_B), zeros(b_p, W_A),
        mix_w, bt=b_p, start=0)
    h1_s, hfin_s, cfin_s, pfin_s = _mixer(
        xs_tm, _time_major(state_conv), _time_major(state_pool), state_h,
        mix_w, bt=b_s, start=PAST_LEN)
    t = n_p + n_s

    w_router_pad = jnp.zeros((D_MODEL, LANES), f32).at[:, :N_EXPERTS].set(w_router)
    b_router_pad = jnp.zeros((1, LANES), f32).at[0, :N_EXPERTS].set(b_router)
    tri = jnp.triu(jnp.ones((ROUTE_ROWS, ROUTE_ROWS), bf16), k=1)
    vpk, idx_t, rank_t, pw, cnt = _router(h1_p, h1_s, row(norm_ffn_pre), w_router_pad.astype(bf16),
                                          b_router_pad, tri)

    n_tiles = (t * TOP_K) // FFN_ROWS + N_EXPERTS
    off, tables = _routing_tables(cnt[:, 0], n_tiles)
    pos_flat = _positions(off, idx_t, rank_t).reshape(-1)

    step_tiles = SC_WORKERS * SC_CHUNK // FFN_ROWS
    tiles_1 = (n_tiles // FIRST_SHARE) // step_tiles * step_tiles
    ys = None
    for tile0, n in ((0, tiles_1), (tiles_1, n_tiles - tiles_1)):
        xs = _sc_dispatch(vpk, pos_flat, tile0 * FFN_ROWS, n * FFN_ROWS)
        ys = _ffn(_tile_tables(tables, tile0, n), xs, n_tiles * FFN_ROWS, tile0,
                  w_gate, b_gate, w_up, b_up, w_down, b_down, prev=ys)
    fin_w = (row(norm_ffn_post), w_ple.astype(bf16), w_ple_gate.astype(bf16), row(b_ple_gate), row(norm_ple))
    a_blocks, b_blocks = n_p // FIN_ROWS, n_s // FIN_ROWS
    assert a_blocks % FIN_PARTS == 0
    per_part = a_blocks // FIN_PARTS
    out_p = None
    for part in range(FIN_PARTS):
        last = part == FIN_PARTS - 1
        t0 = part * per_part * FIN_ROWS
        n = per_part * FIN_ROWS + (n_s if last else 0)
        g = _sc_collect(ys, pos_flat, t0, n).reshape(TOP_K, n, HALF)
        out_p, out_s = _finish(g, h1_p, h1_s, pw, pp, ps_tm, *fin_w, a0=part * per_part, a_steps=per_part,
                               b_steps=b_blocks if last else 0, prev=out_p)

    states = (hfin_p, _batch_major(cfin_p, b_p), _batch_major(pfin_p, b_p),
              hfin_s, _batch_major(cfin_s, b_s), _batch_major(pfin_s, b_s))
    return out_p, out_s, states


def kernel(x_prompt, x_sample, state_rglru_h, state_rglru_conv, state_pool, p_prompt, p_sample, norm_mix_pre, w_in, conv_w, conv_b, w_rgate, b_rgate, w_igate, b_igate, lru_lambda, pool_w, pool_b, pool_scale, norm_group_a, norm_group_b, w_out, norm_mix_post, norm_ffn_pre, w_router, b_router, w_gate, b_gate, w_up, b_up, w_down, b_down, norm_ffn_post, w_ple, w_ple_gate, b_ple_gate, norm_ple):
    depth = w_in.shape[0]
    b_s = x_sample.shape[0]
    per_layer = (norm_mix_pre, w_in, conv_w, conv_b, w_rgate, b_rgate, w_igate, b_igate, lru_lambda,
                 pool_w, pool_b, pool_scale, norm_group_a, norm_group_b, w_out, norm_mix_post,
                 norm_ffn_pre, w_router, b_router, w_gate, b_gate, w_up, b_up, w_down, b_down,
                 norm_ffn_post, w_ple, w_ple_gate, b_ple_gate, norm_ple)
    hp, hs = x_prompt, _time_major(x_sample)
    collected = []
    for i in range(depth):
        hp, hs, states = _layer(hp, hs, p_prompt[i], _time_major(p_sample[i]),
                                state_rglru_h[i], state_rglru_conv[i], state_pool[i],
                                tuple(w[i] for w in per_layer))
        collected.append(states)
    stacked = tuple(jnp.stack([c[j] for c in collected]) for j in range(6))
    return (hp, _batch_major(hs, b_s)) + stacked
```

```python
import dataclasses
import functools

import jax
import jax.numpy as jnp
from jax import lax
from jax.experimental import pallas as pl
from jax.experimental.pallas import tpu as pltpu
from jax.experimental.pallas import tpu_sc as plsc

D_MODEL = 1024
W_A = 512
W_B = 512
N_HEADS_A = 8
HEAD_A = W_A // N_HEADS_A
CONV_W = 4
LRU_C = 8.0
POOL_WINDOWS = (2, 4, 8, 16)
GROUP_B = W_B // len(POOL_WINDOWS)
POOL_BUF = max(POOL_WINDOWS) - 1
N_EXPERTS = 32
TOP_K = 4
D_FF = 1024
SWIGLU_LIMIT = 7.0
SWIGLU_ALPHA = 1.702
PLE_DIM = 256
EPS = 1e-6
PAST_LEN = 16384

LANES = 128
MXU_DIM = 256
HALF = D_MODEL // 2
SC_CORES = 2
SC_SUBCORES = 16
SC_WORKERS = SC_CORES * SC_SUBCORES
SC_LANES = 16
SC_CHUNK = 64

MIX_ROWS = 512
ROUTE_ROWS = 512
FFN_ROWS = 512
FFN_SUB = 256
FIN_ROWS = 512
FIRST_SHARE = 5
FIN_PARTS = 2
VMEM_LIMIT = 56 * 1024 * 1024

f32 = jnp.float32
bf16 = jnp.bfloat16
u32 = jnp.uint32
i32 = jnp.int32


def _rms(x, g):
    return x * lax.rsqrt(jnp.mean(x * x, axis=-1, keepdims=True) + EPS) * g


def _dot(a, b):
    return jnp.dot(a, b, preferred_element_type=f32)


def _pack_rows(x):
    bits = lax.bitcast_convert_type(x.astype(bf16).astype(f32), u32)
    return (bits[:, HALF:] & jnp.uint32(0xFFFF0000)) | (bits[:, :HALF] >> 16)


def _unpack_rows(p):
    lo = lax.bitcast_convert_type(p << 16, f32)
    hi = lax.bitcast_convert_type(p & jnp.uint32(0xFFFF0000), f32)
    return jnp.concatenate([lo, hi], axis=-1)


def _mixer_kernel(x_ref, conv0_ref, pool0_ref, h0_ref, g_pre_ref, w_in_ref, conv_w_ref, conv_b_ref,
                  w_gates_ref, br_ref, bi_ref, lam_ref,
                  pool_w_ref, pool_b_ref, pool_scale_ref, ng_a_ref, ng_b_ref, w_out_ref, g_post_ref,
                  h1_ref, hfin_ref, convfin_ref, poolfin_ref,
                  h_s, conv_s, pool_s, a_s, b_s, hs_s, w_in_s, w_out_s, *, bt, tt, start):
    step = pl.program_id(0)
    rows = bt * tt

    @pl.when(step == 0)
    def _():
        h_s[...] = h0_ref[...]
        conv_s[...] = conv0_ref[...]
        pool_s[...] = pool0_ref[...]
        w_in_s[...] = w_in_ref[...].astype(bf16)
        w_out_s[...] = w_out_ref[...].astype(bf16)

    if x_ref.ndim == 3:
        x = pltpu.einshape("btd->tbd", x_ref[...]).reshape(rows, D_MODEL)
    else:
        x = x_ref[...]
    u = _rms(x, g_pre_ref[...]).astype(bf16)
    z = _dot(u, w_in_s[...])
    xa, ga, xb = z[:, :W_A], z[:, W_A:2 * W_A], z[:, 2 * W_A:]

    ext_a = jnp.concatenate([conv_s[...], xa], axis=0)
    xc = conv_b_ref[...]
    for k in range(CONV_W):
        xc = xc + ext_a[k * bt:k * bt + rows] * conv_w_ref[k:k + 1, :]
    conv_s[...] = ext_a[rows:]

    xc16 = xc.astype(bf16)
    gates = [_dot(xc16[:, j * MXU_DIM:(j + 1) * MXU_DIM], w_gates_ref[j]) for j in range(W_A // MXU_DIM)]
    r = jax.nn.sigmoid(jnp.concatenate([gj[:, :MXU_DIM] for gj in gates], axis=-1) + br_ref[...])
    ig = jax.nn.sigmoid(jnp.concatenate([gj[:, MXU_DIM:] for gj in gates], axis=-1) + bi_ref[...])
    lam = lam_ref[...]
    softplus_neg = jnp.maximum(-lam, 0.0) + jnp.log1p(jnp.exp(-jnp.abs(lam)))
    log_a = (-LRU_C) * r * softplus_neg
    a_s[...] = jnp.exp(log_a)
    th = jnp.tanh(log_a)
    v = (-2.0 * th) / (1.0 - th)
    b_s[...] = (v * lax.rsqrt(jnp.maximum(v, 1.17549435e-38))) * (ig * xc)

    def scan_step(t, h):
        sl = pl.ds(pl.multiple_of(t * bt, bt), bt)
        h = a_s[sl, :] * h + b_s[sl, :]
        hs_s[sl, :] = h
        return h

    h_last = lax.fori_loop(0, tt, scan_step, h_s[...], unroll=True)
    h_s[...] = h_last
    ya = hs_s[...] * jax.nn.gelu(ga)

    ext_b = jnp.concatenate([pool_s[...], xb], axis=0)
    pool_s[...] = ext_b[rows:]
    s2 = ext_b[bt:, :] + ext_b[:-bt, :]
    s4 = s2[2 * bt:, GROUP_B:] + s2[:-2 * bt, GROUP_B:]
    s8 = s4[4 * bt:, GROUP_B:] + s4[:-4 * bt, GROUP_B:]
    s16 = s8[8 * bt:, GROUP_B:] + s8[:-8 * bt, GROUP_B:]
    wins = (s2[14 * bt:, :GROUP_B], s4[12 * bt:, :GROUP_B], s8[8 * bt:, :GROUP_B], s16)
    t_idx = lax.broadcasted_iota(i32, (rows, GROUP_B), 0) // bt
    pos1 = (t_idx + (step * tt + start + 1)).astype(f32)
    d_parts = []
    for g, w in enumerate(POOL_WINDOWS):
        cnt = jnp.minimum(jnp.float32(w), pos1)
        d_parts.append(wins[g] / cnt - xb[:, g * GROUP_B:(g + 1) * GROUP_B])
    d = jnp.concatenate(d_parts, axis=-1).astype(bf16)
    yb_parts = [_dot(d[:, j * MXU_DIM:(j + 1) * MXU_DIM], pool_w_ref[j]) for j in range(W_B // MXU_DIM)]
    yb = (jnp.concatenate(yb_parts, axis=-1) + pool_b_ref[...]) * pool_scale_ref[...]

    na = _rms(ya, ng_a_ref[...])
    nb = _rms(yb, ng_b_ref[...])
    m = _dot(jnp.concatenate([na, nb], axis=-1).astype(bf16), w_out_s[...])
    h1_ref[...] = x + _rms(m, g_post_ref[...])

    hfin_ref[...] = h_last
    convfin_ref[...] = conv_s[...]
    poolfin_ref[...] = pool_s[...]


def _full(shape):
    return pl.BlockSpec(shape, lambda i, *_: (0,) * len(shape))


def _mixer(x, conv0, pool0, h0, wts, *, bt, start):
    tt = MIX_ROWS // bt
    rows = bt * tt
    if x.ndim == 3:
        assert x.shape[0] == bt and x.shape[1] % tt == 0
        n_steps = x.shape[1] // tt
        x_spec = pl.BlockSpec((bt, tt, D_MODEL), lambda i: (0, i, 0))
    else:
        assert x.shape[0] % rows == 0
        n_steps = x.shape[0] // rows
        x_spec = pl.BlockSpec((rows, D_MODEL), lambda i: (i, 0))
    kern = functools.partial(_mixer_kernel, bt=bt, tt=tt, start=start)
    small = [conv0, pool0, h0] + list(wts)
    return pl.pallas_call(
        kern,
        grid=(n_steps,),
        in_specs=[x_spec] + [_full(a.shape) for a in small],
        out_specs=[pl.BlockSpec((rows, D_MODEL), lambda i: (i, 0)),
                   _full((bt, W_A)), _full(((CONV_W - 1) * bt, W_A)), _full((POOL_BUF * bt, W_B))],
        out_shape=[jax.ShapeDtypeStruct((n_steps * rows, D_MODEL), f32),
                   jax.ShapeDtypeStruct((bt, W_A), f32),
                   jax.ShapeDtypeStruct(((CONV_W - 1) * bt, W_A), f32),
                   jax.ShapeDtypeStruct((POOL_BUF * bt, W_B), f32)],
        scratch_shapes=[pltpu.VMEM((bt, W_A), f32),
                        pltpu.VMEM(((CONV_W - 1) * bt, W_A), f32),
                        pltpu.VMEM((POOL_BUF * bt, W_B), f32),
                        pltpu.VMEM((rows, W_A), f32),
                        pltpu.VMEM((rows, W_A), f32),
                        pltpu.VMEM((rows, W_A), f32),
                        pltpu.VMEM((D_MODEL, 2 * W_A + W_B), bf16),
                        pltpu.VMEM((W_A + W_B, D_MODEL), bf16)],
        compiler_params=pltpu.CompilerParams(dimension_semantics=("arbitrary",),
                                             vmem_limit_bytes=VMEM_LIMIT),
        name="mixer",
    )(x, *small)


def _two_groups(step, first_steps, a_ref, b_ref):
    return jnp.where(step < first_steps, a_ref[...], b_ref[...])


def _router_kernel(h1a_ref, h1b_ref, g_ref, wr_ref, br_ref, tri_ref,
                   vpk_ref, idx_ref, rank_ref, pw_ref, cnt_ref, carry_s, *, first_steps):
    step = pl.program_id(0)
    rows = h1a_ref.shape[0]

    @pl.when(step == 0)
    def _():
        carry_s[...] = jnp.zeros_like(carry_s)

    v = _rms(_two_groups(step, first_steps, h1a_ref, h1b_ref), g_ref[...])
    vpk_ref[...] = _pack_rows(v)
    logits = _dot(v.astype(bf16), wr_ref[...]) + br_ref[...]
    lt = jnp.transpose(logits)[:N_EXPERTS, :]

    eio = lax.broadcasted_iota(i32, (N_EXPERTS, rows), 0).astype(f32)
    work = lt
    vals, idxs, sels = [], [], []
    for _ in range(TOP_K):
        m = jnp.max(work, axis=0, keepdims=True)
        ik = jnp.min(jnp.where(work == m, eio, float(N_EXPERTS)), axis=0, keepdims=True)
        sel = eio == ik
        vals.append(m)
        idxs.append(ik)
        sels.append(sel)
        work = jnp.where(sel, -jnp.inf, work)
    exps = [jnp.exp(val - vals[0]) for val in vals]
    denom = exps[0] + exps[1] + exps[2] + exps[3]
    probs = [e / denom for e in exps]

    chosen = sels[0] | sels[1] | sels[2] | sels[3]
    onehot = jnp.where(chosen, 1.0, 0.0)
    before = _dot(onehot.astype(bf16), tri_ref[...])
    base = carry_s[...] + before
    ranks = [jnp.sum(jnp.where(sel, base, 0.0), axis=0, keepdims=True) for sel in sels]
    carry_s[...] = carry_s[...] + jnp.sum(onehot, axis=1, keepdims=True)

    idx_ref[...] = jnp.concatenate(idxs, axis=0).astype(i32)
    rank_ref[...] = jnp.concatenate(ranks, axis=0).astype(i32)
    pad = jnp.zeros((LANES - TOP_K, rows), f32)
    pw_ref[...] = jnp.transpose(jnp.concatenate(probs + [pad], axis=0))
    cnt_ref[...] = carry_s[:, :LANES].astype(i32)


def _group_specs(rows, width, na, nb):
    first = na // rows
    return (pl.BlockSpec((rows, width), lambda i: (jnp.minimum(i, first - 1), 0)),
            pl.BlockSpec((rows, width), lambda i: (jnp.maximum(i - first, 0), 0)))


def _router(h1a, h1b, g, wr, br, tri):
    na, nb = h1a.shape[0], h1b.shape[0]
    t = na + nb
    rows = ROUTE_ROWS
    assert na % rows == 0 and nb % rows == 0
    return pl.pallas_call(
        functools.partial(_router_kernel, first_steps=na // rows),
        grid=(t // rows,),
        in_specs=[*_group_specs(rows, D_MODEL, na, nb),
                  _full(g.shape), _full(wr.shape), _full(br.shape), _full(tri.shape)],
        out_specs=[pl.BlockSpec((rows, HALF), lambda i: (i, 0)),
                   pl.BlockSpec((TOP_K, rows), lambda i: (0, i)),
                   pl.BlockSpec((TOP_K, rows), lambda i: (0, i)),
                   pl.BlockSpec((rows, LANES), lambda i: (i, 0)),
                   _full((N_EXPERTS, LANES))],
        out_shape=[jax.ShapeDtypeStruct((t, HALF), u32),
                   jax.ShapeDtypeStruct((TOP_K, t), i32),
                   jax.ShapeDtypeStruct((TOP_K, t), i32),
                   jax.ShapeDtypeStruct((t, LANES), f32),
                   jax.ShapeDtypeStruct((N_EXPERTS, LANES), i32)],
        scratch_shapes=[pltpu.VMEM((N_EXPERTS, rows), f32)],
        compiler_params=pltpu.CompilerParams(dimension_semantics=("arbitrary",),
                                             vmem_limit_bytes=VMEM_LIMIT),
        name="router",
    )(h1a, h1b, g, wr, br, tri)


def _positions_kernel(off_ref, idx_ref, rank_ref, pos_ref):
    idx = idx_ref[...]
    pos = rank_ref[...]
    for e in range(N_EXPERTS):
        pos = pos + jnp.where(idx == e, off_ref[e], 0)
    pos_ref[...] = pos


def _positions(off, idx_t, rank_t):
    spec = pl.BlockSpec(idx_t.shape, lambda i, *_: (0, 0))
    return pl.pallas_call(
        _positions_kernel,
        grid_spec=pltpu.PrefetchScalarGridSpec(num_scalar_prefetch=1, grid=(1,),
                                               in_specs=[spec, spec], out_specs=spec),
        out_shape=jax.ShapeDtypeStruct(idx_t.shape, i32),
        name="positions",
    )(off, idx_t, rank_t)


def _sc_mesh():
    return plsc.VectorSubcoreMesh(core_axis_name="core", subcore_axis_name="subcore")


def _sc_params():
    return dataclasses.replace(pltpu.CompilerParams(), needs_layout_passes=False)


def _sc_worker():
    return lax.axis_index("subcore") * SC_CORES + lax.axis_index("core")


def _sc_copy_rows(src_hbm, idx_v, dst_hbm, dst_row0, n, rows_v, gsem, wsem):
    n_ch = n // SC_CHUNK
    n_pair = n_ch // 2

    def gather(c, b):
        o = pl.multiple_of(c * SC_CHUNK, SC_CHUNK)
        return pltpu.make_async_copy(src_hbm.at[idx_v.at[pl.ds(o, SC_CHUNK)]], rows_v.at[b], gsem.at[b])

    def write(c, b):
        o = pl.multiple_of(c * SC_CHUNK, SC_CHUNK)
        return pltpu.make_async_copy(rows_v.at[b], dst_hbm.at[pl.ds(dst_row0 + o, SC_CHUNK)], wsem.at[b])

    gather(0, 0).start()

    @pl.loop(0, n_pair)
    def _(i):
        c0 = 2 * i
        gather(c0, 0).wait()

        @pl.when(i > 0)
        def _():
            write(c0 - 1, 1).wait()

        gather(c0 + 1, 1).start()
        write(c0, 0).start()
        gather(c0 + 1, 1).wait()
        write(c0, 0).wait()

        @pl.when(c0 + 2 < n_ch)
        def _():
            gather(c0 + 2, 0).start()

        write(c0 + 1, 1).start()

    if n_ch % 2:
        gather(n_ch - 1, 0).wait()
        write(n_ch - 1, 0).start()
        write(n_ch - 1, 0).wait()
    if n_pair:
        write(2 * n_pair - 1, 1).wait()


def _sc_dispatch(vpk, pos_flat, row0, n_out):
    n_tok = vpk.shape[0]
    per_w = n_out // SC_WORKERS
    assert n_out % (SC_WORKERS * SC_CHUNK) == 0 and n_tok % SC_LANES == 0 and row0 % SC_LANES == 0

    @pl.kernel(out_type=jax.ShapeDtypeStruct((n_out, HALF), u32), mesh=_sc_mesh(),
               compiler_params=_sc_params(), name="dispatch",
               scratch_types=[pltpu.VMEM((per_w,), i32), pltpu.VMEM((n_tok,), i32),
                              pltpu.VMEM((2, SC_CHUNK, HALF), u32),
                              pltpu.SemaphoreType.DMA((2,)), pltpu.SemaphoreType.DMA((2,))])
    def k(v_hbm, p_hbm, o_hbm, src_v, pos_v, rows_v, gsem, wsem):
        out0 = _sc_worker() * per_w
        lo = row0 + out0

        @pl.loop(0, per_w // SC_LANES)
        def _(i):
            o = pl.multiple_of(i * SC_LANES, SC_LANES)
            src_v[pl.ds(o, SC_LANES)] = lax.iota(i32, SC_LANES) + lax.rem(lo + o, n_tok)

        @pl.loop(0, TOP_K)
        def _(kk):
            pltpu.sync_copy(p_hbm.at[pl.ds(kk * n_tok, n_tok)], pos_v)

            @plsc.parallel_loop(0, n_tok, step=SC_LANES, unroll=8)
            def _(o):
                p = pos_v[pl.ds(pl.multiple_of(o, SC_LANES), SC_LANES)] - lo
                mine = lax.bitcast_convert_type(p, u32) < jnp.uint32(per_w)
                tok = lax.iota(i32, SC_LANES) + o
                plsc.store_scatter(src_v, [jnp.where(mine, p, 0)], tok, mask=mine)

        _sc_copy_rows(v_hbm, src_v, o_hbm, out0, per_w, rows_v, gsem, wsem)

    return k(vpk, pos_flat)


def _sc_collect(ys, pos_flat, t0, n):
    n_tok = pos_flat.shape[0] // TOP_K
    per_w = TOP_K * n // SC_WORKERS
    per_k = SC_WORKERS // TOP_K
    assert (TOP_K * n) % (SC_WORKERS * SC_CHUNK) == 0 and t0 % 8 == 0

    @pl.kernel(out_type=jax.ShapeDtypeStruct((TOP_K * n, HALF), u32), mesh=_sc_mesh(),
               compiler_params=_sc_params(), name="collect",
               scratch_types=[pltpu.VMEM((per_w,), i32), pltpu.VMEM((2, SC_CHUNK, HALF), u32),
                              pltpu.SemaphoreType.DMA((2,)), pltpu.SemaphoreType.DMA((2,))])
    def k(y_hbm, p_hbm, o_hbm, pos_v, rows_v, gsem, wsem):
        w = _sc_worker()
        src = lax.div(w, per_k) * n_tok + t0 + lax.rem(w, per_k) * per_w
        pltpu.sync_copy(p_hbm.at[pl.ds(pl.multiple_of(src, 8), per_w)], pos_v)
        _sc_copy_rows(y_hbm, pos_v, o_hbm, w * per_w, per_w, rows_v, gsem, wsem)

    return k(ys, pos_flat)


def _ffn_kernel(te_ref, tf_ref, slot_ref, next_ref, rows_ref, nv_ref, xs_ref, wg_hbm, bg_ref, wu_hbm, bu_ref, wd_hbm, bd_ref,
                *rest):
    ys_ref, wbuf, wg_s, wu_s, wd_s, wsem = rest[-6:]
    step = pl.program_id(0)
    w_hbm = (wg_hbm, wu_hbm, wd_hbm)
    w_bf16 = (wg_s, wu_s, wd_s)

    def fetch(e, slot):
        return [pltpu.make_async_copy(w_hbm[m].at[e], wbuf.at[slot, m], wsem.at[slot, m]) for m in range(3)]

    @pl.when(step < nv_ref[0])
    def _():
        @pl.when(tf_ref[step] == 1)
        def _():
            slot = slot_ref[step]

            @pl.when(step == 0)
            def _():
                for cp in fetch(te_ref[0], slot):
                    cp.start()

            for cp in fetch(te_ref[step], slot):
                cp.wait()
            for s in range(2):
                @pl.when(slot == s)
                def _():
                    for m in range(3):
                        w_bf16[m][...] = wbuf[s, m].astype(bf16)

            @pl.when(next_ref[step] >= 0)
            def _():
                for cp in fetch(next_ref[step], 1 - slot):
                    cp.start()

        def rows_block(rb):
            rs = slice(rb * FFN_SUB, (rb + 1) * FFN_SUB)
            x = _unpack_rows(xs_ref[rs, :]).astype(bf16)
            g = jnp.minimum(_dot(x, wg_s[...]) + bg_ref[0], SWIGLU_LIMIT)
            u = jnp.clip(_dot(x, wu_s[...]) + bu_ref[0], -SWIGLU_LIMIT, SWIGLU_LIMIT)
            hid = (u + 1.0) * (g * jax.nn.sigmoid(SWIGLU_ALPHA * g))
            y = _dot(hid.astype(bf16), wd_s[...]) + bd_ref[0]
            ys_ref[rs, :] = _pack_rows(y)

        @pl.when(rows_ref[step] > FFN_SUB)
        def _():
            rows_block(0)
            rows_block(1)

        @pl.when(rows_ref[step] <= FFN_SUB)
        def _():
            rows_block(0)
            ys_ref[FFN_SUB:, :] = jnp.zeros((FFN_ROWS - FFN_SUB, HALF), u32)

    @pl.when(step >= nv_ref[0])
    def _():
        ys_ref[...] = jnp.zeros_like(ys_ref)


def _ffn(tables, xs, n_rows_total, tile0, w_gate, b_gate, w_up, b_up, w_down, b_down, prev=None):
    n_tiles = xs.shape[0] // FFN_ROWS
    assert D_FF == D_MODEL
    assert FFN_ROWS == 2 * FFN_SUB

    def row_map(i, te, tf, sl, nx, rw, nv):
        return (jnp.minimum(i, jnp.maximum(nv[0] - 1, 0)), 0)

    def b_map(i, te, tf, sl, nx, rw, nv):
        return (te[i], 0, 0)

    w_spec = pl.BlockSpec(memory_space=pl.ANY)
    b_spec = pl.BlockSpec((1, 1, D_FF), b_map)
    in_specs = [pl.BlockSpec((FFN_ROWS, HALF), row_map), w_spec, b_spec, w_spec, b_spec, w_spec, b_spec]
    args = [xs, w_gate, b_gate.reshape(N_EXPERTS, 1, D_FF), w_up, b_up.reshape(N_EXPERTS, 1, D_FF),
            w_down, b_down.reshape(N_EXPERTS, 1, D_MODEL)]
    aliases = {}
    if prev is not None:
        in_specs.append(pl.BlockSpec(memory_space=pl.ANY))
        aliases = {len(tables) + len(args): 0}
        args.append(prev)
    return pl.pallas_call(
        _ffn_kernel,
        grid_spec=pltpu.PrefetchScalarGridSpec(
            num_scalar_prefetch=6,
            grid=(n_tiles,),
            in_specs=in_specs,
            out_specs=pl.BlockSpec((FFN_ROWS, HALF), lambda i, *_: (tile0 + i, 0)),
            scratch_shapes=[pltpu.VMEM((2, 3, D_MODEL, D_FF), f32),
                            pltpu.VMEM((D_MODEL, D_FF), bf16),
                            pltpu.VMEM((D_MODEL, D_FF), bf16),
                            pltpu.VMEM((D_FF, D_MODEL), bf16),
                            pltpu.SemaphoreType.DMA((2, 3))]),
        out_shape=jax.ShapeDtypeStruct((n_rows_total, HALF), u32),
        input_output_aliases=aliases,
        compiler_params=pltpu.CompilerParams(dimension_semantics=("arbitrary",),
                                             vmem_limit_bytes=VMEM_LIMIT),
        name="experts",
    )(*tables, *args)


def _finish_kernel(g_ref, h1a_ref, h1b_ref, pw_ref, pa_ref, pb_ref, g_post_ref, w_ple_ref, w_pg_ref, b_pg_ref,
                   g_ple_ref, *rest, first_steps):
    outa_ref, outb_ref, w_ple_s, w_pg_s = rest[-4:]
    step = pl.program_id(0)
    rows = h1a_ref.shape[0]

    @pl.when(step == 0)
    def _():
        w_ple_s[...] = w_ple_ref[...].astype(bf16)
        w_pg_s[...] = w_pg_ref[...].astype(bf16)

    pw = pw_ref[...]
    f = jnp.zeros((rows, D_MODEL), f32)
    for k in range(TOP_K):
        f = f + pw[:, k:k + 1] * _unpack_rows(g_ref[k])
    h2 = _two_groups(step, first_steps, h1a_ref, h1b_ref) + _rms(f, g_post_ref[...])
    gate = jax.nn.sigmoid(_dot(h2.astype(bf16), w_pg_s[...]) + b_pg_ref[...])
    bt, tt = pa_ref.shape[0], pa_ref.shape[1]
    pin_a = pltpu.einshape("btd->tbd", pa_ref[...]).reshape(rows, PLE_DIM)
    pin = jnp.where(step < first_steps, pin_a, pb_ref[...]).astype(bf16)
    pe = _dot(pin, w_ple_s[...]) * gate
    out = h2 + _rms(pe, g_ple_ref[...])

    @pl.when(step < first_steps)
    def _():
        outa_ref[...] = pltpu.einshape("tbd->btd", out.reshape(tt, bt, D_MODEL))

    @pl.when(step >= first_steps)
    def _():
        outb_ref[...] = out


def _finish(g, h1a, h1b, pw, pa, pb, g_post, w_ple, w_pg, b_pg, g_ple, *, a0, a_steps, b_steps, prev=None):
    na, nb = h1a.shape[0], h1b.shape[0]
    rows = FIN_ROWS
    bt = pa.shape[0]
    tt = rows // bt
    assert na % rows == 0 and nb % rows == 0 and a_steps >= 1 and rows % bt == 0

    def spec_a(width):
        return pl.BlockSpec((rows, width), lambda i: (a0 + jnp.minimum(i, a_steps - 1), 0))

    def spec_a3(width):
        return pl.BlockSpec((bt, tt, width), lambda i: (0, a0 + jnp.minimum(i, a_steps - 1), 0))

    def spec_b(width):
        return pl.BlockSpec((rows, width), lambda i: (jnp.maximum(i - a_steps, 0), 0))

    in_specs = [pl.BlockSpec((TOP_K, rows, HALF), lambda i: (0, i, 0)),
                spec_a(D_MODEL), spec_b(D_MODEL),
                pl.BlockSpec((rows, LANES), lambda i: (a0 + i, 0)),
                spec_a3(PLE_DIM), spec_b(PLE_DIM),
                _full(g_post.shape), _full(w_ple.shape), _full(w_pg.shape),
                _full(b_pg.shape), _full(g_ple.shape)]
    args = [g, h1a, h1b, pw, pa, pb, g_post, w_ple, w_pg, b_pg, g_ple]
    aliases = {}
    if prev is not None:
        in_specs.append(pl.BlockSpec(memory_space=pl.ANY))
        aliases = {len(args): 0}
        args.append(prev)
    return pl.pallas_call(
        functools.partial(_finish_kernel, first_steps=a_steps),
        grid=(a_steps + b_steps,),
        in_specs=in_specs,
        out_specs=[spec_a3(D_MODEL), spec_b(D_MODEL)],
        out_shape=[jax.ShapeDtypeStruct((bt, na // bt, D_MODEL), f32), jax.ShapeDtypeStruct((nb, D_MODEL), f32)],
        scratch_shapes=[pltpu.VMEM(w_ple.shape, bf16), pltpu.VMEM(w_pg.shape, bf16)],
        input_output_aliases=aliases,
        compiler_params=pltpu.CompilerParams(dimension_semantics=("arbitrary",),
                                             vmem_limit_bytes=VMEM_LIMIT),
        name="finish",
    )(*args)


def _block_diag_pairs(w):
    n_blocks, n = w.shape[0], w.shape[1]
    per = MXU_DIM // n
    w4 = w.reshape(n_blocks // per, per, n, n)
    blocks = jnp.where(jnp.eye(per, dtype=bool)[None, :, None, :, None], w4[:, :, :, None, :], 0.0)
    return blocks.reshape(n_blocks // per, MXU_DIM, MXU_DIM)


def _time_major(x):
    x = jnp.swapaxes(x, 0, 1)
    return x.reshape((x.shape[0] * x.shape[1],) + x.shape[2:])


def _batch_major(x, b):
    return jnp.swapaxes(x.reshape(x.shape[0] // b, b, x.shape[1]), 0, 1)


def _routing_tables(counts, n_tiles):
    tiles = (counts + FFN_ROWS - 1) // FFN_ROWS
    tile_ends = jnp.cumsum(tiles)
    off = (tile_ends - tiles) * FFN_ROWS
    n_valid = tile_ends[-1:]
    tile_ids = jnp.minimum(jnp.arange(n_tiles, dtype=i32), n_valid - 1)
    tile_e = jnp.sum((tile_ids[:, None] >= tile_ends[None, :]).astype(i32), axis=1)
    cand = jnp.where(tiles > 0, jnp.arange(N_EXPERTS, dtype=i32), N_EXPERTS)
    suffix_min = lax.cummin(cand[::-1])[::-1]
    nxt = jnp.concatenate([suffix_min[1:], jnp.full((1,), N_EXPERTS, i32)])
    onehot = (tile_e[:, None] == jnp.arange(N_EXPERTS, dtype=i32)[None, :]).astype(i32)
    at_tile = lambda per_expert: jnp.sum(onehot * per_expert[None, :].astype(i32), axis=1)
    tile_next = at_tile(jnp.where(nxt < N_EXPERTS, nxt, -1))
    tile_rows = jnp.clip(at_tile(counts) - (tile_ids - at_tile(tile_ends - tiles)) * FFN_ROWS, 0, FFN_ROWS)
    tile_group_end = at_tile(tile_ends)
    return off.astype(i32), (tile_e.astype(i32), tile_next.astype(i32), tile_group_end.astype(i32),
                             tile_rows.astype(i32), n_valid.astype(i32))


def _tile_tables(full, tile0, n):
    tile_e, tile_next, tile_group_end, tile_rows = (a[tile0:tile0 + n] for a in full[:4])
    n_valid = full[4]
    first = jnp.concatenate([jnp.ones((1,), i32), (tile_e[1:] != tile_e[:-1]).astype(i32)])
    slot = (jnp.cumsum(first) - 1) % 2
    nxt = jnp.where(tile_group_end < tile0 + n, tile_next, -1)
    return (tile_e, first, slot.astype(i32), nxt.astype(i32), tile_rows, jnp.clip(n_valid - tile0, 0, n))


def _layer(xp, xs_tm, pp, ps_tm, state_h, state_conv, state_pool, lw):
    (norm_mix_pre, w_in, conv_w, conv_b, w_rgate, b_rgate, w_igate, b_igate, lru_lambda,
     pool_w, pool_b, pool_scale, norm_group_a, norm_group_b, w_out, norm_mix_post,
     norm_ffn_pre, w_router, b_router, w_gate, b_gate, w_up, b_up, w_down, b_down, norm_ffn_post,
     w_ple, w_ple_gate, b_ple_gate, norm_ple) = lw
    b_p, s_p = xp.shape[0], xp.shape[1]
    n_p, n_s = b_p * s_p, xs_tm.shape[0]
    b_s = state_h.shape[0]
    row = lambda a: a.reshape(1, -1)

    w_gates = jnp.concatenate([_block_diag_pairs(w_rgate), _block_diag_pairs(w_igate)], axis=-1).astype(bf16)
    mix_w = (row(norm_mix_pre), w_in, conv_w, row(conv_b),
             w_gates, row(b_rgate), row(b_igate), row(lru_lambda),
             _block_diag_pairs(pool_w).astype(bf16), row(pool_b), row(pool_scale), row(norm_group_a), row(norm_group_b),
             w_out, row(norm_mix_post))

    zeros = lambda *s: jnp.zeros(s, f32)
    h1_p, hfin_p, cfin_p, pfin_p = _mixer(
        xp, zeros((CONV_W - 1) * b_p, W_A), zeros(POOL_BUF * b_p, W_B), zeros(b_p, W_A),
        mix_w, bt=b_p, start=0)
    h1_s, hfin_s, cfin_s, pfin_s = _mixer(
        xs_tm, _time_major(state_conv), _time_major(state_pool), state_h,
        mix_w, bt=b_s, start=PAST_LEN)
    t = n_p + n_s

    w_router_pad = jnp.zeros((D_MODEL, LANES), f32).at[:, :N_EXPERTS].set(w_router)
    b_router_pad = jnp.zeros((1, LANES), f32).at[0, :N_EXPERTS].set(b_router)
    tri = jnp.triu(jnp.ones((ROUTE_ROWS, ROUTE_ROWS), bf16), k=1)
    vpk, idx_t, rank_t, pw, cnt = _router(h1_p, h1_s, row(norm_ffn_pre), w_router_pad.astype(bf16),
                                          b_router_pad, tri)

    n_tiles = (t * TOP_K) // FFN_ROWS + N_EXPERTS
    off, tables = _routing_tables(cnt[:, 0], n_tiles)
    pos_flat = _positions(off, idx_t, rank_t).reshape(-1)

    step_tiles = SC_WORKERS * SC_CHUNK // FFN_ROWS
    tiles_1 = (n_tiles // FIRST_SHARE) // step_tiles * step_tiles
    ys = None
    for tile0, n in ((0, tiles_1), (tiles_1, n_tiles - tiles_1)):
        xs = _sc_dispatch(vpk, pos_flat, tile0 * FFN_ROWS, n * FFN_ROWS)
        ys = _ffn(_tile_tables(tables, tile0, n), xs, n_tiles * FFN_ROWS, tile0,
                  w_gate, b_gate, w_up, b_up, w_down, b_down, prev=ys)
    fin_w = (row(norm_ffn_post), w_ple, w_ple_gate, row(b_ple_gate), row(norm_ple))
    a_blocks, b_blocks = n_p // FIN_ROWS, n_s // FIN_ROWS
    assert a_blocks % FIN_PARTS == 0
    per_part = a_blocks // FIN_PARTS
    out_p = None
    for part in range(FIN_PARTS):
        last = part == FIN_PARTS - 1
        t0 = part * per_part * FIN_ROWS
        n = per_part * FIN_ROWS + (n_s if last else 0)
        g = _sc_collect(ys, pos_flat, t0, n).reshape(TOP_K, n, HALF)
        out_p, out_s = _finish(g, h1_p, h1_s, pw, pp, ps_tm, *fin_w, a0=part * per_part, a_steps=per_part,
                               b_steps=b_blocks if last else 0, prev=out_p)

    states = (hfin_p, _batch_major(cfin_p, b_p), _batch_major(pfin_p, b_p),
              hfin_s, _batch_major(cfin_s, b_s), _batch_major(pfin_s, b_s))
    return out_p, out_s, states


def kernel(x_prompt, x_sample, state_rglru_h, state_rglru_conv, state_pool, p_prompt, p_sample, norm_mix_pre, w_in, conv_w, conv_b, w_rgate, b_rgate, w_igate, b_igate, lru_lambda, pool_w, pool_b, pool_scale, norm_group_a, norm_group_b, w_out, norm_mix_post, norm_ffn_pre, w_router, b_router, w_gate, b_gate, w_up, b_up, w_down, b_down, norm_ffn_post, w_ple, w_ple_gate, b_ple_gate, norm_ple):
    depth = w_in.shape[0]
    b_s = x_sample.shape[0]
    per_layer = (norm_mix_pre, w_in, conv_w, conv_b, w_rgate, b_rgate, w_igate, b_igate, lru_lambda,
                 pool_w, pool_b, pool_scale, norm_group_a, norm_group_b, w_out, norm_mix_post,
                 norm_ffn_pre, w_router, b_router, w_gate, b_gate, w_up, b_up, w_down, b_down,
                 norm_ffn_post, w_ple, w_ple_gate, b_ple_gate, norm_ple)
    hp, hs = x_prompt, _time_major(x_sample)
    collected = []
    for i in range(depth):
        hp, hs, states = _layer(hp, hs, p_prompt[i], _time_major(p_sample[i]),
                                state_rglru_h[i], state_rglru_conv[i], state_pool[i],
                                tuple(w[i] for w in per_layer))
        collected.append(states)
    stacked = tuple(jnp.stack([c[j] for c in collected]) for j in range(6))
    return (hp, _batch_major(hs, b_s)) + stacked
```

```python
import dataclasses
import functools

import jax
import jax.numpy as jnp
from jax import lax
from jax.experimental import pallas as pl
from jax.experimental.pallas import tpu as pltpu
from jax.experimental.pallas import tpu_sc as plsc

D_MODEL = 1024
W_A = 512
W_B = 512
N_HEADS_A = 8
HEAD_A = W_A // N_HEADS_A
CONV_W = 4
LRU_C = 8.0
POOL_WINDOWS = (2, 4, 8, 16)
GROUP_B = W_B // len(POOL_WINDOWS)
POOL_BUF = max(POOL_WINDOWS) - 1
N_EXPERTS = 32
TOP_K = 4
D_FF = 1024
SWIGLU_LIMIT = 7.0
SWIGLU_ALPHA = 1.702
PLE_DIM = 256
EPS = 1e-6
PAST_LEN = 16384

LANES = 128
MXU_DIM = 256
HALF = D_MODEL // 2
SC_CORES = 2
SC_SUBCORES = 16
SC_WORKERS = SC_CORES * SC_SUBCORES
SC_LANES = 16
SC_CHUNK = 64

MIX_ROWS = 512
ROUTE_ROWS = 512
FFN_ROWS = 512
FFN_SUB = 256
FIN_ROWS = 512
FIRST_SHARE = 5
FIN_PARTS = 2
VMEM_LIMIT = 56 * 1024 * 1024

f32 = jnp.float32
bf16 = jnp.bfloat16
u32 = jnp.uint32
i32 = jnp.int32


def _rms(x, g):
    return x * lax.rsqrt(jnp.mean(x * x, axis=-1, keepdims=True) + EPS) * g


def _dot(a, b):
    return jnp.dot(a, b, preferred_element_type=f32)


def _pack_rows(x):
    bits = lax.bitcast_convert_type(x.astype(bf16).astype(f32), u32)
    return (bits[:, HALF:] & jnp.uint32(0xFFFF0000)) | (bits[:, :HALF] >> 16)


def _unpack_rows(p):
    lo = lax.bitcast_convert_type(p << 16, f32)
    hi = lax.bitcast_convert_type(p & jnp.uint32(0xFFFF0000), f32)
    return jnp.concatenate([lo, hi], axis=-1)


def _mixer_kernel(x_ref, conv0_ref, pool0_ref, h0_ref, g_pre_ref, w_in_ref, conv_w_ref, conv_b_ref,
                  w_gates_ref, br_ref, bi_ref, lam_ref,
                  pool_w_ref, pool_b_ref, pool_scale_ref, ng_a_ref, ng_b_ref, w_out_ref, g_post_ref,
                  h1_ref, hfin_ref, convfin_ref, poolfin_ref,
                  h_s, conv_s, pool_s, a_s, b_s, hs_s, *, bt, tt, start):
    step = pl.program_id(0)
    rows = bt * tt

    @pl.when(step == 0)
    def _():
        h_s[...] = h0_ref[...]
        conv_s[...] = conv0_ref[...]
        pool_s[...] = pool0_ref[...]

    if x_ref.ndim == 3:
        x = pltpu.einshape("btd->tbd", x_ref[...]).reshape(rows, D_MODEL)
    else:
        x = x_ref[...]
    u = _rms(x, g_pre_ref[...]).astype(bf16)
    z = _dot(u, w_in_ref[...])
    xa, ga, xb = z[:, :W_A], z[:, W_A:2 * W_A], z[:, 2 * W_A:]

    ext_a = jnp.concatenate([conv_s[...], xa], axis=0)
    xc = conv_b_ref[...]
    for k in range(CONV_W):
        xc = xc + ext_a[k * bt:k * bt + rows] * conv_w_ref[k:k + 1, :]
    conv_s[...] = ext_a[rows:]

    xc16 = xc.astype(bf16)
    gates = [_dot(xc16[:, j * MXU_DIM:(j + 1) * MXU_DIM], w_gates_ref[j]) for j in range(W_A // MXU_DIM)]
    r = jax.nn.sigmoid(jnp.concatenate([gj[:, :MXU_DIM] for gj in gates], axis=-1) + br_ref[...])
    ig = jax.nn.sigmoid(jnp.concatenate([gj[:, MXU_DIM:] for gj in gates], axis=-1) + bi_ref[...])
    lam = lam_ref[...]
    softplus_neg = jnp.maximum(-lam, 0.0) + jnp.log1p(jnp.exp(-jnp.abs(lam)))
    log_a = (-LRU_C) * r * softplus_neg
    a_s[...] = jnp.exp(log_a)
    th = jnp.tanh(log_a)
    v = (-2.0 * th) / (1.0 - th)
    b_s[...] = (v * lax.rsqrt(jnp.maximum(v, 1.17549435e-38))) * (ig * xc)

    def scan_step(t, h):
        sl = pl.ds(pl.multiple_of(t * bt, bt), bt)
        h = a_s[sl, :] * h + b_s[sl, :]
        hs_s[sl, :] = h
        return h

    h_last = lax.fori_loop(0, tt, scan_step, h_s[...], unroll=True)
    h_s[...] = h_last
    ya = hs_s[...] * jax.nn.gelu(ga)

    ext_b = jnp.concatenate([pool_s[...], xb], axis=0)
    pool_s[...] = ext_b[rows:]
    s2 = ext_b[bt:, :] + ext_b[:-bt, :]
    s4 = s2[2 * bt:, GROUP_B:] + s2[:-2 * bt, GROUP_B:]
    s8 = s4[4 * bt:, GROUP_B:] + s4[:-4 * bt, GROUP_B:]
    s16 = s8[8 * bt:, GROUP_B:] + s8[:-8 * bt, GROUP_B:]
    wins = (s2[14 * bt:, :GROUP_B], s4[12 * bt:, :GROUP_B], s8[8 * bt:, :GROUP_B], s16)
    t_idx = lax.broadcasted_iota(i32, (rows, GROUP_B), 0) // bt
    pos1 = (t_idx + (step * tt + start + 1)).astype(f32)
    d_parts = []
    for g, w in enumerate(POOL_WINDOWS):
        cnt = jnp.minimum(jnp.float32(w), pos1)
        d_parts.append(wins[g] / cnt - xb[:, g * GROUP_B:(g + 1) * GROUP_B])
    d = jnp.concatenate(d_parts, axis=-1).astype(bf16)
    yb_parts = [_dot(d[:, j * MXU_DIM:(j + 1) * MXU_DIM], pool_w_ref[j]) for j in range(W_B // MXU_DIM)]
    yb = (jnp.concatenate(yb_parts, axis=-1) + pool_b_ref[...]) * pool_scale_ref[...]

    na = _rms(ya, ng_a_ref[...])
    nb = _rms(yb, ng_b_ref[...])
    m = _dot(jnp.concatenate([na, nb], axis=-1).astype(bf16), w_out_ref[...])
    h1_ref[...] = x + _rms(m, g_post_ref[...])

    hfin_ref[...] = h_last
    convfin_ref[...] = conv_s[...]
    poolfin_ref[...] = pool_s[...]


def _full(shape):
    return pl.BlockSpec(shape, lambda i, *_: (0,) * len(shape))


def _mixer(x, conv0, pool0, h0, wts, *, bt, start):
    tt = MIX_ROWS // bt
    rows = bt * tt
    if x.ndim == 3:
        assert x.shape[0] == bt and x.shape[1] % tt == 0
        n_steps = x.shape[1] // tt
        x_spec = pl.BlockSpec((bt, tt, D_MODEL), lambda i: (0, i, 0))
    else:
        assert x.shape[0] % rows == 0
        n_steps = x.shape[0] // rows
        x_spec = pl.BlockSpec((rows, D_MODEL), lambda i: (i, 0))
    kern = functools.partial(_mixer_kernel, bt=bt, tt=tt, start=start)
    small = [conv0, pool0, h0] + list(wts)
    return pl.pallas_call(
        kern,
        grid=(n_steps,),
        in_specs=[x_spec] + [_full(a.shape) for a in small],
        out_specs=[pl.BlockSpec((rows, D_MODEL), lambda i: (i, 0)),
                   _full((bt, W_A)), _full(((CONV_W - 1) * bt, W_A)), _full((POOL_BUF * bt, W_B))],
        out_shape=[jax.ShapeDtypeStruct((n_steps * rows, D_MODEL), f32),
                   jax.ShapeDtypeStruct((bt, W_A), f32),
                   jax.ShapeDtypeStruct(((CONV_W - 1) * bt, W_A), f32),
                   jax.ShapeDtypeStruct((POOL_BUF * bt, W_B), f32)],
        scratch_shapes=[pltpu.VMEM((bt, W_A), f32),
                        pltpu.VMEM(((CONV_W - 1) * bt, W_A), f32),
                        pltpu.VMEM((POOL_BUF * bt, W_B), f32),
                        pltpu.VMEM((rows, W_A), f32),
                        pltpu.VMEM((rows, W_A), f32),
                        pltpu.VMEM((rows, W_A), f32)],
        compiler_params=pltpu.CompilerParams(dimension_semantics=("arbitrary",),
                                             vmem_limit_bytes=VMEM_LIMIT),
        name="mixer",
    )(x, *small)


def _two_groups(step, first_steps, a_ref, b_ref):
    return jnp.where(step < first_steps, a_ref[...], b_ref[...])


def _router_kernel(h1a_ref, h1b_ref, g_ref, wr_ref, br_ref, tri_ref,
                   vpk_ref, idx_ref, rank_ref, pw_ref, cnt_ref, carry_s, *, first_steps):
    step = pl.program_id(0)
    rows = h1a_ref.shape[0]

    @pl.when(step == 0)
    def _():
        carry_s[...] = jnp.zeros_like(carry_s)

    v = _rms(_two_groups(step, first_steps, h1a_ref, h1b_ref), g_ref[...])
    vpk_ref[...] = _pack_rows(v)
    logits = _dot(v.astype(bf16), wr_ref[...]) + br_ref[...]
    lt = jnp.transpose(logits)[:N_EXPERTS, :]

    eio = lax.broadcasted_iota(i32, (N_EXPERTS, rows), 0).astype(f32)
    work = lt
    vals, idxs, sels = [], [], []
    for _ in range(TOP_K):
        m = jnp.max(work, axis=0, keepdims=True)
        ik = jnp.min(jnp.where(work == m, eio, float(N_EXPERTS)), axis=0, keepdims=True)
        sel = eio == ik
        vals.append(m)
        idxs.append(ik)
        sels.append(sel)
        work = jnp.where(sel, -jnp.inf, work)
    exps = [jnp.exp(val - vals[0]) for val in vals]
    denom = exps[0] + exps[1] + exps[2] + exps[3]
    probs = [e / denom for e in exps]

    chosen = sels[0] | sels[1] | sels[2] | sels[3]
    onehot = jnp.where(chosen, 1.0, 0.0)
    before = _dot(onehot.astype(bf16), tri_ref[...])
    base = carry_s[...] + before
    ranks = [jnp.sum(jnp.where(sel, base, 0.0), axis=0, keepdims=True) for sel in sels]
    carry_s[...] = carry_s[...] + jnp.sum(onehot, axis=1, keepdims=True)

    idx_ref[...] = jnp.concatenate(idxs, axis=0).astype(i32)
    rank_ref[...] = jnp.concatenate(ranks, axis=0).astype(i32)
    pad = jnp.zeros((LANES - TOP_K, rows), f32)
    pw_ref[...] = jnp.transpose(jnp.concatenate(probs + [pad], axis=0))
    cnt_ref[...] = carry_s[:, :LANES].astype(i32)


def _group_specs(rows, width, na, nb):
    first = na // rows
    return (pl.BlockSpec((rows, width), lambda i: (jnp.minimum(i, first - 1), 0)),
            pl.BlockSpec((rows, width), lambda i: (jnp.maximum(i - first, 0), 0)))


def _router(h1a, h1b, g, wr, br, tri):
    na, nb = h1a.shape[0], h1b.shape[0]
    t = na + nb
    rows = ROUTE_ROWS
    assert na % rows == 0 and nb % rows == 0
    return pl.pallas_call(
        functools.partial(_router_kernel, first_steps=na // rows),
        grid=(t // rows,),
        in_specs=[*_group_specs(rows, D_MODEL, na, nb),
                  _full(g.shape), _full(wr.shape), _full(br.shape), _full(tri.shape)],
        out_specs=[pl.BlockSpec((rows, HALF), lambda i: (i, 0)),
                   pl.BlockSpec((TOP_K, rows), lambda i: (0, i)),
                   pl.BlockSpec((TOP_K, rows), lambda i: (0, i)),
                   pl.BlockSpec((rows, LANES), lambda i: (i, 0)),
                   _full((N_EXPERTS, LANES))],
        out_shape=[jax.ShapeDtypeStruct((t, HALF), u32),
                   jax.ShapeDtypeStruct((TOP_K, t), i32),
                   jax.ShapeDtypeStruct((TOP_K, t), i32),
                   jax.ShapeDtypeStruct((t, LANES), f32),
                   jax.ShapeDtypeStruct((N_EXPERTS, LANES), i32)],
        scratch_shapes=[pltpu.VMEM((N_EXPERTS, rows), f32)],
        compiler_params=pltpu.CompilerParams(dimension_semantics=("arbitrary",),
                                             vmem_limit_bytes=VMEM_LIMIT),
        name="router",
    )(h1a, h1b, g, wr, br, tri)


def _positions_kernel(off_ref, idx_ref, rank_ref, pos_ref):
    idx = idx_ref[...]
    pos = rank_ref[...]
    for e in range(N_EXPERTS):
        pos = pos + jnp.where(idx == e, off_ref[e], 0)
    pos_ref[...] = pos


def _positions(off, idx_t, rank_t):
    spec = pl.BlockSpec(idx_t.shape, lambda i, *_: (0, 0))
    return pl.pallas_call(
        _positions_kernel,
        grid_spec=pltpu.PrefetchScalarGridSpec(num_scalar_prefetch=1, grid=(1,),
                                               in_specs=[spec, spec], out_specs=spec),
        out_shape=jax.ShapeDtypeStruct(idx_t.shape, i32),
        name="positions",
    )(off, idx_t, rank_t)


def _sc_mesh():
    return plsc.VectorSubcoreMesh(core_axis_name="core", subcore_axis_name="subcore")


def _sc_params():
    return dataclasses.replace(pltpu.CompilerParams(), needs_layout_passes=False)


def _sc_worker():
    return lax.axis_index("subcore") * SC_CORES + lax.axis_index("core")


def _sc_copy_rows(src_hbm, idx_v, dst_hbm, dst_row0, n, rows_v, gsem, wsem):
    n_ch = n // SC_CHUNK
    n_pair = n_ch // 2

    def gather(c, b):
        o = pl.multiple_of(c * SC_CHUNK, SC_CHUNK)
        return pltpu.make_async_copy(src_hbm.at[idx_v.at[pl.ds(o, SC_CHUNK)]], rows_v.at[b], gsem.at[b])

    def write(c, b):
        o = pl.multiple_of(c * SC_CHUNK, SC_CHUNK)
        return pltpu.make_async_copy(rows_v.at[b], dst_hbm.at[pl.ds(dst_row0 + o, SC_CHUNK)], wsem.at[b])

    gather(0, 0).start()

    @pl.loop(0, n_pair)
    def _(i):
        c0 = 2 * i
        gather(c0, 0).wait()

        @pl.when(i > 0)
        def _():
            write(c0 - 1, 1).wait()

        gather(c0 + 1, 1).start()
        write(c0, 0).start()
        gather(c0 + 1, 1).wait()
        write(c0, 0).wait()

        @pl.when(c0 + 2 < n_ch)
        def _():
            gather(c0 + 2, 0).start()

        write(c0 + 1, 1).start()

    if n_ch % 2:
        gather(n_ch - 1, 0).wait()
        write(n_ch - 1, 0).start()
        write(n_ch - 1, 0).wait()
    if n_pair:
        write(2 * n_pair - 1, 1).wait()


def _sc_dispatch(vpk, pos_flat, row0, n_out):
    n_tok = vpk.shape[0]
    per_w = n_out // SC_WORKERS
    assert n_out % (SC_WORKERS * SC_CHUNK) == 0 and n_tok % SC_LANES == 0 and row0 % SC_LANES == 0

    @pl.kernel(out_type=jax.ShapeDtypeStruct((n_out, HALF), u32), mesh=_sc_mesh(),
               compiler_params=_sc_params(), name="dispatch",
               scratch_types=[pltpu.VMEM((per_w,), i32), pltpu.VMEM((n_tok,), i32),
                              pltpu.VMEM((2, SC_CHUNK, HALF), u32),
                              pltpu.SemaphoreType.DMA((2,)), pltpu.SemaphoreType.DMA((2,))])
    def k(v_hbm, p_hbm, o_hbm, src_v, pos_v, rows_v, gsem, wsem):
        out0 = _sc_worker() * per_w
        lo = row0 + out0

        @pl.loop(0, per_w // SC_LANES)
        def _(i):
            o = pl.multiple_of(i * SC_LANES, SC_LANES)
            src_v[pl.ds(o, SC_LANES)] = lax.iota(i32, SC_LANES) + lax.rem(lo + o, n_tok)

        @pl.loop(0, TOP_K)
        def _(kk):
            pltpu.sync_copy(p_hbm.at[pl.ds(kk * n_tok, n_tok)], pos_v)

            @plsc.parallel_loop(0, n_tok, step=SC_LANES, unroll=8)
            def _(o):
                p = pos_v[pl.ds(pl.multiple_of(o, SC_LANES), SC_LANES)] - lo
                mine = lax.bitcast_convert_type(p, u32) < jnp.uint32(per_w)
                tok = lax.iota(i32, SC_LANES) + o
                plsc.store_scatter(src_v, [jnp.where(mine, p, 0)], tok, mask=mine)

        _sc_copy_rows(v_hbm, src_v, o_hbm, out0, per_w, rows_v, gsem, wsem)

    return k(vpk, pos_flat)


def _sc_collect(ys, pos_flat, t0, n):
    n_tok = pos_flat.shape[0] // TOP_K
    per_w = TOP_K * n // SC_WORKERS
    per_k = SC_WORKERS // TOP_K
    assert (TOP_K * n) % (SC_WORKERS * SC_CHUNK) == 0 and t0 % 8 == 0

    @pl.kernel(out_type=jax.ShapeDtypeStruct((TOP_K * n, HALF), u32), mesh=_sc_mesh(),
               compiler_params=_sc_params(), name="collect",
               scratch_types=[pltpu.VMEM((per_w,), i32), pltpu.VMEM((2, SC_CHUNK, HALF), u32),
                              pltpu.SemaphoreType.DMA((2,)), pltpu.SemaphoreType.DMA((2,))])
    def k(y_hbm, p_hbm, o_hbm, pos_v, rows_v, gsem, wsem):
        w = _sc_worker()
        src = lax.div(w, per_k) * n_tok + t0 + lax.rem(w, per_k) * per_w
        pltpu.sync_copy(p_hbm.at[pl.ds(pl.multiple_of(src, 8), per_w)], pos_v)
        _sc_copy_rows(y_hbm, pos_v, o_hbm, w * per_w, per_w, rows_v, gsem, wsem)

    return k(ys, pos_flat)


def _ffn_kernel(te_ref, tf_ref, slot_ref, next_ref, rows_ref, nv_ref, xs_ref, wg_hbm, bg_ref, wu_hbm, bu_ref, wd_hbm, bd_ref,
                *rest):
    ys_ref, wbuf, wg_s, wu_s, wd_s, wsem = rest[-6:]
    step = pl.program_id(0)
    w_hbm = (wg_hbm, wu_hbm, wd_hbm)
    w_bf16 = (wg_s, wu_s, wd_s)

    def fetch(e, slot):
        return [pltpu.make_async_copy(w_hbm[m].at[e], wbuf.at[slot, m], wsem.at[slot, m]) for m in range(3)]

    @pl.when(step < nv_ref[0])
    def _():
        @pl.when(tf_ref[step] == 1)
        def _():
            slot = slot_ref[step]

            @pl.when(step == 0)
            def _():
                for cp in fetch(te_ref[0], slot):
                    cp.start()

            for cp in fetch(te_ref[step], slot):
                cp.wait()
            for s in range(2):
                @pl.when(slot == s)
                def _():
                    for m in range(3):
                        w_bf16[m][...] = wbuf[s, m].astype(bf16)

            @pl.when(next_ref[step] >= 0)
            def _():
                for cp in fetch(next_ref[step], 1 - slot):
                    cp.start()

        def rows_block(rb):
            rs = slice(rb * FFN_SUB, (rb + 1) * FFN_SUB)
            x = _unpack_rows(xs_ref[rs, :]).astype(bf16)
            g = jnp.minimum(_dot(x, wg_s[...]) + bg_ref[0], SWIGLU_LIMIT)
            u = jnp.clip(_dot(x, wu_s[...]) + bu_ref[0], -SWIGLU_LIMIT, SWIGLU_LIMIT)
            hid = (u + 1.0) * (g * jax.nn.sigmoid(SWIGLU_ALPHA * g))
            y = _dot(hid.astype(bf16), wd_s[...]) + bd_ref[0]
            ys_ref[rs, :] = _pack_rows(y)

        @pl.when(rows_ref[step] > FFN_SUB)
        def _():
            rows_block(0)
            rows_block(1)

        @pl.when(rows_ref[step] <= FFN_SUB)
        def _():
            rows_block(0)
            ys_ref[FFN_SUB:, :] = jnp.zeros((FFN_ROWS - FFN_SUB, HALF), u32)

    @pl.when(step >= nv_ref[0])
    def _():
        ys_ref[...] = jnp.zeros_like(ys_ref)


def _ffn(tables, xs, n_rows_total, tile0, w_gate, b_gate, w_up, b_up, w_down, b_down, prev=None):
    n_tiles = xs.shape[0] // FFN_ROWS
    assert D_FF == D_MODEL
    assert FFN_ROWS == 2 * FFN_SUB

    def row_map(i, te, tf, sl, nx, rw, nv):
        return (jnp.minimum(i, jnp.maximum(nv[0] - 1, 0)), 0)

    def b_map(i, te, tf, sl, nx, rw, nv):
        return (te[i], 0, 0)

    w_spec = pl.BlockSpec(memory_space=pl.ANY)
    b_spec = pl.BlockSpec((1, 1, D_FF), b_map)
    in_specs = [pl.BlockSpec((FFN_ROWS, HALF), row_map), w_spec, b_spec, w_spec, b_spec, w_spec, b_spec]
    args = [xs, w_gate, b_gate.reshape(N_EXPERTS, 1, D_FF), w_up, b_up.reshape(N_EXPERTS, 1, D_FF),
            w_down, b_down.reshape(N_EXPERTS, 1, D_MODEL)]
    aliases = {}
    if prev is not None:
        in_specs.append(pl.BlockSpec(memory_space=pl.ANY))
        aliases = {len(tables) + len(args): 0}
        args.append(prev)
    return pl.pallas_call(
        _ffn_kernel,
        grid_spec=pltpu.PrefetchScalarGridSpec(
            num_scalar_prefetch=6,
            grid=(n_tiles,),
            in_specs=in_specs,
            out_specs=pl.BlockSpec((FFN_ROWS, HALF), lambda i, *_: (tile0 + i, 0)),
            scratch_shapes=[pltpu.VMEM((2, 3, D_MODEL, D_FF), f32),
                            pltpu.VMEM((D_MODEL, D_FF), bf16),
                            pltpu.VMEM((D_MODEL, D_FF), bf16),
                            pltpu.VMEM((D_FF, D_MODEL), bf16),
                            pltpu.SemaphoreType.DMA((2, 3))]),
        out_shape=jax.ShapeDtypeStruct((n_rows_total, HALF), u32),
        input_output_aliases=aliases,
        compiler_params=pltpu.CompilerParams(dimension_semantics=("arbitrary",),
                                             vmem_limit_bytes=VMEM_LIMIT),
        name="experts",
    )(*tables, *args)


def _finish_kernel(g_ref, h1a_ref, h1b_ref, pw_ref, pa_ref, pb_ref, g_post_ref, w_ple_ref, w_pg_ref, b_pg_ref,
                   g_ple_ref, *rest, first_steps):
    outa_ref, outb_ref = rest[-2:]
    step = pl.program_id(0)
    rows = h1a_ref.shape[0]
    pw = pw_ref[...]
    f = jnp.zeros((rows, D_MODEL), f32)
    for k in range(TOP_K):
        f = f + pw[:, k:k + 1] * _unpack_rows(g_ref[k])
    h2 = _two_groups(step, first_steps, h1a_ref, h1b_ref) + _rms(f, g_post_ref[...])
    gate = jax.nn.sigmoid(_dot(h2.astype(bf16), w_pg_ref[...]) + b_pg_ref[...])
    bt, tt = pa_ref.shape[0], pa_ref.shape[1]
    pin_a = pltpu.einshape("btd->tbd", pa_ref[...]).reshape(rows, PLE_DIM)
    pin = jnp.where(step < first_steps, pin_a, pb_ref[...]).astype(bf16)
    pe = _dot(pin, w_ple_ref[...]) * gate
    out = h2 + _rms(pe, g_ple_ref[...])

    @pl.when(step < first_steps)
    def _():
        outa_ref[...] = pltpu.einshape("tbd->btd", out.reshape(tt, bt, D_MODEL))

    @pl.when(step >= first_steps)
    def _():
        outb_ref[...] = out


def _finish(g, h1a, h1b, pw, pa, pb, g_post, w_ple, w_pg, b_pg, g_ple, *, a0, a_steps, b_steps, prev=None):
    na, nb = h1a.shape[0], h1b.shape[0]
    rows = FIN_ROWS
    bt = pa.shape[0]
    tt = rows // bt
    assert na % rows == 0 and nb % rows == 0 and a_steps >= 1 and rows % bt == 0

    def spec_a(width):
        return pl.BlockSpec((rows, width), lambda i: (a0 + jnp.minimum(i, a_steps - 1), 0))

    def spec_a3(width):
        return pl.BlockSpec((bt, tt, width), lambda i: (0, a0 + jnp.minimum(i, a_steps - 1), 0))

    def spec_b(width):
        return pl.BlockSpec((rows, width), lambda i: (jnp.maximum(i - a_steps, 0), 0))

    in_specs = [pl.BlockSpec((TOP_K, rows, HALF), lambda i: (0, i, 0)),
                spec_a(D_MODEL), spec_b(D_MODEL),
                pl.BlockSpec((rows, LANES), lambda i: (a0 + i, 0)),
                spec_a3(PLE_DIM), spec_b(PLE_DIM),
                _full(g_post.shape), _full(w_ple.shape), _full(w_pg.shape),
                _full(b_pg.shape), _full(g_ple.shape)]
    args = [g, h1a, h1b, pw, pa, pb, g_post, w_ple, w_pg, b_pg, g_ple]
    aliases = {}
    if prev is not None:
        in_specs.append(pl.BlockSpec(memory_space=pl.ANY))
        aliases = {len(args): 0}
        args.append(prev)
    return pl.pallas_call(
        functools.partial(_finish_kernel, first_steps=a_steps),
        grid=(a_steps + b_steps,),
        in_specs=in_specs,
        out_specs=[spec_a3(D_MODEL), spec_b(D_MODEL)],
        out_shape=[jax.ShapeDtypeStruct((bt, na // bt, D_MODEL), f32), jax.ShapeDtypeStruct((nb, D_MODEL), f32)],
        input_output_aliases=aliases,
        compiler_params=pltpu.CompilerParams(dimension_semantics=("arbitrary",),
                                             vmem_limit_bytes=VMEM_LIMIT),
        name="finish",
    )(*args)


def _block_diag_pairs(w):
    n_blocks, n = w.shape[0], w.shape[1]
    per = MXU_DIM // n
    w4 = w.reshape(n_blocks // per, per, n, n)
    blocks = jnp.where(jnp.eye(per, dtype=bool)[None, :, None, :, None], w4[:, :, :, None, :], 0.0)
    return blocks.reshape(n_blocks // per, MXU_DIM, MXU_DIM)


def _time_major(x):
    x = jnp.swapaxes(x, 0, 1)
    return x.reshape((x.shape[0] * x.shape[1],) + x.shape[2:])


def _batch_major(x, b):
    return jnp.swapaxes(x.reshape(x.shape[0] // b, b, x.shape[1]), 0, 1)


def _routing_tables(counts, n_tiles):
    tiles = (counts + FFN_ROWS - 1) // FFN_ROWS
    tile_ends = jnp.cumsum(tiles)
    off = (tile_ends - tiles) * FFN_ROWS
    n_valid = tile_ends[-1:]
    tile_ids = jnp.minimum(jnp.arange(n_tiles, dtype=i32), n_valid - 1)
    tile_e = jnp.sum((tile_ids[:, None] >= tile_ends[None, :]).astype(i32), axis=1)
    cand = jnp.where(tiles > 0, jnp.arange(N_EXPERTS, dtype=i32), N_EXPERTS)
    suffix_min = lax.cummin(cand[::-1])[::-1]
    nxt = jnp.concatenate([suffix_min[1:], jnp.full((1,), N_EXPERTS, i32)])
    onehot = (tile_e[:, None] == jnp.arange(N_EXPERTS, dtype=i32)[None, :]).astype(i32)
    at_tile = lambda per_expert: jnp.sum(onehot * per_expert[None, :].astype(i32), axis=1)
    tile_next = at_tile(jnp.where(nxt < N_EXPERTS, nxt, -1))
    tile_rows = jnp.clip(at_tile(counts) - (tile_ids - at_tile(tile_ends - tiles)) * FFN_ROWS, 0, FFN_ROWS)
    tile_group_end = at_tile(tile_ends)
    return off.astype(i32), (tile_e.astype(i32), tile_next.astype(i32), tile_group_end.astype(i32),
                             tile_rows.astype(i32), n_valid.astype(i32))


def _tile_tables(full, tile0, n):
    tile_e, tile_next, tile_group_end, tile_rows = (a[tile0:tile0 + n] for a in full[:4])
    n_valid = full[4]
    first = jnp.concatenate([jnp.ones((1,), i32), (tile_e[1:] != tile_e[:-1]).astype(i32)])
    slot = (jnp.cumsum(first) - 1) % 2
    nxt = jnp.where(tile_group_end < tile0 + n, tile_next, -1)
    return (tile_e, first, slot.astype(i32), nxt.astype(i32), tile_rows, jnp.clip(n_valid - tile0, 0, n))


def _layer(xp, xs_tm, pp, ps_tm, state_h, state_conv, state_pool, lw):
    (norm_mix_pre, w_in, conv_w, conv_b, w_rgate, b_rgate, w_igate, b_igate, lru_lambda,
     pool_w, pool_b, pool_scale, norm_group_a, norm_group_b, w_out, norm_mix_post,
     norm_ffn_pre, w_router, b_router, w_gate, b_gate, w_up, b_up, w_down, b_down, norm_ffn_post,
     w_ple, w_ple_gate, b_ple_gate, norm_ple) = lw
    b_p, s_p = xp.shape[0], xp.shape[1]
    n_p, n_s = b_p * s_p, xs_tm.shape[0]
    b_s = state_h.shape[0]
    row = lambda a: a.reshape(1, -1)

    w_gates = jnp.concatenate([_block_diag_pairs(w_rgate), _block_diag_pairs(w_igate)], axis=-1).astype(bf16)
    mix_w = (row(norm_mix_pre), w_in.astype(bf16), conv_w, row(conv_b),
             w_gates, row(b_rgate), row(b_igate), row(lru_lambda),
             _block_diag_pairs(pool_w).astype(bf16), row(pool_b), row(pool_scale), row(norm_group_a), row(norm_group_b),
             w_out.astype(bf16), row(norm_mix_post))

    zeros = lambda *s: jnp.zeros(s, f32)
    h1_p, hfin_p, cfin_p, pfin_p = _mixer(
        xp, zeros((CONV_W - 1) * b_p, W_A), zeros(POOL_BUF * b_p, W_B), zeros(b_p, W_A),
        mix_w, bt=b_p, start=0)
    h1_s, hfin_s, cfin_s, pfin_s = _mixer(
        xs_tm, _time_major(state_conv), _time_major(state_pool), state_h,
        mix_w, bt=b_s, start=PAST_LEN)
    t = n_p + n_s

    w_router_pad = jnp.zeros((D_MODEL, LANES), f32).at[:, :N_EXPERTS].set(w_router)
    b_router_pad = jnp.zeros((1, LANES), f32).at[0, :N_EXPERTS].set(b_router)
    tri = jnp.triu(jnp.ones((ROUTE_ROWS, ROUTE_ROWS), bf16), k=1)
    vpk, idx_t, rank_t, pw, cnt = _router(h1_p, h1_s, row(norm_ffn_pre), w_router_pad.astype(bf16),
                                          b_router_pad, tri)

    n_tiles = (t * TOP_K) // FFN_ROWS + N_EXPERTS
    off, tables = _routing_tables(cnt[:, 0], n_tiles)
    pos_flat = _positions(off, idx_t, rank_t).reshape(-1)

    step_tiles = SC_WORKERS * SC_CHUNK // FFN_ROWS
    tiles_1 = (n_tiles // FIRST_SHARE) // step_tiles * step_tiles
    ys = None
    for tile0, n in ((0, tiles_1), (tiles_1, n_tiles - tiles_1)):
        xs = _sc_dispatch(vpk, pos_flat, tile0 * FFN_ROWS, n * FFN_ROWS)
        ys = _ffn(_tile_tables(tables, tile0, n), xs, n_tiles * FFN_ROWS, tile0,
                  w_gate, b_gate, w_up, b_up, w_down, b_down, prev=ys)
    fin_w = (row(norm_ffn_post), w_ple.astype(bf16), w_ple_gate.astype(bf16), row(b_ple_gate), row(norm_ple))
    a_blocks, b_blocks = n_p // FIN_ROWS, n_s // FIN_ROWS
    assert a_blocks % FIN_PARTS == 0
    per_part = a_blocks // FIN_PARTS
    out_p = None
    for part in range(FIN_PARTS):
        last = part == FIN_PARTS - 1
        t0 = part * per_part * FIN_ROWS
        n = per_part * FIN_ROWS + (n_s if last else 0)
        g = _sc_collect(ys, pos_flat, t0, n).reshape(TOP_K, n, HALF)
        out_p, out_s = _finish(g, h1_p, h1_s, pw, pp, ps_tm, *fin_w, a0=part * per_part, a_steps=per_part,
                               b_steps=b_blocks if last else 0, prev=out_p)

    states = (hfin_p, _batch_major(cfin_p, b_p), _batch_major(pfin_p, b_p),
              hfin_s, _batch_major(cfin_s, b_s), _batch_major(pfin_s, b_s))
    return out_p, out_s, states


def kernel(x_prompt, x_sample, state_rglru_h, state_rglru_conv, state_pool, p_prompt, p_sample, norm_mix_pre, w_in, conv_w, conv_b, w_rgate, b_rgate, w_igate, b_igate, lru_lambda, pool_w, pool_b, pool_scale, norm_group_a, norm_group_b, w_out, norm_mix_post, norm_ffn_pre, w_router, b_router, w_gate, b_gate, w_up, b_up, w_down, b_down, norm_ffn_post, w_ple, w_ple_gate, b_ple_gate, norm_ple):
    depth = w_in.shape[0]
    b_s = x_sample.shape[0]
    per_layer = (norm_mix_pre, w_in, conv_w, conv_b, w_rgate, b_rgate, w_igate, b_igate, lru_lambda,
                 pool_w, pool_b, pool_scale, norm_group_a, norm_group_b, w_out, norm_mix_post,
                 norm_ffn_pre, w_router, b_router, w_gate, b_gate, w_up, b_up, w_down, b_down,
                 norm_ffn_post, w_ple, w_ple_gate, b_ple_gate, norm_ple)
    hp, hs = x_prompt, _time_major(x_sample)
    collected = []
    for i in range(depth):
        hp, hs, states = _layer(hp, hs, p_prompt[i], _time_major(p_sample[i]),
                                state_rglru_h[i], state_rglru_conv[i], state_pool[i],
                                tuple(w[i] for w in per_layer))
        collected.append(states)
    stacked = tuple(jnp.stack([c[j] for c in collected]) for j in range(6))
    return (hp, _batch_major(hs, b_s)) + stacked
```
